```python
import math
import jax, jax.numpy as jnp
from jax import lax
import numpy as np

D_MODEL = 2048
BATCH = 1
SEQ = 16384
DEPTH = 1
DEC_BATCH = 16
DEC_SEQ = 32
PAST_LEN = 4096

CHUNK = 64
BLOCK_Q = 128
N_HEADS = 16
Q_RANK = 512
KV_RANK = 512
QK_NOPE = 128
QK_ROPE = 64
QK_HEAD = QK_NOPE + QK_ROPE
V_HEAD = 128
CONV_CH = 1024
CONV_WIDTH = 31
D_FF = -(-8 * D_MODEL // (3 * 256)) * 256
ROPE_THETA = 10000.0
EPS = 1e-6
NEG_INF = -1e30
SCALE = QK_HEAD ** -0.5
IN_SPLITS = (Q_RANK, Q_RANK + KV_RANK, Q_RANK + KV_RANK + QK_ROPE,
             Q_RANK + KV_RANK + QK_ROPE + 2 * CONV_CH)
IN_DIM = Q_RANK + KV_RANK + QK_ROPE + 2 * CONV_CH + 2 * D_MODEL

kernel_name = 'mla_conformer_gated_streaming_step'


def _rms_norm(x, g):
    xf = x.astype(jnp.float32)
    y = xf * lax.rsqrt(jnp.mean(xf * xf, axis=-1, keepdims=True) + EPS)
    return (y * g.astype(jnp.float32)).astype(x.dtype)


def _layer_norm(x, g, b):
    xf = x.astype(jnp.float32)
    xc = xf - jnp.mean(xf, axis=-1, keepdims=True)
    y = xc * lax.rsqrt(jnp.mean(xc * xc, axis=-1, keepdims=True) + EPS)
    return (y * g.astype(jnp.float32) + b.astype(jnp.float32)).astype(x.dtype)


def _rope_angles(pos):
    inv_freq = 1.0 / (ROPE_THETA ** (jnp.arange(0, QK_ROPE, 2, dtype=jnp.float32) / QK_ROPE))
    ang = pos.astype(jnp.float32)[:, None] * inv_freq[None, :]
    return jnp.cos(ang), jnp.sin(ang)


def _apply_rope(x, cos, sin):
    half = QK_ROPE // 2
    xf = x.astype(jnp.float32)
    x1, x2 = xf[..., :half], xf[..., half:]
    return jnp.concatenate([x1 * cos - x2 * sin, x2 * cos + x1 * sin], axis=-1).astype(x.dtype)


def _queries(c_q, cos, sin, g_q_a, w_q_up, g_q_norm):
    B, S, _ = c_q.shape
    q = (_rms_norm(c_q, g_q_a) @ w_q_up).reshape(B, S, N_HEADS, QK_HEAD)
    q = jnp.concatenate([q[..., :QK_NOPE],
                         _apply_rope(q[..., QK_NOPE:], cos[:, None, :], sin[:, None, :])], axis=-1)
    return _rms_norm(q, g_q_norm)


def _keys_values(c_kv, k_pe, w_kv_up, g_k_norm):
    B, L, _ = c_kv.shape
    kv = (c_kv @ w_kv_up).reshape(B, L, N_HEADS, QK_NOPE + V_HEAD)
    k_rope = jnp.broadcast_to(k_pe[:, :, None, :], (B, L, N_HEADS, QK_ROPE))
    k = jnp.concatenate([kv[..., :QK_NOPE], k_rope], axis=-1)
    return _rms_norm(k, g_k_norm), kv[..., QK_NOPE:]


def _attend_prompt(q, k, v):
    B, S, H, Dh = q.shape
    nb = S // BLOCK_Q
    qb = jnp.moveaxis(q.reshape(B, nb, BLOCK_Q, H, Dh), 1, 0)
    kpos = jnp.arange(S)

    def one_block(args):
        i, qi = args
        qpos = i * BLOCK_Q + jnp.arange(BLOCK_Q)
        limit = (qpos // CHUNK + 1) * CHUNK
        s = jnp.einsum('bqhd,bkhd->bhqk', qi, k, preferred_element_type=jnp.float32) * SCALE
        s = jnp.where(kpos[None, :] < limit[:, None], s, NEG_INF)
        p = jax.nn.softmax(s, axis=-1).astype(v.dtype)
        return jnp.einsum('bhqk,bkhd->bqhd', p, v)

    out = lax.map(one_block, (jnp.arange(nb), qb))
    return jnp.moveaxis(out, 0, 1).reshape(B, S, H * V_HEAD)


def _attend_past(q, k, v):
    B, T, H, _ = q.shape
    s = jnp.einsum('bqhd,bkhd->bhqk', q, k, preferred_element_type=jnp.float32) * SCALE
    p = jax.nn.softmax(s, axis=-1).astype(v.dtype)
    return jnp.einsum('bhqk,bkhd->bqhd', p, v).reshape(B, T, H * V_HEAD)


def _conv_branch(glu, past, b_glu, w_dw, b_dw, g_ln, b_ln, w_pw, b_pw):
    a, g = jnp.split(glu + b_glu, 2, axis=-1)
    u = a * jax.nn.sigmoid(g)
    padded = jnp.concatenate([past.astype(u.dtype), u], axis=1)
    y = lax.conv_general_dilated(padded, w_dw[:, None, :], window_strides=(1,), padding='VALID',
                                 dimension_numbers=('NWC', 'WIO', 'NWC'),
                                 feature_group_count=CONV_CH) + b_dw
    y = jax.nn.silu(_layer_norm(y, g_ln, b_ln))
    return y @ w_pw + b_pw, padded[:, -(CONV_WIDTH - 1):, :]


def _layer(x, pos, past_ckv, past_kpe, past_conv, lw):
    (g_mix_norm, w_in, b_glu, b_gate, g_q_a, w_q_up, g_q_norm, g_kv_a, w_kv_up, g_k_norm,
     w_attn_out, w_dw, b_dw, g_conv_ln, b_conv_ln, w_conv_out, b_conv_out, w_out,
     g_ffn_norm, w_ffn_gate, w_ffn_up, w_ffn_down) = lw
    B = x.shape[0]
    h = _rms_norm(x, g_mix_norm)
    c_q, c_kv, k_pe, glu, gates = jnp.split(h @ w_in, IN_SPLITS, axis=-1)
    cos, sin = _rope_angles(pos)
    c_kv = _rms_norm(c_kv, g_kv_a)
    k_pe = _apply_rope(k_pe, cos, sin)
    q = _queries(c_q, cos, sin, g_q_a, w_q_up, g_q_norm)
    if past_ckv is None:
        k, v = _keys_values(c_kv, k_pe, w_kv_up, g_k_norm)
        attn = _attend_prompt(q, k, v)
        past_conv = jnp.zeros((B, CONV_WIDTH - 1, CONV_CH), x.dtype)
    else:
        k, v = _keys_values(jnp.concatenate([past_ckv.astype(c_kv.dtype), c_kv], axis=1),
                            jnp.concatenate([past_kpe.astype(k_pe.dtype), k_pe], axis=1),
                            w_kv_up, g_k_norm)
        attn = _attend_past(q, k, v)
    y_a = attn @ w_attn_out
    y_b, conv_state = _conv_branch(glu, past_conv, b_glu, w_dw, b_dw, g_conv_ln, b_conv_ln,
                                   w_conv_out, b_conv_out)
    g_a, g_b = jnp.split(jax.nn.sigmoid(gates + b_gate), 2, axis=-1)
    x = x + (g_a * y_a + g_b * y_b) @ w_out
    h2 = _rms_norm(x, g_ffn_norm)
    x = x + (jax.nn.silu(h2 @ w_ffn_gate) * (h2 @ w_ffn_up)) @ w_ffn_down
    return x, c_kv, k_pe, conv_state


def _normal(k, shape, scale):
    return scale * jax.random.normal(k, shape, jnp.float32)


def setup_inputs(seed: int = 0) -> dict:
    key = jax.random.key(seed)
    ks = jax.random.split(key, 32)
    L = DEPTH
    HQK = N_HEADS * QK_HEAD
    HKV = N_HEADS * (QK_NOPE + V_HEAD)
    HV = N_HEADS * V_HEAD
    return {
        'x_prompt': _normal(ks[0], (BATCH, SEQ, D_MODEL), 1.0),
        'x_sample': _normal(ks[1], (DEC_BATCH, DEC_SEQ, D_MODEL), 1.0),
        'cache_ckv': _normal(ks[2], (L, DEC_BATCH, PAST_LEN, KV_RANK), 1.0),
        'cache_kpe': _normal(ks[3], (L, DEC_BATCH, PAST_LEN, QK_ROPE), 1.0),
        'state_conv': _normal(ks[4], (L, DEC_BATCH, CONV_WIDTH - 1, CONV_CH), 0.5),
        'g_mix_norm': 1.0 + _normal(ks[5], (L, D_MODEL), 0.02),
        'w_in': _normal(ks[6], (L, D_MODEL, IN_DIM), D_MODEL ** -0.5),
        'b_glu': _normal(ks[7], (L, 2 * CONV_CH), 0.02),
        'b_gate': _normal(ks[8], (L, 2 * D_MODEL), 0.02),
        'g_q_a': 1.0 + _normal(ks[9], (L, Q_RANK), 0.02),
        'w_q_up': _normal(ks[10], (L, Q_RANK, HQK), Q_RANK ** -0.5),
        'g_q_norm': 1.0 + _normal(ks[11], (L, QK_HEAD), 0.02),
        'g_kv_a': 1.0 + _normal(ks[12], (L, KV_RANK), 0.02),
        'w_kv_up': _normal(ks[13], (L, KV_RANK, HKV), KV_RANK ** -0.5),
        'g_k_norm': 1.0 + _normal(ks[14], (L, QK_HEAD), 0.02),
        'w_attn_out': _normal(ks[15], (L, HV, D_MODEL), HV ** -0.5),
        'w_dw': _normal(ks[16], (L, CONV_WIDTH, CONV_CH), CONV_WIDTH ** -0.5),
        'b_dw': _normal(ks[17], (L, CONV_CH), 0.02),
        'g_conv_ln': 1.0 + _normal(ks[18], (L, CONV_CH), 0.02),
        'b_conv_ln': _normal(ks[19], (L, CONV_CH), 0.02),
        'w_conv_out': _normal(ks[20], (L, CONV_CH, D_MODEL), CONV_CH ** -0.5),
        'b_conv_out': _normal(ks[21], (L, D_MODEL), 0.02),
        'w_out': _normal(ks[22], (L, D_MODEL, D_MODEL), D_MODEL ** -0.5),
        'g_ffn_norm': 1.0 + _normal(ks[23], (L, D_MODEL), 0.02),
        'w_ffn_gate': _normal(ks[24], (L, D_MODEL, D_FF), D_MODEL ** -0.5),
        'w_ffn_up': _normal(ks[25], (L, D_MODEL, D_FF), D_MODEL ** -0.5),
        'w_ffn_down': _normal(ks[26], (L, D_FF, D_MODEL), D_FF ** -0.5),
    }


def reference(x_prompt, x_sample, cache_ckv, cache_kpe, state_conv, g_mix_norm, w_in, b_glu,
              b_gate, g_q_a, w_q_up, g_q_norm, g_kv_a, w_kv_up, g_k_norm, w_attn_out, w_dw,
              b_dw, g_conv_ln, b_conv_ln, w_conv_out, b_conv_out, w_out, g_ffn_norm,
              w_ffn_gate, w_ffn_up, w_ffn_down):
    weights = (g_mix_norm, w_in, b_glu, b_gate, g_q_a, w_q_up, g_q_norm, g_kv_a, w_kv_up,
               g_k_norm, w_attn_out, w_dw, b_dw, g_conv_ln, b_conv_ln, w_conv_out, b_conv_out,
               w_out, g_ffn_norm, w_ffn_gate, w_ffn_up, w_ffn_down)
    pos_prompt = jnp.arange(x_prompt.shape[1])
    pos_sample = cache_ckv.shape[2] + jnp.arange(x_sample.shape[1])
    y_prompt, y_sample = x_prompt, x_sample
    ckv_p, kpe_p, conv_p, ckv_s, kpe_s, conv_s = [], [], [], [], [], []
    for l in range(DEPTH):
        lw = tuple(w[l] for w in weights)
        y_prompt, c_kv, k_pe, conv_state = _layer(y_prompt, pos_prompt, None, None, None, lw)
        ckv_p.append(c_kv)
        kpe_p.append(k_pe)
        conv_p.append(conv_state)
        y_sample, c_kv, k_pe, conv_state = _layer(y_sample, pos_sample, cache_ckv[l],
                                                  cache_kpe[l], state_conv[l], lw)
        ckv_s.append(c_kv)
        kpe_s.append(k_pe)
        conv_s.append(conv_state)
    return (y_prompt, y_sample, jnp.stack(ckv_p), jnp.stack(kpe_p), jnp.stack(conv_p),
            jnp.stack(ckv_s), jnp.stack(kpe_s), jnp.stack(conv_s))
```

```python
import functools
import math

import jax
import jax.numpy as jnp
from jax import lax
from jax.experimental import pallas as pl
from jax.experimental.pallas import tpu as pltpu

F32 = jnp.float32
BF16 = jnp.bfloat16

CHUNK = 64
N_HEADS = 16
Q_RANK = 512
KV_RANK = 512
QK_NOPE = 128
QK_ROPE = 64
QK_HEAD = QK_NOPE + QK_ROPE
V_HEAD = 128
CONV_CH = 1024
CONV_WIDTH = 31
ROPE_THETA = 10000.0
EPS = 1e-6
NEG_INF = -1e30
SCALE = QK_HEAD ** -0.5
LOG2E = math.log2(math.e)

LANE = 128
QK_PAD = 2 * LANE
N_PAIRS = N_HEADS // 2
PAST_PAD = 32
MIB = 1024 * 1024


def _params(semantics, vmem_mib):
    return pltpu.CompilerParams(dimension_semantics=semantics, vmem_limit_bytes=vmem_mib * MIB)


def _rms_scale(v, n):
    return lax.rsqrt(jnp.sum(v * v, axis=-1, keepdims=True) * (1.0 / n) + EPS)


def _half_masks():
    lane = lax.broadcasted_iota(jnp.int32, (1, LANE), 1)
    lo = (lane < QK_ROPE).astype(F32)
    return lo, 1.0 - lo


def _inproj_kernel(x_ref, g_ref, w_ref, gq_ref, gkv_ref, c2_ref, s2_ref,
                   h_ref, cq_ref, ckv_ref, ckvb_ref, kpe_ref, kpe2_ref):
    x = x_ref[...]
    h = (x * _rms_scale(x, x.shape[-1]) * g_ref[...]).astype(BF16)
    h_ref[...] = h
    z = jnp.dot(h, w_ref[...], preferred_element_type=F32)
    cq = z[:, :Q_RANK]
    cq_ref[...] = (cq * _rms_scale(cq, Q_RANK) * gq_ref[...]).astype(BF16)
    ckv = z[:, Q_RANK:Q_RANK + KV_RANK]
    ckv = ckv * _rms_scale(ckv, KV_RANK) * gkv_ref[...]
    ckv_ref[...] = ckv
    ckvb_ref[...] = ckv.astype(BF16)
    base = Q_RANK + KV_RANK
    kpe2 = z[:, base:base + LANE] * c2_ref[...] + z[:, base + LANE:base + 2 * LANE] * s2_ref[...]
    kpe2_ref[...] = kpe2
    kpe_ref[...] = kpe2[:, :QK_ROPE]


def _inproj(x2d, c2, s2, W, tm):
    T, D = x2d.shape
    n_in = W['w_small'].shape[1]
    row = lambda i: (i, 0)
    const = lambda i: (0, 0)
    return pl.pallas_call(
        _inproj_kernel,
        grid=(T // tm,),
        in_specs=[
            pl.BlockSpec((tm, D), row),
            pl.BlockSpec((1, D), const),
            pl.BlockSpec((D, n_in), const),
            pl.BlockSpec((1, Q_RANK), const),
            pl.BlockSpec((1, KV_RANK), const),
            pl.BlockSpec((tm, LANE), row),
            pl.BlockSpec((tm, LANE), row),
        ],
        out_specs=[
            pl.BlockSpec((tm, D), row),
            pl.BlockSpec((tm, Q_RANK), row),
            pl.BlockSpec((tm, KV_RANK), row),
            pl.BlockSpec((tm, KV_RANK), row),
            pl.BlockSpec((tm, QK_ROPE), row),
            pl.BlockSpec((tm, LANE), row),
        ],
        out_shape=[
            jax.ShapeDtypeStruct((T, D), BF16),
            jax.ShapeDtypeStruct((T, Q_RANK), BF16),
            jax.ShapeDtypeStruct((T, KV_RANK), F32),
            jax.ShapeDtypeStruct((T, KV_RANK), BF16),
            jax.ShapeDtypeStruct((T, QK_ROPE), F32),
            jax.ShapeDtypeStruct((T, LANE), F32),
        ],
        compiler_params=_params(("arbitrary",), 48),
        name="inproj",
    )(x2d, W['g_mix'], W['w_small'], W['g_q_a'], W['g_kv_a'], c2, s2)


def _qup_kernel(cq_ref, w_ref, c2_ref, s2_ref, gn_ref, gr2_ref, q_ref):
    z = jnp.dot(cq_ref[...], w_ref[...], preferred_element_type=F32)
    rope2 = z[:, 2 * LANE:3 * LANE] * c2_ref[...] + z[:, 3 * LANE:] * s2_ref[...]
    for e, msk in enumerate(_half_masks()):
        nope = z[:, e * LANE:(e + 1) * LANE]
        rope = rope2 * msk
        ss = jnp.sum(nope * nope, axis=-1, keepdims=True) + jnp.sum(rope * rope, axis=-1, keepdims=True)
        r = lax.rsqrt(ss * (1.0 / QK_HEAD) + EPS) * (SCALE * LOG2E)
        q_ref[e, :, :QK_NOPE] = (nope * r * gn_ref[...]).astype(BF16)
        q_ref[e, :, QK_NOPE:] = (rope * r * gr2_ref[...]).astype(BF16)


def _q_up(cq, c2, s2, W, tm):
    T = cq.shape[0]
    return pl.pallas_call(
        _qup_kernel,
        grid=(T // tm, N_PAIRS),
        in_specs=[
            pl.BlockSpec((tm, Q_RANK), lambda i, p: (i, 0)),
            pl.BlockSpec((Q_RANK, 4 * LANE), lambda i, p: (0, p)),
            pl.BlockSpec((tm, LANE), lambda i, p: (i, 0)),
            pl.BlockSpec((tm, LANE), lambda i, p: (i, 0)),
            pl.BlockSpec((1, LANE), lambda i, p: (0, 0)),
            pl.BlockSpec((1, LANE), lambda i, p: (0, 0)),
        ],
        out_specs=pl.BlockSpec((2, tm, QK_PAD), lambda i, p: (p, i, 0)),
        out_shape=jax.ShapeDtypeStruct((N_HEADS, T, QK_PAD), BF16),
        compiler_params=_params(("arbitrary", "arbitrary"), 32),
        name="q_up",
    )(cq, W['w_q'], c2, s2, W['g_q_nope'], W['g_q_rope2'])


def _pair_keys_values(kv, kpe2, gn, gr2):
    lo, hi = _half_masks()
    ss_pe = jnp.sum(kpe2 * kpe2 * lo, axis=-1, keepdims=True)
    out = []
    for e, msk in enumerate((lo, hi)):
        kn = kv[:, e * LANE:(e + 1) * LANE]
        r = lax.rsqrt((jnp.sum(kn * kn, axis=-1, keepdims=True) + ss_pe) * (1.0 / QK_HEAD) + EPS)
        k = jnp.concatenate([kn * r * gn, kpe2 * msk * r * gr2], axis=-1).astype(BF16)
        v = kv[:, (2 + e) * LANE:(3 + e) * LANE].astype(BF16)
        out.append((k, v))
    return out


def _kvup_kernel(ckv_ref, kpe2_ref, w_ref, gn_ref, gr2_ref, k_ref, v_ref):
    kv = jnp.dot(ckv_ref[...], w_ref[0], preferred_element_type=F32)
    for e, (k, v) in enumerate(_pair_keys_values(kv, kpe2_ref[...], gn_ref[...], gr2_ref[...])):
        k_ref[e] = k
        v_ref[e] = v


def _kv_up(ckvb, kpe2, W, tm):
    T = ckvb.shape[0]
    return pl.pallas_call(
        _kvup_kernel,
        grid=(T // tm, N_PAIRS),
        in_specs=[
            pl.BlockSpec((tm, KV_RANK), lambda i, p: (i, 0)),
            pl.BlockSpec((tm, LANE), lambda i, p: (i, 0)),
            pl.BlockSpec((1, KV_RANK, 4 * LANE), lambda i, p: (p, 0, 0)),
            pl.BlockSpec((1, LANE), lambda i, p: (0, 0)),
            pl.BlockSpec((1, LANE), lambda i, p: (0, 0)),
        ],
        out_specs=[
            pl.BlockSpec((2, tm, QK_PAD), lambda i, p: (p, i, 0)),
            pl.BlockSpec((2, tm, V_HEAD), lambda i, p: (p, i, 0)),
        ],
        out_shape=[
            jax.ShapeDtypeStruct((N_HEADS, T, QK_PAD), BF16),
            jax.ShapeDtypeStruct((N_HEADS, T, V_HEAD), BF16),
        ],
        compiler_params=_params(("arbitrary", "arbitrary"), 32),
        name="kv_up",
    )(ckvb, kpe2, W['w_kv'], W['g_k_nope'], W['g_k_rope2'])


def _softmax_update(s, v, m_ref, l_ref, acc_ref):
    m_prev = m_ref[...]
    m_next = jnp.maximum(m_prev, jnp.max(s, axis=-1, keepdims=True))
    p = jnp.exp2(s - m_next[:, :1])
    alpha = jnp.exp2(m_prev - m_next)
    l_ref[...] = alpha * l_ref[...] + jnp.sum(p, axis=-1, keepdims=True)
    acc_ref[...] = alpha * acc_ref[...] + jnp.dot(p.astype(BF16), v, preferred_element_type=F32)
    m_ref[...] = m_next


def _qk(q, k):
    return lax.dot_general(q, k, (((1,), (1,)), ((), ())), preferred_element_type=F32)


def _flash_kernel(q_ref, k_ref, v_ref, o_ref, m_ref, l_ref, acc_ref, *, tq):
    i = pl.program_id(1)
    q = q_ref[0]
    m_ref[...] = jnp.full(m_ref.shape, -jnp.inf, F32)
    l_ref[...] = jnp.zeros(l_ref.shape, F32)
    acc_ref[...] = jnp.zeros(acc_ref.shape, F32)

    def kv_tile(j):
        start = pl.multiple_of(j * tq, tq)
        return k_ref[0, pl.ds(start, tq), :], v_ref[0, pl.ds(start, tq), :]

    def below_diagonal(j, carry):
        k, v = kv_tile(j)
        _softmax_update(_qk(q, k), v, m_ref, l_ref, acc_ref)
        return carry

    lax.fori_loop(0, i, below_diagonal, 0)

    k, v = kv_tile(i)
    qchunk = lax.broadcasted_iota(jnp.int32, (tq, tq), 0) // CHUNK
    kchunk = lax.broadcasted_iota(jnp.int32, (tq, tq), 1) // CHUNK
    s = jnp.where(kchunk <= qchunk, _qk(q, k), NEG_INF)
    _softmax_update(s, v, m_ref, l_ref, acc_ref)

    o_ref[...] = (acc_ref[...] / l_ref[...]).astype(BF16)


def _flash_prompt(q, k, v, tq):
    H, S, _ = q.shape
    return pl.pallas_call(
        functools.partial(_flash_kernel, tq=tq),
        grid=(H, S // tq),
        in_specs=[
            pl.BlockSpec((1, tq, QK_PAD), lambda h, i: (h, i, 0)),
            pl.BlockSpec((1, S, QK_PAD), lambda h, i: (h, 0, 0)),
            pl.BlockSpec((1, S, V_HEAD), lambda h, i: (h, 0, 0)),
        ],
        out_specs=pl.BlockSpec((tq, V_HEAD), lambda h, i: (i, h)),
        out_shape=jax.ShapeDtypeStruct((S, H * V_HEAD), BF16),
        scratch_shapes=[pltpu.VMEM((tq, LANE), F32), pltpu.VMEM((tq, LANE), F32),
                        pltpu.VMEM((tq, V_HEAD), F32)],
        compiler_params=_params(("arbitrary", "arbitrary"), 48),
        name="flash_prompt",
    )(q, k, v)


def _decode_kernel(q_ref, ckvn_ref, kpen_ref, ckvc_ref, kpec_ref, w_ref, gn_ref, gr2_ref,
                   o_ref, m_ref, l_ref, acc_ref, *, n_tiles):
    j = pl.program_id(1)

    def attend(ckv, kpe2):
        def pair(p, carry):
            kv = jnp.dot(ckv, w_ref[p], preferred_element_type=F32)
            for e, (k, v) in enumerate(_pair_keys_values(kv, kpe2, gn_ref[...], gr2_ref[...])):
                hd = 2 * p + e
                _softmax_update(_qk(q_ref[hd], k), v, m_ref.at[hd], l_ref.at[hd], acc_ref.at[hd])
            return carry
        lax.fori_loop(0, N_PAIRS, pair, 0)

    @pl.when(j == 0)
    def _():
        m_ref[...] = jnp.full(m_ref.shape, -jnp.inf, F32)
        l_ref[...] = jnp.zeros(l_ref.shape, F32)
        acc_ref[...] = jnp.zeros(acc_ref.shape, F32)
        attend(ckvn_ref[...], kpen_ref[...])

    attend(ckvc_ref[0].astype(BF16), kpec_ref[0])

    @pl.when(j == n_tiles - 1)
    def _():
        for hd in range(N_HEADS):
            o_ref[:, hd * V_HEAD:(hd + 1) * V_HEAD] = (acc_ref[hd] / l_ref[hd]).astype(BF16)


def _decode_attn(q, ckvb_new, kpe2_new, cache_ckv, cache_kpe2, W, tk):
    B, P, _ = cache_ckv.shape
    T = ckvb_new.shape[0]
    t_new = T // B
    n_tiles = P // tk
    return pl.pallas_call(
        functools.partial(_decode_kernel, n_tiles=n_tiles),
        grid=(B, n_tiles),
        in_specs=[
            pl.BlockSpec((N_HEADS, t_new, QK_PAD), lambda b, j: (0, b, 0)),
            pl.BlockSpec((t_new, KV_RANK), lambda b, j: (b, 0)),
            pl.BlockSpec((t_new, LANE), lambda b, j: (b, 0)),
            pl.BlockSpec((1, tk, KV_RANK), lambda b, j: (b, j, 0)),
            pl.BlockSpec((1, tk, LANE), lambda b, j: (b, j, 0)),
            pl.BlockSpec((N_PAIRS, KV_RANK, 4 * LANE), lambda b, j: (0, 0, 0)),
            pl.BlockSpec((1, LANE), lambda b, j: (0, 0)),
            pl.BlockSpec((1, LANE), lambda b, j: (0, 0)),
        ],
        out_specs=pl.BlockSpec((t_new, N_HEADS * V_HEAD), lambda b, j: (b, 0)),
        out_shape=jax.ShapeDtypeStruct((T, N_HEADS * V_HEAD), BF16),
        scratch_shapes=[pltpu.VMEM((N_HEADS, t_new, LANE), F32), pltpu.VMEM((N_HEADS, t_new, LANE), F32),
                        pltpu.VMEM((N_HEADS, t_new, V_HEAD), F32)],
        compiler_params=_params(("arbitrary", "arbitrary"), 48),
        name="decode_attn",
    )(q, ckvb_new, kpe2_new, cache_ckv, cache_kpe2, W['w_kv'], W['g_k_nope'], W['g_k_rope2'])


CONV_ROWS = 32
CONV_COLS = 256


def _conv_kernel(h_ref, past_ref, wa_ref, wg_ref, ba_ref, bg_ref, wdw_ref, bdw_ref, gln_ref, bln_ref,
                 wpw_ref, bpw_ref, yb_ref, state_ref, buf_ref, y_ref, *, tm, n_t):
    t = pl.program_id(1)
    hist = CONV_WIDTH - 1
    off = PAST_PAD - hist

    @pl.when(t == 0)
    def _():
        buf_ref[0:PAST_PAD, :] = jnp.zeros((PAST_PAD, CONV_CH), F32)
        buf_ref[off:PAST_PAD, :] = past_ref[0]

    @pl.when(t > 0)
    def _():
        buf_ref[0:PAST_PAD, :] = buf_ref[tm:tm + PAST_PAD, :]

    h = h_ref[0]
    a = jnp.dot(h, wa_ref[...], preferred_element_type=F32) + ba_ref[...]
    g = jnp.dot(h, wg_ref[...], preferred_element_type=F32) + bg_ref[...]
    buf_ref[PAST_PAD:PAST_PAD + tm, :] = a * jax.nn.sigmoid(g)

    for r0 in range(0, tm, CONV_ROWS):
        for c0 in range(0, CONV_CH, CONV_COLS):
            acc = jnp.zeros((CONV_ROWS, CONV_COLS), F32)
            for k in range(CONV_WIDTH):
                acc = acc + wdw_ref[k:k + 1, c0:c0 + CONV_COLS] * buf_ref[r0 + off + k:r0 + off + k + CONV_ROWS,
                                                                          c0:c0 + CONV_COLS]
            y_ref[r0:r0 + CONV_ROWS, c0:c0 + CONV_COLS] = acc

    y = y_ref[...] + bdw_ref[...]
    yc = y - jnp.mean(y, axis=-1, keepdims=True)
    y = yc * lax.rsqrt(jnp.mean(yc * yc, axis=-1, keepdims=True) + EPS) * gln_ref[...] + bln_ref[...]
    y = y * jax.nn.sigmoid(y)
    yb = jnp.dot(y.astype(BF16), wpw_ref[...], preferred_element_type=F32) + bpw_ref[...]
    yb_ref[0] = yb.astype(BF16)

    @pl.when(t == n_t - 1)
    def _():
        state_ref[0] = buf_ref[tm + off:tm + PAST_PAD, :]


def _conv_branch(h3d, past, W, tm):
    B, S, D = h3d.shape
    n_t = S // tm
    hist = CONV_WIDTH - 1
    const = lambda b, t: (0, 0)
    return pl.pallas_call(
        functools.partial(_conv_kernel, tm=tm, n_t=n_t),
        grid=(B, n_t),
        in_specs=[
            pl.BlockSpec((1, tm, D), lambda b, t: (b, t, 0)),
            pl.BlockSpec((1, hist, CONV_CH), lambda b, t: (b, 0, 0)),
            pl.BlockSpec((D, CONV_CH), const),
            pl.BlockSpec((D, CONV_CH), const),
            pl.BlockSpec((1, CONV_CH), const),
            pl.BlockSpec((1, CONV_CH), const),
            pl.BlockSpec((CONV_WIDTH, CONV_CH), const),
            pl.BlockSpec((1, CONV_CH), const),
            pl.BlockSpec((1, CONV_CH), const),
            pl.BlockSpec((1, CONV_CH), const),
            pl.BlockSpec((CONV_CH, D), const),
            pl.BlockSpec((1, D), const),
        ],
        out_specs=[
            pl.BlockSpec((1, tm, D), lambda b, t: (b, t, 0)),
            pl.BlockSpec((1, hist, CONV_CH), lambda b, t: (b, 0, 0)),
        ],
        out_shape=[
            jax.ShapeDtypeStruct((B, S, D), BF16),
            jax.ShapeDtypeStruct((B, hist, CONV_CH), F32),
        ],
        scratch_shapes=[pltpu.VMEM((PAST_PAD + tm, CONV_CH), F32), pltpu.VMEM((tm, CONV_CH), F32)],
        compiler_params=_params(("arbitrary", "arbitrary"), 48),
        name="conv_branch",
    )(h3d, past, W['w_glu_a'], W['w_glu_g'], W['b_glu_a'], W['b_glu_g'], W['w_dw'], W['b_dw'],
      W['g_ln'], W['b_ln'], W['w_pw'], W['b_pw'])


def _mix_kernel(h_ref, attn_ref, yb_ref, wga_ref, wgb_ref, wao_ref, bga_ref, bgb_ref, m_ref):
    h = h_ref[...]
    ga = jax.nn.sigmoid(jnp.dot(h, wga_ref[...], preferred_element_type=F32) + bga_ref[...])
    gb = jax.nn.sigmoid(jnp.dot(h, wgb_ref[...], preferred_element_type=F32) + bgb_ref[...])
    ya = jnp.dot(attn_ref[...], wao_ref[...], preferred_element_type=F32)
    m_ref[...] = (ga * ya + gb * yb_ref[...].astype(F32)).astype(BF16)


def _mix(h, attn, yb, W, tm, tn):
    T, D = h.shape
    row = lambda i, j: (i, 0)
    col = lambda i, j: (0, j)
    blk = lambda i, j: (i, j)
    return pl.pallas_call(
        _mix_kernel,
        grid=(T // tm, D // tn),
        in_specs=[
            pl.BlockSpec((tm, D), row),
            pl.BlockSpec((tm, D), row),
            pl.BlockSpec((tm, tn), blk),
            pl.BlockSpec((D, tn), col),
            pl.BlockSpec((D, tn), col),
            pl.BlockSpec((D, tn), col),
            pl.BlockSpec((1, tn), col),
            pl.BlockSpec((1, tn), col),
        ],
        out_specs=pl.BlockSpec((tm, tn), blk),
        out_shape=jax.ShapeDtypeStruct((T, D), BF16),
        compiler_params=_params(("arbitrary", "arbitrary"), 48),
        name="gated_mix",
    )(h, attn, yb, W['w_gate_a'], W['w_gate_b'], W['w_attn_out'], W['b_gate_a'], W['b_gate_b'])


def _outproj_kernel(x_ref, m_ref, w_ref, o_ref):
    o_ref[...] = x_ref[...] + jnp.dot(m_ref[...], w_ref[...], preferred_element_type=F32)


def _out_proj(x2d, m, W, tm, tn):
    T, D = x2d.shape
    return pl.pallas_call(
        _outproj_kernel,
        grid=(T // tm, D // tn),
        in_specs=[
            pl.BlockSpec((tm, tn), lambda i, j: (i, j)),
            pl.BlockSpec((tm, D), lambda i, j: (i, 0)),
            pl.BlockSpec((D, tn), lambda i, j: (0, j)),
        ],
        out_specs=pl.BlockSpec((tm, tn), lambda i, j: (i, j)),
        out_shape=jax.ShapeDtypeStruct((T, D), F32),
        compiler_params=_params(("arbitrary", "arbitrary"), 48),
        name="out_proj",
    )(x2d, m, W['w_out'])


def _ffn_kernel(x_ref, g_ref, wg_ref, wu_ref, wd_ref, o_ref, h_ref):
    @pl.when(pl.program_id(1) == 0)
    def _():
        x = x_ref[...]
        h_ref[...] = (x * _rms_scale(x, x.shape[-1]) * g_ref[...]).astype(BF16)
        o_ref[...] = x

    h = h_ref[...]
    gate = jnp.dot(h, wg_ref[...], preferred_element_type=F32)
    up = jnp.dot(h, wu_ref[...], preferred_element_type=F32)
    act = (gate * jax.nn.sigmoid(gate) * up).astype(BF16)
    o_ref[...] += jnp.dot(act, wd_ref[...], preferred_element_type=F32)


def _ffn(x2d, W, tm, tf):
    T, D = x2d.shape
    d_ff = W['w_ffn_gate'].shape[1]
    return pl.pallas_call(
        _ffn_kernel,
        grid=(T // tm, d_ff // tf),
        in_specs=[
            pl.BlockSpec((tm, D), lambda i, j: (i, 0)),
            pl.BlockSpec((1, D), lambda i, j: (0, 0)),
            pl.BlockSpec((D, tf), lambda i, j: (0, j)),
            pl.BlockSpec((D, tf), lambda i, j: (0, j)),
            pl.BlockSpec((tf, D), lambda i, j: (j, 0)),
        ],
        out_specs=pl.BlockSpec((tm, D), lambda i, j: (i, 0)),
        out_shape=jax.ShapeDtypeStruct((T, D), F32),
        scratch_shapes=[pltpu.VMEM((tm, D), BF16)],
        compiler_params=_params(("arbitrary", "arbitrary"), 48),
        name="ffn",
    )(x2d, W['g_ffn'], W['w_ffn_gate'], W['w_ffn_up'], W['w_ffn_down'])


def _rot_half_cols(w):
    half = QK_ROPE // 2
    return jnp.concatenate([-w[..., half:], w[..., :half]], axis=-1)


def _prep_weights(lw):
    (g_mix_norm, w_in, b_glu, b_gate, g_q_a, w_q_up, g_q_norm, g_kv_a, w_kv_up, g_k_norm,
     w_attn_out, w_dw, b_dw, g_conv_ln, b_conv_ln, w_conv_out, b_conv_out, w_out,
     g_ffn_norm, w_ffn_gate, w_ffn_up, w_ffn_down) = lw
    D = w_in.shape[0]
    o_kv = Q_RANK
    o_pe = o_kv + KV_RANK
    o_glu = o_pe + QK_ROPE
    o_gate = o_glu + 2 * CONV_CH
    w_pe = w_in[:, o_pe:o_glu]
    w_pe_rot = _rot_half_cols(w_pe)
    row = lambda v: v.reshape(1, -1).astype(F32)
    W = {
        'g_mix': row(g_mix_norm),
        'w_small': jnp.concatenate([w_in[:, :o_pe], w_pe, w_pe, w_pe_rot, w_pe_rot], axis=1).astype(BF16),
        'g_q_a': row(g_q_a),
        'g_kv_a': row(g_kv_a),
        'w_glu_a': w_in[:, o_glu:o_glu + CONV_CH].astype(BF16),
        'w_glu_g': w_in[:, o_glu + CONV_CH:o_gate].astype(BF16),
        'b_glu_a': row(b_glu[:CONV_CH]),
        'b_glu_g': row(b_glu[CONV_CH:]),
        'w_gate_a': w_in[:, o_gate:o_gate + D].astype(BF16),
        'w_gate_b': w_in[:, o_gate + D:].astype(BF16),
        'b_gate_a': row(b_gate[:D]),
        'b_gate_b': row(b_gate[D:]),
        'w_attn_out': w_attn_out.astype(BF16),
        'w_dw': w_dw.astype(F32),
        'b_dw': row(b_dw),
        'g_ln': row(g_conv_ln),
        'b_ln': row(b_conv_ln),
        'w_pw': w_conv_out.astype(BF16),
        'b_pw': row(b_conv_out),
        'w_out': w_out.astype(BF16),
        'g_ffn': row(g_ffn_norm),
        'w_ffn_gate': w_ffn_gate.astype(BF16),
        'w_ffn_up': w_ffn_up.astype(BF16),
        'w_ffn_down': w_ffn_down.astype(BF16),
    }
    wq = w_q_up.reshape(Q_RANK, N_PAIRS, 2, QK_HEAD)
    wq_nope = wq[..., :QK_NOPE].reshape(Q_RANK, N_PAIRS, 2 * QK_NOPE)
    wq_rope = wq[..., QK_NOPE:]
    W['w_q'] = jnp.concatenate(
        [wq_nope, wq_rope.reshape(Q_RANK, N_PAIRS, 2 * QK_ROPE),
         _rot_half_cols(wq_rope).reshape(Q_RANK, N_PAIRS, 2 * QK_ROPE)], axis=-1
    ).reshape(Q_RANK, N_PAIRS * 4 * LANE).astype(BF16)
    wkv = w_kv_up.reshape(KV_RANK, N_PAIRS, 2, QK_NOPE + V_HEAD)
    W['w_kv'] = jnp.concatenate(
        [wkv[..., :QK_NOPE].reshape(KV_RANK, N_PAIRS, 2 * QK_NOPE),
         wkv[..., QK_NOPE:].reshape(KV_RANK, N_PAIRS, 2 * V_HEAD)], axis=-1
    ).transpose(1, 0, 2).astype(BF16)
    dup = lambda v: jnp.concatenate([v, v]).reshape(1, LANE).astype(F32)
    W['g_q_nope'] = row(g_q_norm[:QK_NOPE])
    W['g_q_rope2'] = dup(g_q_norm[QK_NOPE:])
    W['g_k_nope'] = row(g_k_norm[:QK_NOPE])
    W['g_k_rope2'] = dup(g_k_norm[QK_NOPE:])
    return W


def _rope_tables(pos):
    inv_freq = 1.0 / (ROPE_THETA ** (jnp.arange(0, QK_ROPE, 2, dtype=F32) / QK_ROPE))
    ang = pos.astype(F32)[:, None] * inv_freq[None, :]
    return jnp.tile(jnp.cos(ang), (1, 4)), jnp.tile(jnp.sin(ang), (1, 4))


def _layer(x, pos, past_ckv, past_kpe, past_conv, W):
    B, S, D = x.shape
    T = B * S
    x2d = x.reshape(T, D)
    c2, s2 = _rope_tables(pos)
    if B > 1:
        c2, s2 = jnp.tile(c2, (B, 1)), jnp.tile(s2, (B, 1))
    tm = min(T, 512)
    h, cq, ckv, ckvb, kpe, kpe2 = _inproj(x2d, c2, s2, W, tm)
    q = _q_up(cq, c2, s2, W, tm)
    if past_ckv is None:
        k, v = _kv_up(ckvb, kpe2, W, tm)
        attn = _flash_prompt(q, k, v, 512)
        past_conv = jnp.zeros((B, CONV_WIDTH - 1, CONV_CH), x.dtype)
    else:
        cache_kpe2 = jnp.concatenate([past_kpe, past_kpe], axis=-1)
        attn = _decode_attn(q, ckvb, kpe2, past_ckv, cache_kpe2, W, 1024)
    yb, conv_state = _conv_branch(h.reshape(B, S, D), past_conv, W, min(S, 256))
    m = _mix(h, attn, yb.reshape(T, D), W, tm, 512)
    x1 = _out_proj(x2d, m, W, tm, 512)
    y = _ffn(x1, W, tm, 512)
    return (y.reshape(B, S, D), ckv.reshape(B, S, KV_RANK), kpe.reshape(B, S, QK_ROPE), conv_state)


def kernel(x_prompt, x_sample, cache_ckv, cache_kpe, state_conv, g_mix_norm, w_in, b_glu, b_gate, g_q_a,
           w_q_up, g_q_norm, g_kv_a, w_kv_up, g_k_norm, w_attn_out, w_dw, b_dw, g_conv_ln, b_conv_ln,
           w_conv_out, b_conv_out, w_out, g_ffn_norm, w_ffn_gate, w_ffn_up, w_ffn_down):
    weights = (g_mix_norm, w_in, b_glu, b_gate, g_q_a, w_q_up, g_q_norm, g_kv_a, w_kv_up, g_k_norm,
               w_attn_out, w_dw, b_dw, g_conv_ln, b_conv_ln, w_conv_out, b_conv_out, w_out,
               g_ffn_norm, w_ffn_gate, w_ffn_up, w_ffn_down)
    depth = w_in.shape[0]
    pos_prompt = jnp.arange(x_prompt.shape[1])
    pos_sample = cache_ckv.shape[2] + jnp.arange(x_sample.shape[1])
    y_prompt, y_sample = x_prompt, x_sample
    outs = [[] for _ in range(6)]
    for l in range(depth):
        W = _prep_weights(tuple(w[l] for w in weights))
        y_prompt, ckv, kpe, conv = _layer(y_prompt, pos_prompt, None, None, None, W)
        outs[0].append(ckv); outs[1].append(kpe); outs[2].append(conv)
        y_sample, ckv, kpe, conv = _layer(y_sample, pos_sample, cache_ckv[l], cache_kpe[l], state_conv[l], W)
        outs[3].append(ckv); outs[4].append(kpe); outs[5].append(conv)
    return (y_prompt, y_sample) + tuple(jnp.stack(o) for o in outs)
```

```python
import functools
import math

import jax
import jax.numpy as jnp
from jax import lax
from jax.experimental import pallas as pl
from jax.experimental.pallas import tpu as pltpu

F32 = jnp.float32
BF16 = jnp.bfloat16

CHUNK = 64
N_HEADS = 16
Q_RANK = 512
KV_RANK = 512
QK_NOPE = 128
QK_ROPE = 64
QK_HEAD = QK_NOPE + QK_ROPE
V_HEAD = 128
CONV_CH = 1024
CONV_WIDTH = 31
ROPE_THETA = 10000.0
EPS = 1e-6
NEG_INF = -1e30
SCALE = QK_HEAD ** -0.5
LOG2E = math.log2(math.e)

LANE = 128
QK_PAD = 2 * LANE
N_PAIRS = N_HEADS // 2
PAST_PAD = 32
MIB = 1024 * 1024


def _params(semantics, vmem_mib):
    return pltpu.CompilerParams(dimension_semantics=semantics, vmem_limit_bytes=vmem_mib * MIB)


def _rms_scale(v, n):
    return lax.rsqrt(jnp.sum(v * v, axis=-1, keepdims=True) * (1.0 / n) + EPS)


def _half_masks():
    lane = lax.broadcasted_iota(jnp.int32, (1, LANE), 1)
    lo = (lane < QK_ROPE).astype(F32)
    return lo, 1.0 - lo


def _inproj_kernel(x_ref, g_ref, w_ref, gq_ref, gkv_ref, c2_ref, s2_ref,
                   h_ref, cq_ref, ckv_ref, ckvb_ref, kpe_ref, kpe2_ref):
    x = x_ref[...]
    h = (x * _rms_scale(x, x.shape[-1]) * g_ref[...]).astype(BF16)
    h_ref[...] = h
    z = jnp.dot(h, w_ref[...], preferred_element_type=F32)
    cq = z[:, :Q_RANK]
    cq_ref[...] = (cq * _rms_scale(cq, Q_RANK) * gq_ref[...]).astype(BF16)
    ckv = z[:, Q_RANK:Q_RANK + KV_RANK]
    ckv = ckv * _rms_scale(ckv, KV_RANK) * gkv_ref[...]
    ckv_ref[...] = ckv
    ckvb_ref[...] = ckv.astype(BF16)
    base = Q_RANK + KV_RANK
    kpe2 = z[:, base:base + LANE] * c2_ref[...] + z[:, base + LANE:base + 2 * LANE] * s2_ref[...]
    kpe2_ref[...] = kpe2
    kpe_ref[...] = kpe2[:, :QK_ROPE]


def _inproj(x2d, c2, s2, W, tm):
    T, D = x2d.shape
    n_in = W['w_small'].shape[1]
    row = lambda i: (i, 0)
    const = lambda i: (0, 0)
    return pl.pallas_call(
        _inproj_kernel,
        grid=(T // tm,),
        in_specs=[
            pl.BlockSpec((tm, D), row),
            pl.BlockSpec((1, D), const),
            pl.BlockSpec((D, n_in), const),
            pl.BlockSpec((1, Q_RANK), const),
            pl.BlockSpec((1, KV_RANK), const),
            pl.BlockSpec((tm, LANE), row),
            pl.BlockSpec((tm, LANE), row),
        ],
        out_specs=[
            pl.BlockSpec((tm, D), row),
            pl.BlockSpec((tm, Q_RANK), row),
            pl.BlockSpec((tm, KV_RANK), row),
            pl.BlockSpec((tm, KV_RANK), row),
            pl.BlockSpec((tm, QK_ROPE), row),
            pl.BlockSpec((tm, LANE), row),
        ],
        out_shape=[
            jax.ShapeDtypeStruct((T, D), BF16),
            jax.ShapeDtypeStruct((T, Q_RANK), BF16),
            jax.ShapeDtypeStruct((T, KV_RANK), F32),
            jax.ShapeDtypeStruct((T, KV_RANK), BF16),
            jax.ShapeDtypeStruct((T, QK_ROPE), F32),
            jax.ShapeDtypeStruct((T, LANE), F32),
        ],
        compiler_params=_params(("arbitrary",), 48),
        name="inproj",
    )(x2d, W['g_mix'], W['w_small'], W['g_q_a'], W['g_kv_a'], c2, s2)


def _qup_kernel(cq_ref, w_ref, c2_ref, s2_ref, gn_ref, gr2_ref, q_ref):
    z = jnp.dot(cq_ref[...], w_ref[...], preferred_element_type=F32)
    rope2 = z[:, 2 * LANE:3 * LANE] * c2_ref[...] + z[:, 3 * LANE:] * s2_ref[...]
    for e, msk in enumerate(_half_masks()):
        nope = z[:, e * LANE:(e + 1) * LANE]
        rope = rope2 * msk
        ss = jnp.sum(nope * nope, axis=-1, keepdims=True) + jnp.sum(rope * rope, axis=-1, keepdims=True)
        r = lax.rsqrt(ss * (1.0 / QK_HEAD) + EPS) * (SCALE * LOG2E)
        q_ref[e, :, :QK_NOPE] = (nope * r * gn_ref[...]).astype(BF16)
        q_ref[e, :, QK_NOPE:] = (rope * r * gr2_ref[...]).astype(BF16)


def _q_up(cq, c2, s2, W, tm):
    T = cq.shape[0]
    return pl.pallas_call(
        _qup_kernel,
        grid=(T // tm, N_PAIRS),
        in_specs=[
            pl.BlockSpec((tm, Q_RANK), lambda i, p: (i, 0)),
            pl.BlockSpec((Q_RANK, 4 * LANE), lambda i, p: (0, p)),
            pl.BlockSpec((tm, LANE), lambda i, p: (i, 0)),
            pl.BlockSpec((tm, LANE), lambda i, p: (i, 0)),
            pl.BlockSpec((1, LANE), lambda i, p: (0, 0)),
            pl.BlockSpec((1, LANE), lambda i, p: (0, 0)),
        ],
        out_specs=pl.BlockSpec((2, tm, QK_PAD), lambda i, p: (p, i, 0)),
        out_shape=jax.ShapeDtypeStruct((N_HEADS, T, QK_PAD), BF16),
        compiler_params=_params(("arbitrary", "arbitrary"), 32),
        name="q_up",
    )(cq, W['w_q'], c2, s2, W['g_q_nope'], W['g_q_rope2'])


def _pair_keys_values(kv, kpe2, gn, gr2):
    lo, hi = _half_masks()
    ss_pe = jnp.sum(kpe2 * kpe2 * lo, axis=-1, keepdims=True)
    out = []
    for e, msk in enumerate((lo, hi)):
        kn = kv[:, e * LANE:(e + 1) * LANE]
        r = lax.rsqrt((jnp.sum(kn * kn, axis=-1, keepdims=True) + ss_pe) * (1.0 / QK_HEAD) + EPS)
        k = jnp.concatenate([kn * r * gn, kpe2 * msk * r * gr2], axis=-1).astype(BF16)
        v = kv[:, (2 + e) * LANE:(3 + e) * LANE].astype(BF16)
        out.append((k, v))
    return out


def _kvup_kernel(ckv_ref, kpe2_ref, w_ref, gn_ref, gr2_ref, k_ref, v_ref):
    kv = jnp.dot(ckv_ref[...], w_ref[0], preferred_element_type=F32)
    for e, (k, v) in enumerate(_pair_keys_values(kv, kpe2_ref[...], gn_ref[...], gr2_ref[...])):
        k_ref[e] = k
        v_ref[e] = v


def _kv_up(ckvb, kpe2, W, tm):
    T = ckvb.shape[0]
    return pl.pallas_call(
        _kvup_kernel,
        grid=(T // tm, N_PAIRS),
        in_specs=[
            pl.BlockSpec((tm, KV_RANK), lambda i, p: (i, 0)),
            pl.BlockSpec((tm, LANE), lambda i, p: (i, 0)),
            pl.BlockSpec((1, KV_RANK, 4 * LANE), lambda i, p: (p, 0, 0)),
            pl.BlockSpec((1, LANE), lambda i, p: (0, 0)),
            pl.BlockSpec((1, LANE), lambda i, p: (0, 0)),
        ],
        out_specs=[
            pl.BlockSpec((2, tm, QK_PAD), lambda i, p: (p, i, 0)),
            pl.BlockSpec((2, tm, V_HEAD), lambda i, p: (p, i, 0)),
        ],
        out_shape=[
            jax.ShapeDtypeStruct((N_HEADS, T, QK_PAD), BF16),
            jax.ShapeDtypeStruct((N_HEADS, T, V_HEAD), BF16),
        ],
        compiler_params=_params(("arbitrary", "arbitrary"), 32),
        name="kv_up",
    )(ckvb, kpe2, W['w_kv'], W['g_k_nope'], W['g_k_rope2'])


def _softmax_update(s, v, m_ref, l_ref, acc_ref):
    tk = s.shape[1]
    m_prev = m_ref[...]
    if tk % LANE == 0:
        blocks = [s[:, c:c + LANE] for c in range(0, tk, LANE)]
        m_next = jnp.maximum(m_prev, jnp.max(functools.reduce(jnp.maximum, blocks), axis=-1, keepdims=True))
        blocks = [jnp.exp2(b - m_next) for b in blocks]
        p_sum = functools.reduce(jnp.add, blocks)
        p = jnp.concatenate(blocks, axis=-1)
    else:
        m_next = jnp.maximum(m_prev, jnp.max(s, axis=-1, keepdims=True))
        p = jnp.exp2(s - m_next[:, :1])
        p_sum = jnp.sum(p, axis=-1, keepdims=True) * (1.0 / LANE)
    alpha = jnp.exp2(m_prev - m_next)
    l_ref[...] = alpha * l_ref[...] + p_sum
    acc_ref[...] = alpha * acc_ref[...] + jnp.dot(p.astype(BF16), v, preferred_element_type=F32)
    m_ref[...] = m_next


def _qk(q, k):
    return lax.dot_general(q, k, (((1,), (1,)), ((), ())), preferred_element_type=F32)


N_CHAINS = 2


def _flash_kernel(q_ref, k_ref, v_ref, o_ref, m_ref, l_ref, acc_ref, *, tq):
    i = pl.program_id(1)
    q = q_ref[0]
    m_ref[...] = jnp.full(m_ref.shape, -jnp.inf, F32)
    l_ref[...] = jnp.zeros(l_ref.shape, F32)
    acc_ref[...] = jnp.zeros(acc_ref.shape, F32)

    def tile(j, chain, diagonal=False):
        start = pl.multiple_of(j * tq, tq)
        s = _qk(q, k_ref[0, pl.ds(start, tq), :])
        if diagonal:
            qchunk = lax.broadcasted_iota(jnp.int32, (tq, tq), 0) // CHUNK
            kchunk = lax.broadcasted_iota(jnp.int32, (tq, tq), 1) // CHUNK
            s = jnp.where(kchunk <= qchunk, s, NEG_INF)
        _softmax_update(s, v_ref[0, pl.ds(start, tq), :], m_ref.at[chain], l_ref.at[chain], acc_ref.at[chain])

    def quad(jj, carry):
        for t in range(4):
            tile(4 * jj + t, t % N_CHAINS)
        return carry

    def pair(jj, carry):
        tile(2 * jj, 0)
        tile(2 * jj + 1, 1)
        return carry

    lax.fori_loop(0, i // 4, quad, 0)
    lax.fori_loop(2 * (i // 4), i // 2, pair, 0)

    @pl.when(i % 2 == 1)
    def _():
        tile(i - 1, 0)
        tile(i, 1, diagonal=True)

    @pl.when(i % 2 == 0)
    def _():
        tile(i, 1, diagonal=True)

    m = jnp.maximum(m_ref[0], m_ref[1])
    w0 = jnp.exp2(m_ref[0] - m)
    w1 = jnp.exp2(m_ref[1] - m)
    l = jnp.sum(w0 * l_ref[0] + w1 * l_ref[1], axis=-1, keepdims=True)
    o_ref[...] = ((w0 * acc_ref[0] + w1 * acc_ref[1]) / l).astype(BF16)


def _flash_prompt(q, k, v, tq):
    H, S, _ = q.shape
    return pl.pallas_call(
        functools.partial(_flash_kernel, tq=tq),
        grid=(H, S // tq),
        in_specs=[
            pl.BlockSpec((1, tq, QK_PAD), lambda h, i: (h, i, 0)),
            pl.BlockSpec((1, S, QK_PAD), lambda h, i: (h, 0, 0)),
            pl.BlockSpec((1, S, V_HEAD), lambda h, i: (h, 0, 0)),
        ],
        out_specs=pl.BlockSpec((tq, V_HEAD), lambda h, i: (i, h)),
        out_shape=jax.ShapeDtypeStruct((S, H * V_HEAD), BF16),
        scratch_shapes=[pltpu.VMEM((N_CHAINS, tq, LANE), F32), pltpu.VMEM((N_CHAINS, tq, LANE), F32),
                        pltpu.VMEM((N_CHAINS, tq, V_HEAD), F32)],
        compiler_params=_params(("arbitrary", "arbitrary"), 48),
        name="flash_prompt",
    )(q, k, v)


def _decode_kernel(q_ref, ckvn_ref, kpen_ref, ckvc_ref, kpec_ref, w_ref, gn_ref, gr2_ref,
                   o_ref, m_ref, l_ref, acc_ref, *, n_tiles):
    j = pl.program_id(1)

    def attend(ckv, kpe2):
        def pair(p, carry):
            kv = jnp.dot(ckv, w_ref[p], preferred_element_type=F32)
            for e, (k, v) in enumerate(_pair_keys_values(kv, kpe2, gn_ref[...], gr2_ref[...])):
                hd = 2 * p + e
                _softmax_update(_qk(q_ref[hd], k), v, m_ref.at[hd], l_ref.at[hd], acc_ref.at[hd])
            return carry
        lax.fori_loop(0, N_PAIRS, pair, 0)

    @pl.when(j == 0)
    def _():
        m_ref[...] = jnp.full(m_ref.shape, -jnp.inf, F32)
        l_ref[...] = jnp.zeros(l_ref.shape, F32)
        acc_ref[...] = jnp.zeros(acc_ref.shape, F32)
        attend(ckvn_ref[...], kpen_ref[...])

    attend(ckvc_ref[0].astype(BF16), kpec_ref[0])

    @pl.when(j == n_tiles - 1)
    def _():
        for hd in range(N_HEADS):
            l = jnp.sum(l_ref[hd], axis=-1, keepdims=True)
            o_ref[:, hd * V_HEAD:(hd + 1) * V_HEAD] = (acc_ref[hd] / l).astype(BF16)


def _decode_attn(q, ckvb_new, kpe2_new, cache_ckv, cache_kpe2, W, tk):
    B, P, _ = cache_ckv.shape
    T = ckvb_new.shape[0]
    t_new = T // B
    n_tiles = P // tk
    return pl.pallas_call(
        functools.partial(_decode_kernel, n_tiles=n_tiles),
        grid=(B, n_tiles),
        in_specs=[
            pl.BlockSpec((N_HEADS, t_new, QK_PAD), lambda b, j: (0, b, 0)),
            pl.BlockSpec((t_new, KV_RANK), lambda b, j: (b, 0)),
            pl.BlockSpec((t_new, LANE), lambda b, j: (b, 0)),
            pl.BlockSpec((1, tk, KV_RANK), lambda b, j: (b, j, 0)),
            pl.BlockSpec((1, tk, LANE), lambda b, j: (b, j, 0)),
            pl.BlockSpec((N_PAIRS, KV_RANK, 4 * LANE), lambda b, j: (0, 0, 0)),
            pl.BlockSpec((1, LANE), lambda b, j: (0, 0)),
            pl.BlockSpec((1, LANE), lambda b, j: (0, 0)),
        ],
        out_specs=pl.BlockSpec((t_new, N_HEADS * V_HEAD), lambda b, j: (b, 0)),
        out_shape=jax.ShapeDtypeStruct((T, N_HEADS * V_HEAD), BF16),
        scratch_shapes=[pltpu.VMEM((N_HEADS, t_new, LANE), F32), pltpu.VMEM((N_HEADS, t_new, LANE), F32),
                        pltpu.VMEM((N_HEADS, t_new, V_HEAD), F32)],
        compiler_params=_params(("arbitrary", "arbitrary"), 48),
        name="decode_attn",
    )(q, ckvb_new, kpe2_new, cache_ckv, cache_kpe2, W['w_kv'], W['g_k_nope'], W['g_k_rope2'])


CONV_ROWS = 32
CONV_COLS = 256


def _conv_kernel(h_ref, past_ref, wa_ref, wg_ref, ba_ref, bg_ref, wdw_ref, bdw_ref, gln_ref, bln_ref,
                 wpw_ref, bpw_ref, yb_ref, state_ref, buf_ref, y_ref, *, tm, n_t):
    t = pl.program_id(1)
    hist = CONV_WIDTH - 1
    off = PAST_PAD - hist

    @pl.when(t == 0)
    def _():
        buf_ref[0:PAST_PAD, :] = jnp.zeros((PAST_PAD, CONV_CH), F32)
        buf_ref[off:PAST_PAD, :] = past_ref[0]

    @pl.when(t > 0)
    def _():
        buf_ref[0:PAST_PAD, :] = buf_ref[tm:tm + PAST_PAD, :]

    h = h_ref[0]
    a = jnp.dot(h, wa_ref[...], preferred_element_type=F32) + ba_ref[...]
    g = jnp.dot(h, wg_ref[...], preferred_element_type=F32) + bg_ref[...]
    buf_ref[PAST_PAD:PAST_PAD + tm, :] = a * jax.nn.sigmoid(g)

    for r0 in range(0, tm, CONV_ROWS):
        for c0 in range(0, CONV_CH, CONV_COLS):
            acc = jnp.zeros((CONV_ROWS, CONV_COLS), F32)
            for k in range(CONV_WIDTH):
                acc = acc + wdw_ref[k:k + 1, c0:c0 + CONV_COLS] * buf_ref[r0 + off + k:r0 + off + k + CONV_ROWS,
                                                                          c0:c0 + CONV_COLS]
            y_ref[r0:r0 + CONV_ROWS, c0:c0 + CONV_COLS] = acc

    y = y_ref[...] + bdw_ref[...]
    yc = y - jnp.mean(y, axis=-1, keepdims=True)
    y = yc * lax.rsqrt(jnp.mean(yc * yc, axis=-1, keepdims=True) + EPS) * gln_ref[...] + bln_ref[...]
    y = y * jax.nn.sigmoid(y)
    yb = jnp.dot(y.astype(BF16), wpw_ref[...], preferred_element_type=F32) + bpw_ref[...]
    yb_ref[0] = yb.astype(BF16)

    @pl.when(t == n_t - 1)
    def _():
        state_ref[0] = buf_ref[tm + off:tm + PAST_PAD, :]


def _conv_branch(h3d, past, W, tm):
    B, S, D = h3d.shape
    n_t = S // tm
    hist = CONV_WIDTH - 1
    const = lambda b, t: (0, 0)
    return pl.pallas_call(
        functools.partial(_conv_kernel, tm=tm, n_t=n_t),
        grid=(B, n_t),
        in_specs=[
            pl.BlockSpec((1, tm, D), lambda b, t: (b, t, 0)),
            pl.BlockSpec((1, hist, CONV_CH), lambda b, t: (b, 0, 0)),
            pl.BlockSpec((D, CONV_CH), const),
            pl.BlockSpec((D, CONV_CH), const),
            pl.BlockSpec((1, CONV_CH), const),
            pl.BlockSpec((1, CONV_CH), const),
            pl.BlockSpec((CONV_WIDTH, CONV_CH), const),
            pl.BlockSpec((1, CONV_CH), const),
            pl.BlockSpec((1, CONV_CH), const),
            pl.BlockSpec((1, CONV_CH), const),
            pl.BlockSpec((CONV_CH, D), const),
            pl.BlockSpec((1, D), const),
        ],
        out_specs=[
            pl.BlockSpec((1, tm, D), lambda b, t: (b, t, 0)),
            pl.BlockSpec((1, hist, CONV_CH), lambda b, t: (b, 0, 0)),
        ],
        out_shape=[
            jax.ShapeDtypeStruct((B, S, D), BF16),
            jax.ShapeDtypeStruct((B, hist, CONV_CH), F32),
        ],
        scratch_shapes=[pltpu.VMEM((PAST_PAD + tm, CONV_CH), F32), pltpu.VMEM((tm, CONV_CH), F32)],
        compiler_params=_params(("arbitrary", "arbitrary"), 48),
        name="conv_branch",
    )(h3d, past, W['w_glu_a'], W['w_glu_g'], W['b_glu_a'], W['b_glu_g'], W['w_dw'], W['b_dw'],
      W['g_ln'], W['b_ln'], W['w_pw'], W['b_pw'])


def _mix_kernel(h_ref, attn_ref, yb_ref, wga_ref, wgb_ref, wao_ref, bga_ref, bgb_ref, m_ref):
    h = h_ref[...]
    ga = jax.nn.sigmoid(jnp.dot(h, wga_ref[...], preferred_element_type=F32) + bga_ref[...])
    gb = jax.nn.sigmoid(jnp.dot(h, wgb_ref[...], preferred_element_type=F32) + bgb_ref[...])
    ya = jnp.dot(attn_ref[...], wao_ref[...], preferred_element_type=F32)
    m_ref[...] = (ga * ya + gb * yb_ref[...].astype(F32)).astype(BF16)


def _mix(h, attn, yb, W, tm, tn):
    T, D = h.shape
    row = lambda i, j: (i, 0)
    col = lambda i, j: (0, j)
    blk = lambda i, j: (i, j)
    return pl.pallas_call(
        _mix_kernel,
        grid=(T // tm, D // tn),
        in_specs=[
            pl.BlockSpec((tm, D), row),
            pl.BlockSpec((tm, D), row),
            pl.BlockSpec((tm, tn), blk),
            pl.BlockSpec((D, tn), col),
            pl.BlockSpec((D, tn), col),
            pl.BlockSpec((D, tn), col),
            pl.BlockSpec((1, tn), col),
            pl.BlockSpec((1, tn), col),
        ],
        out_specs=pl.BlockSpec((tm, tn), blk),
        out_shape=jax.ShapeDtypeStruct((T, D), BF16),
        compiler_params=_params(("arbitrary", "arbitrary"), 48),
        name="gated_mix",
    )(h, attn, yb, W['w_gate_a'], W['w_gate_b'], W['w_attn_out'], W['b_gate_a'], W['b_gate_b'])


def _outproj_kernel(x_ref, m_ref, w_ref, o_ref):
    o_ref[...] = x_ref[...] + jnp.dot(m_ref[...], w_ref[...], preferred_element_type=F32)


def _out_proj(x2d, m, W, tm, tn):
    T, D = x2d.shape
    return pl.pallas_call(
        _outproj_kernel,
        grid=(T // tm, D // tn),
        in_specs=[
            pl.BlockSpec((tm, tn), lambda i, j: (i, j)),
            pl.BlockSpec((tm, D), lambda i, j: (i, 0)),
            pl.BlockSpec((D, tn), lambda i, j: (0, j)),
        ],
        out_specs=pl.BlockSpec((tm, tn), lambda i, j: (i, j)),
        out_shape=jax.ShapeDtypeStruct((T, D), F32),
        compiler_params=_params(("arbitrary", "arbitrary"), 48),
        name="out_proj",
    )(x2d, m, W['w_out'])


def _ffn_kernel(x_ref, g_ref, wg_ref, wu_ref, wd_ref, o_ref, h_ref):
    @pl.when(pl.program_id(1) == 0)
    def _():
        x = x_ref[...]
        h_ref[...] = (x * _rms_scale(x, x.shape[-1]) * g_ref[...]).astype(BF16)
        o_ref[...] = x

    h = h_ref[...]
    gate = jnp.dot(h, wg_ref[...], preferred_element_type=F32)
    up = jnp.dot(h, wu_ref[...], preferred_element_type=F32)
    act = (gate * jax.nn.sigmoid(gate) * up).astype(BF16)
    o_ref[...] += jnp.dot(act, wd_ref[...], preferred_element_type=F32)


def _ffn(x2d, W, tm, tf):
    T, D = x2d.shape
    d_ff = W['w_ffn_gate'].shape[1]
    return pl.pallas_call(
        _ffn_kernel,
        grid=(T // tm, d_ff // tf),
        in_specs=[
            pl.BlockSpec((tm, D), lambda i, j: (i, 0)),
            pl.BlockSpec((1, D), lambda i, j: (0, 0)),
            pl.BlockSpec((D, tf), lambda i, j: (0, j)),
            pl.BlockSpec((D, tf), lambda i, j: (0, j)),
            pl.BlockSpec((tf, D), lambda i, j: (j, 0)),
        ],
        out_specs=pl.BlockSpec((tm, D), lambda i, j: (i, 0)),
        out_shape=jax.ShapeDtypeStruct((T, D), F32),
        scratch_shapes=[pltpu.VMEM((tm, D), BF16)],
        compiler_params=_params(("arbitrary", "arbitrary"), 48),
        name="ffn",
    )(x2d, W['g_ffn'], W['w_ffn_gate'], W['w_ffn_up'], W['w_ffn_down'])


def _rot_half_cols(w):
    half = QK_ROPE // 2
    return jnp.concatenate([-w[..., half:], w[..., :half]], axis=-1)


def _prep_weights(lw):
    (g_mix_norm, w_in, b_glu, b_gate, g_q_a, w_q_up, g_q_norm, g_kv_a, w_kv_up, g_k_norm,
     w_attn_out, w_dw, b_dw, g_conv_ln, b_conv_ln, w_conv_out, b_conv_out, w_out,
     g_ffn_norm, w_ffn_gate, w_ffn_up, w_ffn_down) = lw
    D = w_in.shape[0]
    o_kv = Q_RANK
    o_pe = o_kv + KV_RANK
    o_glu = o_pe + QK_ROPE
    o_gate = o_glu + 2 * CONV_CH
    w_pe = w_in[:, o_pe:o_glu]
    w_pe_rot = _rot_half_cols(w_pe)
    row = lambda v: v.reshape(1, -1).astype(F32)
    W = {
        'g_mix': row(g_mix_norm),
        'w_small': jnp.concatenate([w_in[:, :o_pe], w_pe, w_pe, w_pe_rot, w_pe_rot], axis=1).astype(BF16),
        'g_q_a': row(g_q_a),
        'g_kv_a': row(g_kv_a),
        'w_glu_a': w_in[:, o_glu:o_glu + CONV_CH].astype(BF16),
        'w_glu_g': w_in[:, o_glu + CONV_CH:o_gate].astype(BF16),
        'b_glu_a': row(b_glu[:CONV_CH]),
        'b_glu_g': row(b_glu[CONV_CH:]),
        'w_gate_a': w_in[:, o_gate:o_gate + D].astype(BF16),
        'w_gate_b': w_in[:, o_gate + D:].astype(BF16),
        'b_gate_a': row(b_gate[:D]),
        'b_gate_b': row(b_gate[D:]),
        'w_attn_out': w_attn_out.astype(BF16),
        'w_dw': w_dw.astype(F32),
        'b_dw': row(b_dw),
        'g_ln': row(g_conv_ln),
        'b_ln': row(b_conv_ln),
        'w_pw': w_conv_out.astype(BF16),
        'b_pw': row(b_conv_out),
        'w_out': w_out.astype(BF16),
        'g_ffn': row(g_ffn_norm),
        'w_ffn_gate': w_ffn_gate.astype(BF16),
        'w_ffn_up': w_ffn_up.astype(BF16),
        'w_ffn_down': w_ffn_down.astype(BF16),
    }
    wq = w_q_up.reshape(Q_RANK, N_PAIRS, 2, QK_HEAD)
    wq_nope = wq[..., :QK_NOPE].reshape(Q_RANK, N_PAIRS, 2 * QK_NOPE)
    wq_rope = wq[..., QK_NOPE:]
    W['w_q'] = jnp.concatenate(
        [wq_nope, wq_rope.reshape(Q_RANK, N_PAIRS, 2 * QK_ROPE),
         _rot_half_cols(wq_rope).reshape(Q_RANK, N_PAIRS, 2 * QK_ROPE)], axis=-1
    ).reshape(Q_RANK, N_PAIRS * 4 * LANE).astype(BF16)
    wkv = w_kv_up.reshape(KV_RANK, N_PAIRS, 2, QK_NOPE + V_HEAD)
    W['w_kv'] = jnp.concatenate(
        [wkv[..., :QK_NOPE].reshape(KV_RANK, N_PAIRS, 2 * QK_NOPE),
         wkv[..., QK_NOPE:].reshape(KV_RANK, N_PAIRS, 2 * V_HEAD)], axis=-1
    ).transpose(1, 0, 2).astype(BF16)
    dup = lambda v: jnp.concatenate([v, v]).reshape(1, LANE).astype(F32)
    W['g_q_nope'] = row(g_q_norm[:QK_NOPE])
    W['g_q_rope2'] = dup(g_q_norm[QK_NOPE:])
    W['g_k_nope'] = row(g_k_norm[:QK_NOPE])
    W['g_k_rope2'] = dup(g_k_norm[QK_NOPE:])
    return W


def _rope_tables(pos):
    inv_freq = 1.0 / (ROPE_THETA ** (jnp.arange(0, QK_ROPE, 2, dtype=F32) / QK_ROPE))
    ang = pos.astype(F32)[:, None] * inv_freq[None, :]
    return jnp.tile(jnp.cos(ang), (1, 4)), jnp.tile(jnp.sin(ang), (1, 4))


def _layer(x, pos, past_ckv, past_kpe, past_conv, W):
    B, S, D = x.shape
    T = B * S
    x2d = x.reshape(T, D)
    c2, s2 = _rope_tables(pos)
    if B > 1:
        c2, s2 = jnp.tile(c2, (B, 1)), jnp.tile(s2, (B, 1))
    tm = min(T, 512)
    h, cq, ckv, ckvb, kpe, kpe2 = _inproj(x2d, c2, s2, W, tm)
    q = _q_up(cq, c2, s2, W, tm)
    if past_ckv is None:
        k, v = _kv_up(ckvb, kpe2, W, tm)
        attn = _flash_prompt(q, k, v, 512)
        past_conv = jnp.zeros((B, CONV_WIDTH - 1, CONV_CH), x.dtype)
    else:
        cache_kpe2 = jnp.concatenate([past_kpe, past_kpe], axis=-1)
        attn = _decode_attn(q, ckvb, kpe2, past_ckv, cache_kpe2, W, 1024)
    yb, conv_state = _conv_branch(h.reshape(B, S, D), past_conv, W, min(S, 256))
    m = _mix(h, attn, yb.reshape(T, D), W, tm, 512)
    x1 = _out_proj(x2d, m, W, tm, 512)
    y = _ffn(x1, W, tm, 512)
    return (y.reshape(B, S, D), ckv.reshape(B, S, KV_RANK), kpe.reshape(B, S, QK_ROPE), conv_state)


def kernel(x_prompt, x_sample, cache_ckv, cache_kpe, state_conv, g_mix_norm, w_in, b_glu, b_gate, g_q_a,
           w_q_up, g_q_norm, g_kv_a, w_kv_up, g_k_norm, w_attn_out, w_dw, b_dw, g_conv_ln, b_conv_ln,
           w_conv_out, b_conv_out, w_out, g_ffn_norm, w_ffn_gate, w_ffn_up, w_ffn_down):
    weights = (g_mix_norm, w_in, b_glu, b_gate, g_q_a, w_q_up, g_q_norm, g_kv_a, w_kv_up, g_k_norm,
               w_attn_out, w_dw, b_dw, g_conv_ln, b_conv_ln, w_conv_out, b_conv_out, w_out,
               g_ffn_norm, w_ffn_gate, w_ffn_up, w_ffn_down)
    depth = w_in.shape[0]
    pos_prompt = jnp.arange(x_prompt.shape[1])
    pos_sample = cache_ckv.shape[2] + jnp.arange(x_sample.shape[1])
    y_prompt, y_sample = x_prompt, x_sample
    outs = [[] for _ in range(6)]
    for l in range(depth):
        W = _prep_weights(tuple(w[l] for w in weights))
        y_prompt, ckv, kpe, conv = _layer(y_prompt, pos_prompt, None, None, None, W)
        outs[0].append(ckv); outs[1].append(kpe); outs[2].append(conv)
        y_sample, ckv, kpe, conv = _layer(y_sample, pos_sample, cache_ckv[l], cache_kpe[l], state_conv[l], W)
        outs[3].append(ckv); outs[4].append(kpe); outs[5].append(conv)
    return (y_prompt, y_sample) + tuple(jnp.stack(o) for o in outs)
```

```python
import functools
import math

import jax
import jax.numpy as jnp
from jax import lax
from jax.experimental import pallas as pl
from jax.experimental.pallas import tpu as pltpu

F32 = jnp.float32
BF16 = jnp.bfloat16

CHUNK = 64
N_HEADS = 16
Q_RANK = 512
KV_RANK = 512
QK_NOPE = 128
QK_ROPE = 64
QK_HEAD = QK_NOPE + QK_ROPE
V_HEAD = 128
CONV_CH = 1024
CONV_WIDTH = 31
ROPE_THETA = 10000.0
EPS = 1e-6
NEG_INF = -1e30
SCALE = QK_HEAD ** -0.5
LOG2E = math.log2(math.e)

LANE = 128
QK_PAD = 2 * LANE
N_PAIRS = N_HEADS // 2
PAST_PAD = 32
MIB = 1024 * 1024


def _params(semantics, vmem_mib):
    return pltpu.CompilerParams(dimension_semantics=semantics, vmem_limit_bytes=vmem_mib * MIB)


def _rms_scale(v, n):
    return lax.rsqrt(jnp.sum(v * v, axis=-1, keepdims=True) * (1.0 / n) + EPS)


def _half_masks():
    lane = lax.broadcasted_iota(jnp.int32, (1, LANE), 1)
    lo = (lane < QK_ROPE).astype(F32)
    return lo, 1.0 - lo


def _inproj_kernel(x_ref, g_ref, w_ref, gq_ref, gkv_ref, c2_ref, s2_ref,
                   h_ref, cq_ref, ckv_ref, ckvb_ref, kpe_ref, kpe2_ref):
    x = x_ref[...]
    h = (x * _rms_scale(x, x.shape[-1]) * g_ref[...]).astype(BF16)
    h_ref[...] = h
    z = jnp.dot(h, w_ref[...], preferred_element_type=F32)
    cq = z[:, :Q_RANK]
    cq_ref[...] = (cq * _rms_scale(cq, Q_RANK) * gq_ref[...]).astype(BF16)
    ckv = z[:, Q_RANK:Q_RANK + KV_RANK]
    ckv = ckv * _rms_scale(ckv, KV_RANK) * gkv_ref[...]
    ckv_ref[...] = ckv
    ckvb_ref[...] = ckv.astype(BF16)
    base = Q_RANK + KV_RANK
    kpe2 = z[:, base:base + LANE] * c2_ref[...] + z[:, base + LANE:base + 2 * LANE] * s2_ref[...]
    kpe2_ref[...] = kpe2
    kpe_ref[...] = kpe2[:, :QK_ROPE]


def _inproj(x2d, c2, s2, W, tm):
    T, D = x2d.shape
    n_in = W['w_small'].shape[1]
    row = lambda i: (i, 0)
    const = lambda i: (0, 0)
    return pl.pallas_call(
        _inproj_kernel,
        grid=(T // tm,),
        in_specs=[
            pl.BlockSpec((tm, D), row),
            pl.BlockSpec((1, D), const),
            pl.BlockSpec((D, n_in), const),
            pl.BlockSpec((1, Q_RANK), const),
            pl.BlockSpec((1, KV_RANK), const),
            pl.BlockSpec((tm, LANE), row),
            pl.BlockSpec((tm, LANE), row),
        ],
        out_specs=[
            pl.BlockSpec((tm, D), row),
            pl.BlockSpec((tm, Q_RANK), row),
            pl.BlockSpec((tm, KV_RANK), row),
            pl.BlockSpec((tm, KV_RANK), row),
            pl.BlockSpec((tm, QK_ROPE), row),
            pl.BlockSpec((tm, LANE), row),
        ],
        out_shape=[
            jax.ShapeDtypeStruct((T, D), BF16),
            jax.ShapeDtypeStruct((T, Q_RANK), BF16),
            jax.ShapeDtypeStruct((T, KV_RANK), F32),
            jax.ShapeDtypeStruct((T, KV_RANK), BF16),
            jax.ShapeDtypeStruct((T, QK_ROPE), F32),
            jax.ShapeDtypeStruct((T, LANE), F32),
        ],
        compiler_params=_params(("arbitrary",), 48),
        name="inproj",
    )(x2d, W['g_mix'], W['w_small'], W['g_q_a'], W['g_kv_a'], c2, s2)


def _qup_kernel(cq_ref, w_ref, c2_ref, s2_ref, gn_ref, gr2_ref, q_ref):
    cq = cq_ref[...]
    masks = _half_masks()
    for p in range(N_PAIRS):
        z = jnp.dot(cq, w_ref[:, p * 4 * LANE:(p + 1) * 4 * LANE], preferred_element_type=F32)
        rope2 = z[:, 2 * LANE:3 * LANE] * c2_ref[...] + z[:, 3 * LANE:] * s2_ref[...]
        for e, msk in enumerate(masks):
            nope = z[:, e * LANE:(e + 1) * LANE]
            rope = rope2 * msk
            ss = jnp.sum(nope * nope, axis=-1, keepdims=True) + jnp.sum(rope * rope, axis=-1, keepdims=True)
            r = lax.rsqrt(ss * (1.0 / QK_HEAD) + EPS) * (SCALE * LOG2E)
            q_ref[2 * p + e, :, :QK_NOPE] = (nope * r * gn_ref[...]).astype(BF16)
            q_ref[2 * p + e, :, QK_NOPE:] = (rope * r * gr2_ref[...]).astype(BF16)


def _q_up(cq, c2, s2, W, tm):
    T = cq.shape[0]
    row = lambda i: (i, 0)
    const = lambda i: (0, 0)
    return pl.pallas_call(
        _qup_kernel,
        grid=(T // tm,),
        in_specs=[
            pl.BlockSpec((tm, Q_RANK), row),
            pl.BlockSpec((Q_RANK, N_PAIRS * 4 * LANE), const),
            pl.BlockSpec((tm, LANE), row),
            pl.BlockSpec((tm, LANE), row),
            pl.BlockSpec((1, LANE), const),
            pl.BlockSpec((1, LANE), const),
        ],
        out_specs=pl.BlockSpec((N_HEADS, tm, QK_PAD), lambda i: (0, i, 0)),
        out_shape=jax.ShapeDtypeStruct((N_HEADS, T, QK_PAD), BF16),
        compiler_params=_params(("arbitrary",), 48),
        name="q_up",
    )(cq, W['w_q'], c2, s2, W['g_q_nope'], W['g_q_rope2'])


def _pair_keys_values(kv, kpe2, gn, gr2):
    lo, hi = _half_masks()
    ss_pe = jnp.sum(kpe2 * kpe2 * lo, axis=-1, keepdims=True)
    out = []
    for e, msk in enumerate((lo, hi)):
        kn = kv[:, e * LANE:(e + 1) * LANE]
        r = lax.rsqrt((jnp.sum(kn * kn, axis=-1, keepdims=True) + ss_pe) * (1.0 / QK_HEAD) + EPS)
        k = jnp.concatenate([kn * r * gn, kpe2 * msk * r * gr2], axis=-1).astype(BF16)
        v = kv[:, (2 + e) * LANE:(3 + e) * LANE].astype(BF16)
        out.append((k, v))
    return out


def _kvup_kernel(ckv_ref, kpe2_ref, w_ref, gn_ref, gr2_ref, k_ref, v_ref):
    ckv = ckv_ref[...]
    for p in range(N_PAIRS):
        kv = jnp.dot(ckv, w_ref[p], preferred_element_type=F32)
        for e, (k, v) in enumerate(_pair_keys_values(kv, kpe2_ref[...], gn_ref[...], gr2_ref[...])):
            k_ref[2 * p + e] = k
            v_ref[2 * p + e] = v


def _kv_up(ckvb, kpe2, W, tm):
    T = ckvb.shape[0]
    row = lambda i: (i, 0)
    const = lambda i: (0, 0)
    return pl.pallas_call(
        _kvup_kernel,
        grid=(T // tm,),
        in_specs=[
            pl.BlockSpec((tm, KV_RANK), row),
            pl.BlockSpec((tm, LANE), row),
            pl.BlockSpec((N_PAIRS, KV_RANK, 4 * LANE), lambda i: (0, 0, 0)),
            pl.BlockSpec((1, LANE), const),
            pl.BlockSpec((1, LANE), const),
        ],
        out_specs=[
            pl.BlockSpec((N_HEADS, tm, QK_PAD), lambda i: (0, i, 0)),
            pl.BlockSpec((N_HEADS, tm, V_HEAD), lambda i: (0, i, 0)),
        ],
        out_shape=[
            jax.ShapeDtypeStruct((N_HEADS, T, QK_PAD), BF16),
            jax.ShapeDtypeStruct((N_HEADS, T, V_HEAD), BF16),
        ],
        compiler_params=_params(("arbitrary",), 48),
        name="kv_up",
    )(ckvb, kpe2, W['w_kv'], W['g_k_nope'], W['g_k_rope2'])


def _online_softmax(s, m_ref, l_ref):
    m_prev = m_ref[...]
    blocks = [s[:, c:c + LANE] for c in range(0, s.shape[1], LANE)]
    m_next = jnp.maximum(m_prev, jnp.max(functools.reduce(jnp.maximum, blocks), axis=-1, keepdims=True))
    blocks = [jnp.exp2(b - m_next) for b in blocks]
    alpha = jnp.exp2(m_prev - m_next)
    l_ref[...] = alpha * l_ref[...] + functools.reduce(jnp.add, blocks)
    m_ref[...] = m_next
    return jnp.concatenate(blocks, axis=-1).astype(BF16), alpha


def _qk(q, k):
    return lax.dot_general(q, k, (((1,), (1,)), ((), ())), preferred_element_type=F32)


N_CHAINS = 2


def _flash_kernel(q_ref, k_ref, v_ref, o_ref, s_ref, m_ref, l_ref, acc_ref, *, tq):
    i = pl.program_id(1)
    n = i + 1
    q = q_ref[0]
    m_ref[...] = jnp.full(m_ref.shape, -jnp.inf, F32)
    l_ref[...] = jnp.zeros(l_ref.shape, F32)
    acc_ref[...] = jnp.zeros(acc_ref.shape, F32)

    def scores(j, chain):
        start = pl.multiple_of(j * tq, tq)
        s_ref[chain] = _qk(q, k_ref[0, pl.ds(start, tq), :])

    def consume(j, chain, diagonal=False):
        s = s_ref[chain]
        if diagonal:
            qchunk = lax.broadcasted_iota(jnp.int32, (tq, tq), 0) // CHUNK
            kchunk = lax.broadcasted_iota(jnp.int32, (tq, tq), 1) // CHUNK
            s = jnp.where(kchunk <= qchunk, s, NEG_INF)
        start = pl.multiple_of(j * tq, tq)
        p, alpha = _online_softmax(s, m_ref.at[chain], l_ref.at[chain])
        acc_ref[chain] = alpha * acc_ref[chain] + jnp.dot(p, v_ref[0, pl.ds(start, tq), :],
                                                          preferred_element_type=F32)

    scores(0, 0)

    @pl.when(n >= 2)
    def _():
        scores(1, 1)

    def pair(jj, carry):
        consume(2 * jj, 0)
        scores(2 * jj + 2, 0)
        consume(2 * jj + 1, 1)
        scores(2 * jj + 3, 1)
        return carry

    def two_pairs(jj, carry):
        pair(2 * jj, carry)
        pair(2 * jj + 1, carry)
        return carry

    n_pairs = jnp.maximum(n // 2 - 1, 0)
    lax.fori_loop(0, n_pairs // 2, two_pairs, 0)
    lax.fori_loop(2 * (n_pairs // 2), n_pairs, pair, 0)

    @pl.when(n % 2 == 0)
    def _():
        consume(n - 2, 0)
        consume(n - 1, 1, diagonal=True)

    @pl.when(jnp.logical_and(n % 2 == 1, n >= 3))
    def _():
        consume(n - 3, 0)
        scores(n - 1, 0)
        consume(n - 2, 1)
        consume(n - 1, 0, diagonal=True)

    @pl.when(n == 1)
    def _():
        consume(0, 0, diagonal=True)

    m = jnp.maximum(m_ref[0], m_ref[1])
    w0 = jnp.exp2(m_ref[0] - m)
    w1 = jnp.exp2(m_ref[1] - m)
    l = jnp.sum(w0 * l_ref[0] + w1 * l_ref[1], axis=-1, keepdims=True)
    o_ref[...] = ((w0 * acc_ref[0] + w1 * acc_ref[1]) / l).astype(BF16)


def _flash_prompt(q, k, v, tq):
    H, S, _ = q.shape
    return pl.pallas_call(
        functools.partial(_flash_kernel, tq=tq),
        grid=(H, S // tq),
        in_specs=[
            pl.BlockSpec((1, tq, QK_PAD), lambda h, i: (h, i, 0)),
            pl.BlockSpec((1, S, QK_PAD), lambda h, i: (h, 0, 0)),
            pl.BlockSpec((1, S, V_HEAD), lambda h, i: (h, 0, 0)),
        ],
        out_specs=pl.BlockSpec((tq, V_HEAD), lambda h, i: (i, h)),
        out_shape=jax.ShapeDtypeStruct((S, H * V_HEAD), BF16),
        scratch_shapes=[pltpu.VMEM((N_CHAINS, tq, tq), F32),
                        pltpu.VMEM((N_CHAINS, tq, LANE), F32), pltpu.VMEM((N_CHAINS, tq, LANE), F32),
                        pltpu.VMEM((N_CHAINS, tq, V_HEAD), F32)],
        compiler_params=_params(("arbitrary", "arbitrary"), 48),
        name="flash_prompt",
    )(q, k, v)


Q_LAT = KV_RANK + LANE
_NT = (((1,), (1,)), ((), ()))


def _qabsorb_kernel(q_ref, wkt_ref, gn_ref, gr2_ref, o_ref):
    q = q_ref[0]
    qn = (q[:, :QK_NOPE].astype(F32) * gn_ref[...]).astype(BF16)
    o_ref[0, :, :KV_RANK] = jnp.dot(qn, wkt_ref[...], preferred_element_type=F32).astype(BF16)
    o_ref[0, :, KV_RANK:] = (q[:, QK_NOPE:].astype(F32) * gr2_ref[...]).astype(BF16)


def _q_absorb(q, W):
    H, T, _ = q.shape
    return pl.pallas_call(
        _qabsorb_kernel,
        grid=(H,),
        in_specs=[
            pl.BlockSpec((1, T, QK_PAD), lambda h: (h, 0, 0)),
            pl.BlockSpec((QK_NOPE, KV_RANK), lambda h: (h, 0)),
            pl.BlockSpec((1, LANE), lambda h: (0, 0)),
            pl.BlockSpec((1, LANE), lambda h: (0, 0)),
        ],
        out_specs=pl.BlockSpec((1, T, Q_LAT), lambda h: (h, 0, 0)),
        out_shape=jax.ShapeDtypeStruct((H, T, Q_LAT), BF16),
        compiler_params=_params(("arbitrary",), 32),
        name="q_absorb",
    )(q, W['w_kT'], W['g_k_nope'], W['g_k_rope2'])


def _decode_kernel(q_ref, ckvn_ref, kpetn_ref, ckvc_ref, kpetc_ref, wkt_ref, wv_ref, o_ref,
                   m_ref, l_ref, acc_ref, p_ref, alpha_ref, ckvpad_ref, kpetpad_ref, *, n_tiles, t_new):
    j = pl.program_id(1)
    rows = N_HEADS * t_new
    qa = q_ref[...].reshape(rows, Q_LAT)

    def attend(ckv, kpet2, n_valid):
        n = ckv.shape[0]
        knt = lax.dot_general(wkt_ref[...], ckv, _NT, preferred_element_type=F32)
        t = (lax.dot_general(qa[:, :KV_RANK], ckv, _NT, preferred_element_type=F32)
             + jnp.dot(qa[:, KV_RANK:], kpet2.astype(BF16), preferred_element_type=F32))
        kpet = kpet2[:QK_ROPE]
        ss_pe = jnp.sum(kpet * kpet, axis=0, keepdims=True)
        valid = lax.broadcasted_iota(jnp.int32, (1, n), 1) < n_valid
        for hd in range(N_HEADS):
            kn = knt[hd * QK_NOPE:(hd + 1) * QK_NOPE]
            r = lax.rsqrt((jnp.sum(kn * kn, axis=0, keepdims=True) + ss_pe) * (1.0 / QK_HEAD) + EPS)
            s = t[hd * t_new:(hd + 1) * t_new] * r
            if n_valid < n:
                s = jnp.where(valid, s, NEG_INF)
            p, alpha = _online_softmax(s, m_ref.at[hd], l_ref.at[hd])
            p_ref[hd * t_new:(hd + 1) * t_new, :n] = p
            alpha_ref[hd * t_new:(hd + 1) * t_new, :] = alpha
        pv = jnp.dot(p_ref[:, :n], ckv, preferred_element_type=F32)
        alpha = alpha_ref[...]
        acc_ref[...] = jnp.concatenate([alpha] * (KV_RANK // LANE), axis=1) * acc_ref[...] + pv

    @pl.when(j == 0)
    def _():
        m_ref[...] = jnp.full(m_ref.shape, -jnp.inf, F32)
        l_ref[...] = jnp.zeros(l_ref.shape, F32)
        acc_ref[...] = jnp.zeros(acc_ref.shape, F32)
        ckvpad_ref[...] = jnp.zeros(ckvpad_ref.shape, BF16)
        ckvpad_ref[:t_new, :] = ckvn_ref[...]
        kpetpad_ref[...] = jnp.zeros(kpetpad_ref.shape, F32)
        kpetpad_ref[:, :t_new] = kpetn_ref[0]
        attend(ckvpad_ref[...], kpetpad_ref[...], t_new)

    tk = ckvc_ref.shape[1]
    attend(ckvc_ref[0].astype(BF16), kpetc_ref[0], tk)

    @pl.when(j == n_tiles - 1)
    def _():
        for hd in range(N_HEADS):
            l = jnp.sum(l_ref[hd], axis=-1, keepdims=True)
            lat = (acc_ref[hd * t_new:(hd + 1) * t_new, :] / l).astype(BF16)
            o_ref[:, hd * V_HEAD:(hd + 1) * V_HEAD] = jnp.dot(
                lat, wv_ref[hd], preferred_element_type=F32).astype(BF16)


def _decode_attn(qa, ckvb_new, kpet2_new, cache_ckv, cache_kpet2, W, tk):
    B, P, _ = cache_ckv.shape
    T = ckvb_new.shape[0]
    t_new = T // B
    n_tiles = P // tk
    rows = N_HEADS * t_new
    return pl.pallas_call(
        functools.partial(_decode_kernel, n_tiles=n_tiles, t_new=t_new),
        grid=(B, n_tiles),
        in_specs=[
            pl.BlockSpec((N_HEADS, t_new, Q_LAT), lambda b, j: (0, b, 0)),
            pl.BlockSpec((t_new, KV_RANK), lambda b, j: (b, 0)),
            pl.BlockSpec((1, LANE, t_new), lambda b, j: (b, 0, 0)),
            pl.BlockSpec((1, tk, KV_RANK), lambda b, j: (b, j, 0)),
            pl.BlockSpec((1, LANE, tk), lambda b, j: (b, 0, j)),
            pl.BlockSpec((N_HEADS * QK_NOPE, KV_RANK), lambda b, j: (0, 0)),
            pl.BlockSpec((N_HEADS, KV_RANK, V_HEAD), lambda b, j: (0, 0, 0)),
        ],
        out_specs=pl.BlockSpec((t_new, N_HEADS * V_HEAD), lambda b, j: (b, 0)),
        out_shape=jax.ShapeDtypeStruct((T, N_HEADS * V_HEAD), BF16),
        scratch_shapes=[
            pltpu.VMEM((N_HEADS, t_new, LANE), F32),
            pltpu.VMEM((N_HEADS, t_new, LANE), F32),
            pltpu.VMEM((rows, KV_RANK), F32),
            pltpu.VMEM((rows, tk), BF16),
            pltpu.VMEM((rows, LANE), F32),
            pltpu.VMEM((LANE, KV_RANK), BF16),
            pltpu.VMEM((LANE, LANE), F32),
        ],
        compiler_params=_params(("arbitrary", "arbitrary"), 56),
        name="decode_attn",
    )(qa, ckvb_new, kpet2_new, cache_ckv, cache_kpet2, W['w_kT'], W['w_v'])


CONV_ROWS = 64
CONV_COLS = 256
SUBLANE = 8


def _conv_kernel(h_ref, past_ref, wa_ref, wg_ref, ba_ref, bg_ref, wdw_ref, bdw_ref, gln_ref, bln_ref,
                 wpw_ref, bpw_ref, yb_ref, state_ref, buf_ref, shift_ref, y_ref, *, tm, n_t):
    t = pl.program_id(1)
    hist = CONV_WIDTH - 1
    off = PAST_PAD - hist

    @pl.when(t == 0)
    def _():
        buf_ref[0:PAST_PAD, :] = jnp.zeros((PAST_PAD, CONV_CH), F32)
        buf_ref[off:PAST_PAD, :] = past_ref[0]

    @pl.when(t > 0)
    def _():
        buf_ref[0:PAST_PAD, :] = buf_ref[tm:tm + PAST_PAD, :]

    h = h_ref[0]
    a = jnp.dot(h, wa_ref[...], preferred_element_type=F32) + ba_ref[...]
    g = jnp.dot(h, wg_ref[...], preferred_element_type=F32) + bg_ref[...]
    buf_ref[PAST_PAD:PAST_PAD + tm, :] = a * jax.nn.sigmoid(g)

    span = shift_ref.shape[1]
    for r in range(1, SUBLANE):
        shift_ref[r - 1] = buf_ref[r:r + span, :]
    rows = min(CONV_ROWS, tm)
    for r0 in range(0, tm, rows):
        for c0 in range(0, CONV_CH, CONV_COLS):
            cols = slice(c0, c0 + CONV_COLS)
            acc = jnp.zeros((rows, CONV_COLS), F32)
            for q in range(off, PAST_PAD + 1):
                r, base = q % SUBLANE, r0 + q - q % SUBLANE
                src = buf_ref if r == 0 else shift_ref.at[r - 1]
                acc = acc + wdw_ref[q - off:q - off + 1, cols] * src[base:base + rows, cols]
            y_ref[r0:r0 + rows, cols] = acc

    y = y_ref[...] + bdw_ref[...]
    yc = y - jnp.mean(y, axis=-1, keepdims=True)
    y = yc * lax.rsqrt(jnp.mean(yc * yc, axis=-1, keepdims=True) + EPS) * gln_ref[...] + bln_ref[...]
    y = y * jax.nn.sigmoid(y)
    yb = jnp.dot(y.astype(BF16), wpw_ref[...], preferred_element_type=F32) + bpw_ref[...]
    yb_ref[0] = yb.astype(BF16)

    @pl.when(t == n_t - 1)
    def _():
        state_ref[0] = buf_ref[tm + off:tm + PAST_PAD, :]


def _conv_branch(h3d, past, W, tm):
    B, S, D = h3d.shape
    n_t = S // tm
    hist = CONV_WIDTH - 1
    const = lambda b, t: (0, 0)
    return pl.pallas_call(
        functools.partial(_conv_kernel, tm=tm, n_t=n_t),
        grid=(B, n_t),
        in_specs=[
            pl.BlockSpec((1, tm, D), lambda b, t: (b, t, 0)),
            pl.BlockSpec((1, hist, CONV_CH), lambda b, t: (b, 0, 0)),
            pl.BlockSpec((D, CONV_CH), const),
            pl.BlockSpec((D, CONV_CH), const),
            pl.BlockSpec((1, CONV_CH), const),
            pl.BlockSpec((1, CONV_CH), const),
            pl.BlockSpec((CONV_WIDTH, CONV_CH), const),
            pl.BlockSpec((1, CONV_CH), const),
            pl.BlockSpec((1, CONV_CH), const),
            pl.BlockSpec((1, CONV_CH), const),
            pl.BlockSpec((CONV_CH, D), const),
            pl.BlockSpec((1, D), const),
        ],
        out_specs=[
            pl.BlockSpec((1, tm, D), lambda b, t: (b, t, 0)),
            pl.BlockSpec((1, hist, CONV_CH), lambda b, t: (b, 0, 0)),
        ],
        out_shape=[
            jax.ShapeDtypeStruct((B, S, D), BF16),
            jax.ShapeDtypeStruct((B, hist, CONV_CH), F32),
        ],
        scratch_shapes=[pltpu.VMEM((PAST_PAD + tm, CONV_CH), F32),
                        pltpu.VMEM((SUBLANE - 1, PAST_PAD + tm - SUBLANE, CONV_CH), F32),
                        pltpu.VMEM((tm, CONV_CH), F32)],
        compiler_params=_params(("arbitrary", "arbitrary"), 48),
        name="conv_branch",
    )(h3d, past, W['w_glu_a'], W['w_glu_g'], W['b_glu_a'], W['b_glu_g'], W['w_dw'], W['b_dw'],
      W['g_ln'], W['b_ln'], W['w_pw'], W['b_pw'])


def _mix_kernel(h_ref, attn_ref, yb_ref, wga_ref, wgb_ref, wao_ref, bga_ref, bgb_ref, m_ref):
    h = h_ref[...]
    ga = jax.nn.sigmoid(jnp.dot(h, wga_ref[...], preferred_element_type=F32) + bga_ref[...])
    gb = jax.nn.sigmoid(jnp.dot(h, wgb_ref[...], preferred_element_type=F32) + bgb_ref[...])
    ya = jnp.dot(attn_ref[...], wao_ref[...], preferred_element_type=F32)
    m_ref[...] = (ga * ya + gb * yb_ref[...].astype(F32)).astype(BF16)


def _mix(h, attn, yb, W, tm, tn):
    T, D = h.shape
    row = lambda i, j: (i, 0)
    col = lambda i, j: (0, j)
    blk = lambda i, j: (i, j)
    return pl.pallas_call(
        _mix_kernel,
        grid=(T // tm, D // tn),
        in_specs=[
            pl.BlockSpec((tm, D), row),
            pl.BlockSpec((tm, D), row),
            pl.BlockSpec((tm, tn), blk),
            pl.BlockSpec((D, tn), col),
            pl.BlockSpec((D, tn), col),
            pl.BlockSpec((D, tn), col),
            pl.BlockSpec((1, tn), col),
            pl.BlockSpec((1, tn), col),
        ],
        out_specs=pl.BlockSpec((tm, tn), blk),
        out_shape=jax.ShapeDtypeStruct((T, D), BF16),
        compiler_params=_params(("arbitrary", "arbitrary"), 48),
        name="gated_mix",
    )(h, attn, yb, W['w_gate_a'], W['w_gate_b'], W['w_attn_out'], W['b_gate_a'], W['b_gate_b'])


def _outproj_kernel(x_ref, m_ref, w_ref, o_ref):
    o_ref[...] = x_ref[...] + jnp.dot(m_ref[...], w_ref[...], preferred_element_type=F32)


def _out_proj(x2d, m, W, tm, tn):
    T, D = x2d.shape
    return pl.pallas_call(
        _outproj_kernel,
        grid=(T // tm, D // tn),
        in_specs=[
            pl.BlockSpec((tm, tn), lambda i, j: (i, j)),
            pl.BlockSpec((tm, D), lambda i, j: (i, 0)),
            pl.BlockSpec((D, tn), lambda i, j: (0, j)),
        ],
        out_specs=pl.BlockSpec((tm, tn), lambda i, j: (i, j)),
        out_shape=jax.ShapeDtypeStruct((T, D), F32),
        compiler_params=_params(("arbitrary", "arbitrary"), 48),
        name="out_proj",
    )(x2d, m, W['w_out'])


def _ffn_kernel(x_ref, g_ref, wg_ref, wu_ref, wd_ref, o_ref, h_ref):
    @pl.when(pl.program_id(1) == 0)
    def _():
        x = x_ref[...]
        h_ref[...] = (x * _rms_scale(x, x.shape[-1]) * g_ref[...]).astype(BF16)
        o_ref[...] = x

    h = h_ref[...]
    gate = jnp.dot(h, wg_ref[...], preferred_element_type=F32)
    up = jnp.dot(h, wu_ref[...], preferred_element_type=F32)
    act = (gate * jax.nn.sigmoid(gate) * up).astype(BF16)
    o_ref[...] += jnp.dot(act, wd_ref[...], preferred_element_type=F32)


def _ffn(x2d, W, tm, tf):
    T, D = x2d.shape
    d_ff = W['w_ffn_gate'].shape[1]
    return pl.pallas_call(
        _ffn_kernel,
        grid=(T // tm, d_ff // tf),
        in_specs=[
            pl.BlockSpec((tm, D), lambda i, j: (i, 0)),
            pl.BlockSpec((1, D), lambda i, j: (0, 0)),
            pl.BlockSpec((D, tf), lambda i, j: (0, j)),
            pl.BlockSpec((D, tf), lambda i, j: (0, j)),
            pl.BlockSpec((tf, D), lambda i, j: (j, 0)),
        ],
        out_specs=pl.BlockSpec((tm, D), lambda i, j: (i, 0)),
        out_shape=jax.ShapeDtypeStruct((T, D), F32),
        scratch_shapes=[pltpu.VMEM((tm, D), BF16)],
        compiler_params=_params(("arbitrary", "arbitrary"), 48),
        name="ffn",
    )(x2d, W['g_ffn'], W['w_ffn_gate'], W['w_ffn_up'], W['w_ffn_down'])


def _rot_half_cols(w):
    half = QK_ROPE // 2
    return jnp.concatenate([-w[..., half:], w[..., :half]], axis=-1)


def _prep_weights(lw):
    (g_mix_norm, w_in, b_glu, b_gate, g_q_a, w_q_up, g_q_norm, g_kv_a, w_kv_up, g_k_norm,
     w_attn_out, w_dw, b_dw, g_conv_ln, b_conv_ln, w_conv_out, b_conv_out, w_out,
     g_ffn_norm, w_ffn_gate, w_ffn_up, w_ffn_down) = lw
    D = w_in.shape[0]
    o_kv = Q_RANK
    o_pe = o_kv + KV_RANK
    o_glu = o_pe + QK_ROPE
    o_gate = o_glu + 2 * CONV_CH
    w_pe = w_in[:, o_pe:o_glu]
    w_pe_rot = _rot_half_cols(w_pe)
    row = lambda v: v.reshape(1, -1).astype(F32)
    W = {
        'g_mix': row(g_mix_norm),
        'w_small': jnp.concatenate([w_in[:, :o_pe], w_pe, w_pe, w_pe_rot, w_pe_rot], axis=1).astype(BF16),
        'g_q_a': row(g_q_a),
        'g_kv_a': row(g_kv_a),
        'w_glu_a': w_in[:, o_glu:o_glu + CONV_CH].astype(BF16),
        'w_glu_g': w_in[:, o_glu + CONV_CH:o_gate].astype(BF16),
        'b_glu_a': row(b_glu[:CONV_CH]),
        'b_glu_g': row(b_glu[CONV_CH:]),
        'w_gate_a': w_in[:, o_gate:o_gate + D].astype(BF16),
        'w_gate_b': w_in[:, o_gate + D:].astype(BF16),
        'b_gate_a': row(b_gate[:D]),
        'b_gate_b': row(b_gate[D:]),
        'w_attn_out': w_attn_out.astype(BF16),
        'w_dw': w_dw.astype(F32),
        'b_dw': row(b_dw),
        'g_ln': row(g_conv_ln),
        'b_ln': row(b_conv_ln),
        'w_pw': w_conv_out.astype(BF16),
        'b_pw': row(b_conv_out),
        'w_out': w_out.astype(BF16),
        'g_ffn': row(g_ffn_norm),
        'w_ffn_gate': w_ffn_gate.astype(BF16),
        'w_ffn_up': w_ffn_up.astype(BF16),
        'w_ffn_down': w_ffn_down.astype(BF16),
    }
    wq = w_q_up.reshape(Q_RANK, N_PAIRS, 2, QK_HEAD)
    wq_nope = wq[..., :QK_NOPE].reshape(Q_RANK, N_PAIRS, 2 * QK_NOPE)
    wq_rope = wq[..., QK_NOPE:]
    W['w_q'] = jnp.concatenate(
        [wq_nope, wq_rope.reshape(Q_RANK, N_PAIRS, 2 * QK_ROPE),
         _rot_half_cols(wq_rope).reshape(Q_RANK, N_PAIRS, 2 * QK_ROPE)], axis=-1
    ).reshape(Q_RANK, N_PAIRS * 4 * LANE).astype(BF16)
    wkv = w_kv_up.reshape(KV_RANK, N_PAIRS, 2, QK_NOPE + V_HEAD)
    W['w_kv'] = jnp.concatenate(
        [wkv[..., :QK_NOPE].reshape(KV_RANK, N_PAIRS, 2 * QK_NOPE),
         wkv[..., QK_NOPE:].reshape(KV_RANK, N_PAIRS, 2 * V_HEAD)], axis=-1
    ).transpose(1, 0, 2).astype(BF16)
    W['w_kT'] = wkv[..., :QK_NOPE].reshape(KV_RANK, N_HEADS * QK_NOPE).T.astype(BF16)
    W['w_v'] = wkv[..., QK_NOPE:].reshape(KV_RANK, N_HEADS, V_HEAD).transpose(1, 0, 2).astype(BF16)
    dup = lambda v: jnp.concatenate([v, v]).reshape(1, LANE).astype(F32)
    W['g_q_nope'] = row(g_q_norm[:QK_NOPE])
    W['g_q_rope2'] = dup(g_q_norm[QK_NOPE:])
    W['g_k_nope'] = row(g_k_norm[:QK_NOPE])
    W['g_k_rope2'] = dup(g_k_norm[QK_NOPE:])
    return W


def _rope_tables(pos):
    inv_freq = 1.0 / (ROPE_THETA ** (jnp.arange(0, QK_ROPE, 2, dtype=F32) / QK_ROPE))
    ang = pos.astype(F32)[:, None] * inv_freq[None, :]
    return jnp.tile(jnp.cos(ang), (1, 4)), jnp.tile(jnp.sin(ang), (1, 4))


def _layer(x, pos, past_ckv, past_kpe, past_conv, W):
    B, S, D = x.shape
    T = B * S
    x2d = x.reshape(T, D)
    c2, s2 = _rope_tables(pos)
    if B > 1:
        c2, s2 = jnp.tile(c2, (B, 1)), jnp.tile(s2, (B, 1))
    tm = min(T, 512)
    h, cq, ckv, ckvb, kpe, kpe2 = _inproj(x2d, c2, s2, W, tm)
    q = _q_up(cq, c2, s2, W, tm)
    if past_ckv is None:
        k, v = _kv_up(ckvb, kpe2, W, tm)
        attn = _flash_prompt(q, k, v, 512)
        past_conv = jnp.zeros((B, CONV_WIDTH - 1, CONV_CH), x.dtype)
    else:
        cache_kpet = jnp.swapaxes(past_kpe, 1, 2)
        cache_kpet2 = jnp.concatenate([cache_kpet, cache_kpet], axis=1)
        kpet2_new = jnp.swapaxes(kpe2.reshape(B, S, LANE), 1, 2)
        attn = _decode_attn(_q_absorb(q, W), ckvb, kpet2_new, past_ckv, cache_kpet2, W, 1024)
    yb, conv_state = _conv_branch(h.reshape(B, S, D), past_conv, W, min(S, 256))
    m = _mix(h, attn, yb.reshape(T, D), W, tm, 512)
    x1 = _out_proj(x2d, m, W, tm, 512)
    y = _ffn(x1, W, tm, 512)
    return (y.reshape(B, S, D), ckv.reshape(B, S, KV_RANK), kpe.reshape(B, S, QK_ROPE), conv_state)


def kernel(x_prompt, x_sample, cache_ckv, cache_kpe, state_conv, g_mix_norm, w_in, b_glu, b_gate, g_q_a,
           w_q_up, g_q_norm, g_kv_a, w_kv_up, g_k_norm, w_attn_out, w_dw, b_dw, g_conv_ln, b_conv_ln,
           w_conv_out, b_conv_out, w_out, g_ffn_norm, w_ffn_gate, w_ffn_up, w_ffn_down):
    weights = (g_mix_norm, w_in, b_glu, b_gate, g_q_a, w_q_up, g_q_norm, g_kv_a, w_kv_up, g_k_norm,
               w_attn_out, w_dw, b_dw, g_conv_ln, b_conv_ln, w_conv_out, b_conv_out, w_out,
               g_ffn_norm, w_ffn_gate, w_ffn_up, w_ffn_down)
    depth = w_in.shape[0]
    pos_prompt = jnp.arange(x_prompt.shape[1])
    pos_sample = cache_ckv.shape[2] + jnp.arange(x_sample.shape[1])
    y_prompt, y_sample = x_prompt, x_sample
    outs = [[] for _ in range(6)]
    for l in range(depth):
        W = _prep_weights(tuple(w[l] for w in weights))
        y_prompt, ckv, kpe, conv = _layer(y_prompt, pos_prompt, None, None, None, W)
        outs[0].append(ckv); outs[1].append(kpe); outs[2].append(conv)
        y_sample, ckv, kpe, conv = _layer(y_sample, pos_sample, cache_ckv[l], cache_kpe[l], state_conv[l], W)
        outs[3].append(ckv); outs[4].append(kpe); outs[5].append(conv)
    return (y_prompt, y_sample) + tuple(jnp.stack(o) for o in outs)
```

```python
import functools
import math

import jax
import jax.numpy as jnp
from jax import lax
from jax.experimental import pallas as pl
from jax.experimental.pallas import tpu as pltpu

F32 = jnp.float32
BF16 = jnp.bfloat16

CHUNK = 64
N_HEADS = 16
Q_RANK = 512
KV_RANK = 512
QK_NOPE = 128
QK_ROPE = 64
QK_HEAD = QK_NOPE + QK_ROPE
V_HEAD = 128
CONV_CH = 1024
CONV_WIDTH = 31
ROPE_THETA = 10000.0
EPS = 1e-6
NEG_INF = -1e30
SCALE = QK_HEAD ** -0.5
LOG2E = math.log2(math.e)

LANE = 128
QK_PAD = 2 * LANE
N_PAIRS = N_HEADS // 2
PAST_PAD = 32
MIB = 1024 * 1024


def _params(semantics, vmem_mib):
    return pltpu.CompilerParams(dimension_semantics=semantics, vmem_limit_bytes=vmem_mib * MIB)


def _rms_scale(v, n):
    return lax.rsqrt(jnp.sum(v * v, axis=-1, keepdims=True) * (1.0 / n) + EPS)


def _half_masks():
    lane = lax.broadcasted_iota(jnp.int32, (1, LANE), 1)
    lo = (lane < QK_ROPE).astype(F32)
    return lo, 1.0 - lo


def _inproj_kernel(x_ref, g_ref, w_ref, gq_ref, gkv_ref, c2_ref, s2_ref,
                   h_ref, cq_ref, ckv_ref, ckvb_ref, kpe_ref, kpe2_ref):
    x = x_ref[...]
    h = (x * _rms_scale(x, x.shape[-1]) * g_ref[...]).astype(BF16)
    h_ref[...] = h
    z = jnp.dot(h, w_ref[...], preferred_element_type=F32)
    cq = z[:, :Q_RANK]
    cq_ref[...] = (cq * _rms_scale(cq, Q_RANK) * gq_ref[...]).astype(BF16)
    ckv = z[:, Q_RANK:Q_RANK + KV_RANK]
    ckv = ckv * _rms_scale(ckv, KV_RANK) * gkv_ref[...]
    ckv_ref[...] = ckv
    ckvb_ref[...] = ckv.astype(BF16)
    base = Q_RANK + KV_RANK
    kpe2 = z[:, base:base + LANE] * c2_ref[...] + z[:, base + LANE:base + 2 * LANE] * s2_ref[...]
    kpe2_ref[...] = kpe2
    kpe_ref[...] = kpe2[:, :QK_ROPE]


def _inproj(x2d, c2, s2, W, tm):
    T, D = x2d.shape
    n_in = W['w_small'].shape[1]
    row = lambda i: (i, 0)
    const = lambda i: (0, 0)
    return pl.pallas_call(
        _inproj_kernel,
        grid=(T // tm,),
        in_specs=[
            pl.BlockSpec((tm, D), row),
            pl.BlockSpec((1, D), const),
            pl.BlockSpec((D, n_in), const),
            pl.BlockSpec((1, Q_RANK), const),
            pl.BlockSpec((1, KV_RANK), const),
            pl.BlockSpec((tm, LANE), row),
            pl.BlockSpec((tm, LANE), row),
        ],
        out_specs=[
            pl.BlockSpec((tm, D), row),
            pl.BlockSpec((tm, Q_RANK), row),
            pl.BlockSpec((tm, KV_RANK), row),
            pl.BlockSpec((tm, KV_RANK), row),
            pl.BlockSpec((tm, QK_ROPE), row),
            pl.BlockSpec((tm, LANE), row),
        ],
        out_shape=[
            jax.ShapeDtypeStruct((T, D), BF16),
            jax.ShapeDtypeStruct((T, Q_RANK), BF16),
            jax.ShapeDtypeStruct((T, KV_RANK), F32),
            jax.ShapeDtypeStruct((T, KV_RANK), BF16),
            jax.ShapeDtypeStruct((T, QK_ROPE), F32),
            jax.ShapeDtypeStruct((T, LANE), F32),
        ],
        compiler_params=_params(("arbitrary",), 48),
        name="inproj",
    )(x2d, W['g_mix'], W['w_small'], W['g_q_a'], W['g_kv_a'], c2, s2)


def _qup_kernel(cq_ref, w_ref, c2_ref, s2_ref, gn_ref, gr2_ref, q_ref, *, transposed):
    cq = cq_ref[...]
    masks = _half_masks()
    for p in range(N_PAIRS):
        z = jnp.dot(cq, w_ref[:, p * 4 * LANE:(p + 1) * 4 * LANE], preferred_element_type=F32)
        rope2 = z[:, 2 * LANE:3 * LANE] * c2_ref[...] + z[:, 3 * LANE:] * s2_ref[...]
        for e, msk in enumerate(masks):
            nope = z[:, e * LANE:(e + 1) * LANE]
            rope = rope2 * msk
            ss = jnp.sum(nope * nope, axis=-1, keepdims=True) + jnp.sum(rope * rope, axis=-1, keepdims=True)
            r = lax.rsqrt(ss * (1.0 / QK_HEAD) + EPS) * (SCALE * LOG2E)
            if transposed:
                q_ref[2 * p + e, 0, :QK_NOPE, :] = (nope * r * gn_ref[...]).T.astype(BF16)
                q_ref[2 * p + e, 0, QK_NOPE:, :] = (rope * r * gr2_ref[...]).T.astype(BF16)
            else:
                q_ref[2 * p + e, :, :QK_NOPE] = (nope * r * gn_ref[...]).astype(BF16)
                q_ref[2 * p + e, :, QK_NOPE:] = (rope * r * gr2_ref[...]).astype(BF16)


def _q_up(cq, c2, s2, W, tm, transposed):
    T = cq.shape[0]
    row = lambda i: (i, 0)
    const = lambda i: (0, 0)
    if transposed:
        out_spec = pl.BlockSpec((N_HEADS, 1, QK_PAD, tm), lambda i: (0, i, 0, 0))
        out_shape = jax.ShapeDtypeStruct((N_HEADS, T // tm, QK_PAD, tm), BF16)
    else:
        out_spec = pl.BlockSpec((N_HEADS, tm, QK_PAD), lambda i: (0, i, 0))
        out_shape = jax.ShapeDtypeStruct((N_HEADS, T, QK_PAD), BF16)
    return pl.pallas_call(
        functools.partial(_qup_kernel, transposed=transposed),
        grid=(T // tm,),
        in_specs=[
            pl.BlockSpec((tm, Q_RANK), row),
            pl.BlockSpec((Q_RANK, N_PAIRS * 4 * LANE), const),
            pl.BlockSpec((tm, LANE), row),
            pl.BlockSpec((tm, LANE), row),
            pl.BlockSpec((1, LANE), const),
            pl.BlockSpec((1, LANE), const),
        ],
        out_specs=out_spec,
        out_shape=out_shape,
        compiler_params=_params(("arbitrary",), 48),
        name="q_up",
    )(cq, W['w_q'], c2, s2, W['g_q_nope'], W['g_q_rope2'])


V_ROWS = V_HEAD + 16


def _kvup_kernel(ckv_ref, kpe2_ref, w_ref, gn_ref, gr2_ref, k_ref, v_ref):
    ckv = ckv_ref[...]
    kpe2 = kpe2_ref[...]
    lo, hi = _half_masks()
    ss_pe = jnp.sum(kpe2 * kpe2 * lo, axis=-1, keepdims=True)
    tm = ckv.shape[0]
    for p in range(N_PAIRS):
        kv = jnp.dot(ckv, w_ref[p], preferred_element_type=F32)
        for e, msk in enumerate((lo, hi)):
            hd = 2 * p + e
            kn = kv[:, e * LANE:(e + 1) * LANE]
            r = lax.rsqrt((jnp.sum(kn * kn, axis=-1, keepdims=True) + ss_pe) * (1.0 / QK_HEAD) + EPS)
            k_ref[hd, :, :QK_NOPE] = (kn * r * gn_ref[...]).astype(BF16)
            k_ref[hd, :, QK_NOPE:] = (kpe2 * msk * r * gr2_ref[...]).astype(BF16)
            v_ref[hd, 0, :V_HEAD, :] = kv[:, (2 + e) * LANE:(3 + e) * LANE].T.astype(BF16)
            v_ref[hd, 0, V_HEAD:, :] = jnp.ones((V_ROWS - V_HEAD, tm), BF16)


def _kv_up(ckvb, kpe2, W, tm):
    T = ckvb.shape[0]
    row = lambda i: (i, 0)
    const = lambda i: (0, 0)
    return pl.pallas_call(
        _kvup_kernel,
        grid=(T // tm,),
        in_specs=[
            pl.BlockSpec((tm, KV_RANK), row),
            pl.BlockSpec((tm, LANE), row),
            pl.BlockSpec((N_PAIRS, KV_RANK, 4 * LANE), lambda i: (0, 0, 0)),
            pl.BlockSpec((1, LANE), const),
            pl.BlockSpec((1, LANE), const),
        ],
        out_specs=[
            pl.BlockSpec((N_HEADS, tm, QK_PAD), lambda i: (0, i, 0)),
            pl.BlockSpec((N_HEADS, 1, V_ROWS, tm), lambda i: (0, i, 0, 0)),
        ],
        out_shape=[
            jax.ShapeDtypeStruct((N_HEADS, T, QK_PAD), BF16),
            jax.ShapeDtypeStruct((N_HEADS, T // tm, V_ROWS, tm), BF16),
        ],
        compiler_params=_params(("arbitrary",), 48),
        name="kv_up",
    )(ckvb, kpe2, W['w_kv'], W['g_k_nope'], W['g_k_rope2'])


def _online_softmax(s, m_ref, l_ref):
    m_prev = m_ref[...]
    blocks = [s[:, c:c + LANE] for c in range(0, s.shape[1], LANE)]
    m_next = jnp.maximum(m_prev, jnp.max(functools.reduce(jnp.maximum, blocks), axis=-1, keepdims=True))
    blocks = [jnp.exp2(b - m_next) for b in blocks]
    alpha = jnp.exp2(m_prev - m_next)
    l_ref[...] = alpha * l_ref[...] + functools.reduce(jnp.add, blocks)
    m_ref[...] = m_next
    return jnp.concatenate(blocks, axis=-1).astype(BF16), alpha


N_CHAINS = 2
FLASH_UNROLLS = (4, 2, 1)


def _flash_kernel(qt_ref, k_ref, vt_ref, o_ref, s_ref, m_ref, acc_ref, *, tq):
    i = pl.program_id(1)
    n = i + 1
    qt = qt_ref[0, 0]
    m_ref[...] = jnp.full(m_ref.shape, -jnp.inf, F32)
    acc_ref[...] = jnp.zeros(acc_ref.shape, F32)

    def scores(j, chain):
        start = pl.multiple_of(j * tq, tq)
        s_ref[chain] = jnp.dot(k_ref[0, pl.ds(start, tq), :], qt, preferred_element_type=F32)

    def consume(j, chain, diagonal=False):
        s = s_ref[chain]
        if diagonal:
            kchunk = lax.broadcasted_iota(jnp.int32, (tq, tq), 0) // CHUNK
            qchunk = lax.broadcasted_iota(jnp.int32, (tq, tq), 1) // CHUNK
            s = jnp.where(kchunk <= qchunk, s, NEG_INF)
        m_prev = m_ref[chain]
        m_next = jnp.maximum(m_prev, jnp.max(s, axis=0, keepdims=True))
        p = jnp.exp2(s - m_next).astype(BF16)
        alpha = jnp.exp2(m_prev - m_next)
        acc_ref[chain] = alpha * acc_ref[chain] + jnp.dot(vt_ref[0, j], p, preferred_element_type=F32)
        m_ref[chain] = m_next

    scores(0, 0)

    @pl.when(n >= 2)
    def _():
        scores(1, 1)

    def pair(jj, carry):
        consume(2 * jj, 0)
        scores(2 * jj + 2, 0)
        consume(2 * jj + 1, 1)
        scores(2 * jj + 3, 1)
        return carry

    def pairs(count):
        def body(jj, carry):
            for u in range(count):
                pair(count * jj + u, carry)
            return carry
        return body

    n_pairs = jnp.maximum(n // 2 - 1, 0)
    done = 0
    for count in FLASH_UNROLLS:
        iters = (n_pairs - done) // count
        lax.fori_loop(done // count, done // count + iters, pairs(count), 0)
        done = done + iters * count

    @pl.when(n % 2 == 0)
    def _():
        consume(n - 2, 0)
        consume(n - 1, 1, diagonal=True)

    @pl.when(jnp.logical_and(n % 2 == 1, n >= 3))
    def _():
        consume(n - 3, 0)
        scores(n - 1, 0)
        consume(n - 2, 1)
        consume(n - 1, 0, diagonal=True)

    @pl.when(n == 1)
    def _():
        consume(0, 0, diagonal=True)

    m = jnp.maximum(m_ref[0], m_ref[1])
    acc = jnp.exp2(m_ref[0] - m) * acc_ref[0] + jnp.exp2(m_ref[1] - m) * acc_ref[1]
    out_t = acc[:V_HEAD] / acc[V_HEAD:V_HEAD + 1]
    o_ref[...] = out_t.T.astype(BF16)


def _flash_prompt(qt, k, vt):
    H, n_q, _, tq = qt.shape
    S = k.shape[1]
    return pl.pallas_call(
        functools.partial(_flash_kernel, tq=tq),
        grid=(H, n_q),
        in_specs=[
            pl.BlockSpec((1, 1, QK_PAD, tq), lambda h, i: (h, i, 0, 0)),
            pl.BlockSpec((1, S, QK_PAD), lambda h, i: (h, 0, 0)),
            pl.BlockSpec((1, n_q, V_ROWS, tq), lambda h, i: (h, 0, 0, 0)),
        ],
        out_specs=pl.BlockSpec((tq, V_HEAD), lambda h, i: (i, h)),
        out_shape=jax.ShapeDtypeStruct((S, H * V_HEAD), BF16),
        scratch_shapes=[pltpu.VMEM((N_CHAINS, tq, tq), F32),
                        pltpu.VMEM((N_CHAINS, 1, tq), F32),
                        pltpu.VMEM((N_CHAINS, V_ROWS, tq), F32)],
        compiler_params=_params(("arbitrary", "arbitrary"), 48),
        name="flash_prompt",
    )(qt, k, vt)


Q_LAT = KV_RANK + LANE
_NT = (((1,), (1,)), ((), ()))


def _qabsorb_kernel(q_ref, wkt_ref, gn_ref, gr2_ref, o_ref):
    q = q_ref[0]
    qn = (q[:, :QK_NOPE].astype(F32) * gn_ref[...]).astype(BF16)
    o_ref[0, :, :KV_RANK] = jnp.dot(qn, wkt_ref[...], preferred_element_type=F32).astype(BF16)
    o_ref[0, :, KV_RANK:] = (q[:, QK_NOPE:].astype(F32) * gr2_ref[...]).astype(BF16)


def _q_absorb(q, W):
    H, T, _ = q.shape
    return pl.pallas_call(
        _qabsorb_kernel,
        grid=(H,),
        in_specs=[
            pl.BlockSpec((1, T, QK_PAD), lambda h: (h, 0, 0)),
            pl.BlockSpec((QK_NOPE, KV_RANK), lambda h: (h, 0)),
            pl.BlockSpec((1, LANE), lambda h: (0, 0)),
            pl.BlockSpec((1, LANE), lambda h: (0, 0)),
        ],
        out_specs=pl.BlockSpec((1, T, Q_LAT), lambda h: (h, 0, 0)),
        out_shape=jax.ShapeDtypeStruct((H, T, Q_LAT), BF16),
        compiler_params=_params(("arbitrary",), 32),
        name="q_absorb",
    )(q, W['w_kT'], W['g_k_nope'], W['g_k_rope2'])


def _decode_kernel(q_ref, ckvn_ref, kpetn_ref, ckvc_ref, kpetc_ref, wkt_ref, wv_ref, o_ref,
                   m_ref, l_ref, acc_ref, p_ref, alpha_ref, ckvpad_ref, kpetpad_ref, *, n_tiles, t_new):
    j = pl.program_id(1)
    rows = N_HEADS * t_new
    qa = q_ref[...].reshape(rows, Q_LAT)

    def attend(ckv, kpet2, n_valid):
        n = ckv.shape[0]
        knt = lax.dot_general(wkt_ref[...], ckv, _NT, preferred_element_type=F32)
        t = (lax.dot_general(qa[:, :KV_RANK], ckv, _NT, preferred_element_type=F32)
             + jnp.dot(qa[:, KV_RANK:], kpet2.astype(BF16), preferred_element_type=F32))
        kpet = kpet2[:QK_ROPE]
        ss_pe = jnp.sum(kpet * kpet, axis=0, keepdims=True)
        valid = lax.broadcasted_iota(jnp.int32, (1, n), 1) < n_valid
        for hd in range(N_HEADS):
            kn = knt[hd * QK_NOPE:(hd + 1) * QK_NOPE]
            r = lax.rsqrt((jnp.sum(kn * kn, axis=0, keepdims=True) + ss_pe) * (1.0 / QK_HEAD) + EPS)
            s = t[hd * t_new:(hd + 1) * t_new] * r
            if n_valid < n:
                s = jnp.where(valid, s, NEG_INF)
            p, alpha = _online_softmax(s, m_ref.at[hd], l_ref.at[hd])
            p_ref[hd * t_new:(hd + 1) * t_new, :n] = p
            alpha_ref[hd * t_new:(hd + 1) * t_new, :] = alpha
        pv = jnp.dot(p_ref[:, :n], ckv, preferred_element_type=F32)
        alpha = alpha_ref[...]
        acc_ref[...] = jnp.concatenate([alpha] * (KV_RANK // LANE), axis=1) * acc_ref[...] + pv

    @pl.when(j == 0)
    def _():
        m_ref[...] = jnp.full(m_ref.shape, -jnp.inf, F32)
        l_ref[...] = jnp.zeros(l_ref.shape, F32)
        acc_ref[...] = jnp.zeros(acc_ref.shape, F32)
        ckvpad_ref[...] = jnp.zeros(ckvpad_ref.shape, BF16)
        ckvpad_ref[:t_new, :] = ckvn_ref[...]
        kpetpad_ref[...] = jnp.zeros(kpetpad_ref.shape, F32)
        kpetpad_ref[:, :t_new] = kpetn_ref[0]
        attend(ckvpad_ref[...], kpetpad_ref[...], t_new)

    tk = ckvc_ref.shape[1]
    attend(ckvc_ref[0].astype(BF16), kpetc_ref[0], tk)

    @pl.when(j == n_tiles - 1)
    def _():
        for hd in range(N_HEADS):
            l = jnp.sum(l_ref[hd], axis=-1, keepdims=True)
            lat = (acc_ref[hd * t_new:(hd + 1) * t_new, :] / l).astype(BF16)
            o_ref[:, hd * V_HEAD:(hd + 1) * V_HEAD] = jnp.dot(
                lat, wv_ref[hd], preferred_element_type=F32).astype(BF16)


def _decode_attn(qa, ckvb_new, kpet2_new, cache_ckv, cache_kpet2, W, tk):
    B, P, _ = cache_ckv.shape
    T = ckvb_new.shape[0]
    t_new = T // B
    n_tiles = P // tk
    rows = N_HEADS * t_new
    return pl.pallas_call(
        functools.partial(_decode_kernel, n_tiles=n_tiles, t_new=t_new),
        grid=(B, n_tiles),
        in_specs=[
            pl.BlockSpec((N_HEADS, t_new, Q_LAT), lambda b, j: (0, b, 0)),
            pl.BlockSpec((t_new, KV_RANK), lambda b, j: (b, 0)),
            pl.BlockSpec((1, LANE, t_new), lambda b, j: (b, 0, 0)),
            pl.BlockSpec((1, tk, KV_RANK), lambda b, j: (b, j, 0)),
            pl.BlockSpec((1, LANE, tk), lambda b, j: (b, 0, j)),
            pl.BlockSpec((N_HEADS * QK_NOPE, KV_RANK), lambda b, j: (0, 0)),
            pl.BlockSpec((N_HEADS, KV_RANK, V_HEAD), lambda b, j: (0, 0, 0)),
        ],
        out_specs=pl.BlockSpec((t_new, N_HEADS * V_HEAD), lambda b, j: (b, 0)),
        out_shape=jax.ShapeDtypeStruct((T, N_HEADS * V_HEAD), BF16),
        scratch_shapes=[
            pltpu.VMEM((N_HEADS, t_new, LANE), F32),
            pltpu.VMEM((N_HEADS, t_new, LANE), F32),
            pltpu.VMEM((rows, KV_RANK), F32),
            pltpu.VMEM((rows, tk), BF16),
            pltpu.VMEM((rows, LANE), F32),
            pltpu.VMEM((LANE, KV_RANK), BF16),
            pltpu.VMEM((LANE, LANE), F32),
        ],
        compiler_params=_params(("arbitrary", "arbitrary"), 56),
        name="decode_attn",
    )(qa, ckvb_new, kpet2_new, cache_ckv, cache_kpet2, W['w_kT'], W['w_v'])


CONV_ROWS = 64
CONV_COLS = 256
SUBLANE = 8


def _conv_kernel(h_ref, past_ref, wa_ref, wg_ref, ba_ref, bg_ref, wdw_ref, bdw_ref, gln_ref, bln_ref,
                 wpw_ref, bpw_ref, yb_ref, state_ref, buf_ref, shift_ref, y_ref, *, tm, n_t):
    t = pl.program_id(1)
    hist = CONV_WIDTH - 1
    off = PAST_PAD - hist

    @pl.when(t == 0)
    def _():
        buf_ref[0:PAST_PAD, :] = jnp.zeros((PAST_PAD, CONV_CH), F32)
        buf_ref[off:PAST_PAD, :] = past_ref[0]

    @pl.when(t > 0)
    def _():
        buf_ref[0:PAST_PAD, :] = buf_ref[tm:tm + PAST_PAD, :]

    h = h_ref[0]
    a = jnp.dot(h, wa_ref[...], preferred_element_type=F32) + ba_ref[...]
    g = jnp.dot(h, wg_ref[...], preferred_element_type=F32) + bg_ref[...]
    buf_ref[PAST_PAD:PAST_PAD + tm, :] = a * jax.nn.sigmoid(g)

    span = shift_ref.shape[1]
    for r in range(1, SUBLANE):
        shift_ref[r - 1] = buf_ref[r:r + span, :]
    rows = min(CONV_ROWS, tm)
    for r0 in range(0, tm, rows):
        for c0 in range(0, CONV_CH, CONV_COLS):
            cols = slice(c0, c0 + CONV_COLS)
            acc = jnp.zeros((rows, CONV_COLS), F32)
            for q in range(off, PAST_PAD + 1):
                r, base = q % SUBLANE, r0 + q - q % SUBLANE
                src = buf_ref if r == 0 else shift_ref.at[r - 1]
                acc = acc + wdw_ref[q - off:q - off + 1, cols] * src[base:base + rows, cols]
            y_ref[r0:r0 + rows, cols] = acc

    y = y_ref[...] + bdw_ref[...]
    yc = y - jnp.mean(y, axis=-1, keepdims=True)
    y = yc * lax.rsqrt(jnp.mean(yc * yc, axis=-1, keepdims=True) + EPS) * gln_ref[...] + bln_ref[...]
    y = y * jax.nn.sigmoid(y)
    yb = jnp.dot(y.astype(BF16), wpw_ref[...], preferred_element_type=F32) + bpw_ref[...]
    yb_ref[0] = yb.astype(BF16)

    @pl.when(t == n_t - 1)
    def _():
        state_ref[0] = buf_ref[tm + off:tm + PAST_PAD, :]


def _conv_branch(h3d, past, W, tm):
    B, S, D = h3d.shape
    n_t = S // tm
    hist = CONV_WIDTH - 1
    const = lambda b, t: (0, 0)
    return pl.pallas_call(
        functools.partial(_conv_kernel, tm=tm, n_t=n_t),
        grid=(B, n_t),
        in_specs=[
            pl.BlockSpec((1, tm, D), lambda b, t: (b, t, 0)),
            pl.BlockSpec((1, hist, CONV_CH), lambda b, t: (b, 0, 0)),
            pl.BlockSpec((D, CONV_CH), const),
            pl.BlockSpec((D, CONV_CH), const),
            pl.BlockSpec((1, CONV_CH), const),
            pl.BlockSpec((1, CONV_CH), const),
            pl.BlockSpec((CONV_WIDTH, CONV_CH), const),
            pl.BlockSpec((1, CONV_CH), const),
            pl.BlockSpec((1, CONV_CH), const),
            pl.BlockSpec((1, CONV_CH), const),
            pl.BlockSpec((CONV_CH, D), const),
            pl.BlockSpec((1, D), const),
        ],
        out_specs=[
            pl.BlockSpec((1, tm, D), lambda b, t: (b, t, 0)),
            pl.BlockSpec((1, hist, CONV_CH), lambda b, t: (b, 0, 0)),
        ],
        out_shape=[
            jax.ShapeDtypeStruct((B, S, D), BF16),
            jax.ShapeDtypeStruct((B, hist, CONV_CH), F32),
        ],
        scratch_shapes=[pltpu.VMEM((PAST_PAD + tm, CONV_CH), F32),
                        pltpu.VMEM((SUBLANE - 1, PAST_PAD + tm - SUBLANE, CONV_CH), F32),
                        pltpu.VMEM((tm, CONV_CH), F32)],
        compiler_params=_params(("arbitrary", "arbitrary"), 48),
        name="conv_branch",
    )(h3d, past, W['w_glu_a'], W['w_glu_g'], W['b_glu_a'], W['b_glu_g'], W['w_dw'], W['b_dw'],
      W['g_ln'], W['b_ln'], W['w_pw'], W['b_pw'])


def _mix_kernel(h_ref, attn_ref, yb_ref, wga_ref, wgb_ref, wao_ref, bga_ref, bgb_ref, m_ref):
    h = h_ref[...]
    ga = jax.nn.sigmoid(jnp.dot(h, wga_ref[...], preferred_element_type=F32) + bga_ref[...])
    gb = jax.nn.sigmoid(jnp.dot(h, wgb_ref[...], preferred_element_type=F32) + bgb_ref[...])
    ya = jnp.dot(attn_ref[...], wao_ref[...], preferred_element_type=F32)
    m_ref[...] = (ga * ya + gb * yb_ref[...].astype(F32)).astype(BF16)


def _mix(h, attn, yb, W, tm, tn):
    T, D = h.shape
    row = lambda i, j: (i, 0)
    col = lambda i, j: (0, j)
    blk = lambda i, j: (i, j)
    return pl.pallas_call(
        _mix_kernel,
        grid=(T // tm, D // tn),
        in_specs=[
            pl.BlockSpec((tm, D), row),
            pl.BlockSpec((tm, D), row),
            pl.BlockSpec((tm, tn), blk),
            pl.BlockSpec((D, tn), col),
            pl.BlockSpec((D, tn), col),
            pl.BlockSpec((D, tn), col),
            pl.BlockSpec((1, tn), col),
            pl.BlockSpec((1, tn), col),
        ],
        out_specs=pl.BlockSpec((tm, tn), blk),
        out_shape=jax.ShapeDtypeStruct((T, D), BF16),
        compiler_params=_params(("arbitrary", "arbitrary"), 48),
        name="gated_mix",
    )(h, attn, yb, W['w_gate_a'], W['w_gate_b'], W['w_attn_out'], W['b_gate_a'], W['b_gate_b'])


def _outproj_kernel(x_ref, m_ref, w_ref, o_ref):
    o_ref[...] = x_ref[...] + jnp.dot(m_ref[...], w_ref[...], preferred_element_type=F32)


def _out_proj(x2d, m, W, tm):
    T, D = x2d.shape
    return pl.pallas_call(
        _outproj_kernel,
        grid=(T // tm,),
        in_specs=[
            pl.BlockSpec((tm, D), lambda i: (i, 0)),
            pl.BlockSpec((tm, D), lambda i: (i, 0)),
            pl.BlockSpec((D, D), lambda i: (0, 0)),
        ],
        out_specs=pl.BlockSpec((tm, D), lambda i: (i, 0)),
        out_shape=jax.ShapeDtypeStruct((T, D), F32),
        compiler_params=_params(("arbitrary",), 48),
        name="out_proj",
    )(x2d, m, W['w_out'])


def _ffn_kernel(x_ref, g_ref, wg_ref, wu_ref, wd_ref, o_ref, h_ref):
    @pl.when(pl.program_id(1) == 0)
    def _():
        x = x_ref[...]
        h_ref[...] = (x * _rms_scale(x, x.shape[-1]) * g_ref[...]).astype(BF16)
        o_ref[...] = x

    h = h_ref[...]
    gate = jnp.dot(h, wg_ref[...], preferred_element_type=F32)
    up = jnp.dot(h, wu_ref[...], preferred_element_type=F32)
    act = (gate * jax.nn.sigmoid(gate) * up).astype(BF16)
    o_ref[...] += jnp.dot(act, wd_ref[...], preferred_element_type=F32)


def _ffn(x2d, W, tm, tf):
    T, D = x2d.shape
    d_ff = W['w_ffn_gate'].shape[1]
    return pl.pallas_call(
        _ffn_kernel,
        grid=(T // tm, d_ff // tf),
        in_specs=[
            pl.BlockSpec((tm, D), lambda i, j: (i, 0)),
            pl.BlockSpec((1, D), lambda i, j: (0, 0)),
            pl.BlockSpec((D, tf), lambda i, j: (0, j)),
            pl.BlockSpec((D, tf), lambda i, j: (0, j)),
            pl.BlockSpec((tf, D), lambda i, j: (j, 0)),
        ],
        out_specs=pl.BlockSpec((tm, D), lambda i, j: (i, 0)),
        out_shape=jax.ShapeDtypeStruct((T, D), F32),
        scratch_shapes=[pltpu.VMEM((tm, D), BF16)],
        compiler_params=_params(("arbitrary", "arbitrary"), 56),
        name="ffn",
    )(x2d, W['g_ffn'], W['w_ffn_gate'], W['w_ffn_up'], W['w_ffn_down'])


def _rot_half_cols(w):
    half = QK_ROPE // 2
    return jnp.concatenate([-w[..., half:], w[..., :half]], axis=-1)


def _prep_weights(lw):
    (g_mix_norm, w_in, b_glu, b_gate, g_q_a, w_q_up, g_q_norm, g_kv_a, w_kv_up, g_k_norm,
     w_attn_out, w_dw, b_dw, g_conv_ln, b_conv_ln, w_conv_out, b_conv_out, w_out,
     g_ffn_norm, w_ffn_gate, w_ffn_up, w_ffn_down) = lw
    D = w_in.shape[0]
    o_kv = Q_RANK
    o_pe = o_kv + KV_RANK
    o_glu = o_pe + QK_ROPE
    o_gate = o_glu + 2 * CONV_CH
    w_pe = w_in[:, o_pe:o_glu]
    w_pe_rot = _rot_half_cols(w_pe)
    row = lambda v: v.reshape(1, -1).astype(F32)
    W = {
        'g_mix': row(g_mix_norm),
        'w_small': jnp.concatenate([w_in[:, :o_pe], w_pe, w_pe, w_pe_rot, w_pe_rot], axis=1).astype(BF16),
        'g_q_a': row(g_q_a),
        'g_kv_a': row(g_kv_a),
        'w_glu_a': w_in[:, o_glu:o_glu + CONV_CH].astype(BF16),
        'w_glu_g': w_in[:, o_glu + CONV_CH:o_gate].astype(BF16),
        'b_glu_a': row(b_glu[:CONV_CH]),
        'b_glu_g': row(b_glu[CONV_CH:]),
        'w_gate_a': w_in[:, o_gate:o_gate + D].astype(BF16),
        'w_gate_b': w_in[:, o_gate + D:].astype(BF16),
        'b_gate_a': row(b_gate[:D]),
        'b_gate_b': row(b_gate[D:]),
        'w_attn_out': w_attn_out.astype(BF16),
        'w_dw': w_dw.astype(F32),
        'b_dw': row(b_dw),
        'g_ln': row(g_conv_ln),
        'b_ln': row(b_conv_ln),
        'w_pw': w_conv_out.astype(BF16),
        'b_pw': row(b_conv_out),
        'w_out': w_out.astype(BF16),
        'g_ffn': row(g_ffn_norm),
        'w_ffn_gate': w_ffn_gate.astype(BF16),
        'w_ffn_up': w_ffn_up.astype(BF16),
        'w_ffn_down': w_ffn_down.astype(BF16),
    }
    wq = w_q_up.reshape(Q_RANK, N_PAIRS, 2, QK_HEAD)
    wq_nope = wq[..., :QK_NOPE].reshape(Q_RANK, N_PAIRS, 2 * QK_NOPE)
    wq_rope = wq[..., QK_NOPE:]
    W['w_q'] = jnp.concatenate(
        [wq_nope, wq_rope.reshape(Q_RANK, N_PAIRS, 2 * QK_ROPE),
         _rot_half_cols(wq_rope).reshape(Q_RANK, N_PAIRS, 2 * QK_ROPE)], axis=-1
    ).reshape(Q_RANK, N_PAIRS * 4 * LANE).astype(BF16)
    wkv = w_kv_up.reshape(KV_RANK, N_PAIRS, 2, QK_NOPE + V_HEAD)
    W['w_kv'] = jnp.concatenate(
        [wkv[..., :QK_NOPE].reshape(KV_RANK, N_PAIRS, 2 * QK_NOPE),
         wkv[..., QK_NOPE:].reshape(KV_RANK, N_PAIRS, 2 * V_HEAD)], axis=-1
    ).transpose(1, 0, 2).astype(BF16)
    W['w_kT'] = wkv[..., :QK_NOPE].reshape(KV_RANK, N_HEADS * QK_NOPE).T.astype(BF16)
    W['w_v'] = wkv[..., QK_NOPE:].reshape(KV_RANK, N_HEADS, V_HEAD).transpose(1, 0, 2).astype(BF16)
    dup = lambda v: jnp.concatenate([v, v]).reshape(1, LANE).astype(F32)
    W['g_q_nope'] = row(g_q_norm[:QK_NOPE])
    W['g_q_rope2'] = dup(g_q_norm[QK_NOPE:])
    W['g_k_nope'] = row(g_k_norm[:QK_NOPE])
    W['g_k_rope2'] = dup(g_k_norm[QK_NOPE:])
    return W


def _rope_tables(pos):
    inv_freq = 1.0 / (ROPE_THETA ** (jnp.arange(0, QK_ROPE, 2, dtype=F32) / QK_ROPE))
    ang = pos.astype(F32)[:, None] * inv_freq[None, :]
    return jnp.tile(jnp.cos(ang), (1, 4)), jnp.tile(jnp.sin(ang), (1, 4))


ROW_TILE = 512
WIDE_ROW_TILE = 1024
COL_TILE = 512
FFN_TILE = 256
CONV_ROW_TILE = 256
DECODE_KEY_TILE = 1024


def _layer(x, pos, past_ckv, past_kpe, past_conv, W):
    B, S, D = x.shape
    T = B * S
    x2d = x.reshape(T, D)
    c2, s2 = _rope_tables(pos)
    if B > 1:
        c2, s2 = jnp.tile(c2, (B, 1)), jnp.tile(s2, (B, 1))
    tm = min(T, ROW_TILE)
    tm_wide = min(T, WIDE_ROW_TILE)
    h, cq, ckv, ckvb, kpe, kpe2 = _inproj(x2d, c2, s2, W, tm)
    if past_ckv is None:
        qt = _q_up(cq, c2, s2, W, tm, transposed=True)
        k, vt = _kv_up(ckvb, kpe2, W, tm)
        attn = _flash_prompt(qt, k, vt)
        past_conv = jnp.zeros((B, CONV_WIDTH - 1, CONV_CH), x.dtype)
    else:
        q = _q_up(cq, c2, s2, W, tm, transposed=False)
        cache_kpet = jnp.swapaxes(past_kpe, 1, 2)
        cache_kpet2 = jnp.concatenate([cache_kpet, cache_kpet], axis=1)
        kpet2_new = jnp.swapaxes(kpe2.reshape(B, S, LANE), 1, 2)
        attn = _decode_attn(_q_absorb(q, W), ckvb, kpet2_new, past_ckv, cache_kpet2, W, DECODE_KEY_TILE)
    yb, conv_state = _conv_branch(h.reshape(B, S, D), past_conv, W, min(S, CONV_ROW_TILE))
    m = _mix(h, attn, yb.reshape(T, D), W, tm_wide, COL_TILE)
    x1 = _out_proj(x2d, m, W, tm)
    y = _ffn(x1, W, tm_wide, FFN_TILE)
    return (y.reshape(B, S, D), ckv.reshape(B, S, KV_RANK), kpe.reshape(B, S, QK_ROPE), conv_state)


def kernel(x_prompt, x_sample, cache_ckv, cache_kpe, state_conv, g_mix_norm, w_in, b_glu, b_gate, g_q_a,
           w_q_up, g_q_norm, g_kv_a, w_kv_up, g_k_norm, w_attn_out, w_dw, b_dw, g_conv_ln, b_conv_ln,
           w_conv_out, b_conv_out, w_out, g_ffn_norm, w_ffn_gate, w_ffn_up, w_ffn_down):
    weights = (g_mix_norm, w_in, b_glu, b_gate, g_q_a, w_q_up, g_q_norm, g_kv_a, w_kv_up, g_k_norm,
               w_attn_out, w_dw, b_dw, g_conv_ln, b_conv_ln, w_conv_out, b_conv_out, w_out,
               g_ffn_norm, w_ffn_gate, w_ffn_up, w_ffn_down)
    depth = w_in.shape[0]
    pos_prompt = jnp.arange(x_prompt.shape[1])
    pos_sample = cache_ckv.shape[2] + jnp.arange(x_sample.shape[1])
    y_prompt, y_sample = x_prompt, x_sample
    outs = [[] for _ in range(6)]
    for l in range(depth):
        W = _prep_weights(tuple(w[l] for w in weights))
        y_prompt, ckv, kpe, conv = _layer(y_prompt, pos_prompt, None, None, None, W)
        outs[0].append(ckv); outs[1].append(kpe); outs[2].append(conv)
        y_sample, ckv, kpe, conv = _layer(y_sample, pos_sample, cache_ckv[l], cache_kpe[l], state_conv[l], W)
        outs[3].append(ckv); outs[4].append(kpe); outs[5].append(conv)
    return (y_prompt, y_sample) + tuple(jnp.stack(o) for o in outs)
```

```python
import functools
import math

import jax
import jax.numpy as jnp
from jax import lax
from jax.experimental import pallas as pl
from jax.experimental.pallas import tpu as pltpu

F32 = jnp.float32
BF16 = jnp.bfloat16

CHUNK = 64
N_HEADS = 16
Q_RANK = 512
KV_RANK = 512
QK_NOPE = 128
QK_ROPE = 64
QK_HEAD = QK_NOPE + QK_ROPE
V_HEAD = 128
CONV_CH = 1024
CONV_WIDTH = 31
ROPE_THETA = 10000.0
EPS = 1e-6
NEG_INF = -1e30
SCALE = QK_HEAD ** -0.5
LOG2E = math.log2(math.e)

LANE = 128
QK_PAD = 2 * LANE
N_PAIRS = N_HEADS // 2
PAST_PAD = 32
MIB = 1024 * 1024
_NT = (((1,), (1,)), ((), ()))


def _params(semantics, vmem_mib):
    return pltpu.CompilerParams(dimension_semantics=semantics, vmem_limit_bytes=vmem_mib * MIB)


def _rms_scale(v, n):
    return lax.rsqrt(jnp.sum(v * v, axis=-1, keepdims=True) * (1.0 / n) + EPS)


def _half_masks():
    lane = lax.broadcasted_iota(jnp.int32, (1, LANE), 1)
    lo = (lane < QK_ROPE).astype(F32)
    return lo, 1.0 - lo


def _inproj_kernel(x_ref, g_ref, w_ref, gq_ref, gkv_ref, c2_ref, s2_ref,
                   h_ref, cq_ref, ckv_ref, ckvb_ref, kpe_ref, kpe2_ref):
    x = x_ref[...]
    h = (x * _rms_scale(x, x.shape[-1]) * g_ref[...]).astype(BF16)
    h_ref[...] = h
    z = lax.dot_general(h, w_ref[...], _NT, preferred_element_type=F32)
    cq = z[:, :Q_RANK]
    cq_ref[...] = (cq * _rms_scale(cq, Q_RANK) * gq_ref[...]).astype(BF16)
    ckv = z[:, Q_RANK:Q_RANK + KV_RANK]
    ckv = ckv * _rms_scale(ckv, KV_RANK) * gkv_ref[...]
    ckv_ref[...] = ckv
    ckvb_ref[...] = ckv.astype(BF16)
    base = Q_RANK + KV_RANK
    kpe2 = z[:, base:base + LANE] * c2_ref[...] + z[:, base + LANE:base + 2 * LANE] * s2_ref[...]
    kpe2_ref[...] = kpe2
    kpe_ref[...] = kpe2[:, :QK_ROPE]


def _inproj(x2d, c2, s2, W, tm):
    T, D = x2d.shape
    n_in = W['w_small_t'].shape[0]
    row = lambda i: (i, 0)
    const = lambda i: (0, 0)
    return pl.pallas_call(
        _inproj_kernel,
        grid=(T // tm,),
        in_specs=[
            pl.BlockSpec((tm, D), row),
            pl.BlockSpec((1, D), const),
            pl.BlockSpec((n_in, D), const),
            pl.BlockSpec((1, Q_RANK), const),
            pl.BlockSpec((1, KV_RANK), const),
            pl.BlockSpec((tm, LANE), row),
            pl.BlockSpec((tm, LANE), row),
        ],
        out_specs=[
            pl.BlockSpec((tm, D), row),
            pl.BlockSpec((tm, Q_RANK), row),
            pl.BlockSpec((tm, KV_RANK), row),
            pl.BlockSpec((tm, KV_RANK), row),
            pl.BlockSpec((tm, QK_ROPE), row),
            pl.BlockSpec((tm, LANE), row),
        ],
        out_shape=[
            jax.ShapeDtypeStruct((T, D), BF16),
            jax.ShapeDtypeStruct((T, Q_RANK), BF16),
            jax.ShapeDtypeStruct((T, KV_RANK), F32),
            jax.ShapeDtypeStruct((T, KV_RANK), BF16),
            jax.ShapeDtypeStruct((T, QK_ROPE), F32),
            jax.ShapeDtypeStruct((T, LANE), F32),
        ],
        compiler_params=_params(("arbitrary",), 48),
        name="inproj",
    )(x2d, W['g_mix'], W['w_small_t'], W['g_q_a'], W['g_kv_a'], c2, s2)


def _qup_kernel(cq_ref, w_ref, c2_ref, s2_ref, gn_ref, gr2_ref, q_ref, *, transposed):
    cq = cq_ref[...]
    masks = _half_masks()
    for p in range(N_PAIRS):
        z = jnp.dot(cq, w_ref[:, p * 4 * LANE:(p + 1) * 4 * LANE], preferred_element_type=F32)
        rope2 = z[:, 2 * LANE:3 * LANE] * c2_ref[...] + z[:, 3 * LANE:] * s2_ref[...]
        for e, msk in enumerate(masks):
            nope = z[:, e * LANE:(e + 1) * LANE]
            rope = rope2 * msk
            ss = jnp.sum(nope * nope, axis=-1, keepdims=True) + jnp.sum(rope * rope, axis=-1, keepdims=True)
            r = lax.rsqrt(ss * (1.0 / QK_HEAD) + EPS) * (SCALE * LOG2E)
            if transposed:
                q_ref[2 * p + e, 0, :QK_NOPE, :] = (nope * r * gn_ref[...]).T.astype(BF16)
                q_ref[2 * p + e, 0, QK_NOPE:, :] = (rope * r * gr2_ref[...]).T.astype(BF16)
            else:
                q_ref[2 * p + e, :, :QK_NOPE] = (nope * r * gn_ref[...]).astype(BF16)
                q_ref[2 * p + e, :, QK_NOPE:] = (rope * r * gr2_ref[...]).astype(BF16)


def _q_up(cq, c2, s2, W, tm, transposed):
    T = cq.shape[0]
    row = lambda i: (i, 0)
    const = lambda i: (0, 0)
    if transposed:
        out_spec = pl.BlockSpec((N_HEADS, 1, QK_PAD, tm), lambda i: (0, i, 0, 0))
        out_shape = jax.ShapeDtypeStruct((N_HEADS, T // tm, QK_PAD, tm), BF16)
    else:
        out_spec = pl.BlockSpec((N_HEADS, tm, QK_PAD), lambda i: (0, i, 0))
        out_shape = jax.ShapeDtypeStruct((N_HEADS, T, QK_PAD), BF16)
    return pl.pallas_call(
        functools.partial(_qup_kernel, transposed=transposed),
        grid=(T // tm,),
        in_specs=[
            pl.BlockSpec((tm, Q_RANK), row),
            pl.BlockSpec((Q_RANK, N_PAIRS * 4 * LANE), const),
            pl.BlockSpec((tm, LANE), row),
            pl.BlockSpec((tm, LANE), row),
            pl.BlockSpec((1, LANE), const),
            pl.BlockSpec((1, LANE), const),
        ],
        out_specs=out_spec,
        out_shape=out_shape,
        compiler_params=_params(("arbitrary",), 48),
        name="q_up",
    )(cq, W['w_q'], c2, s2, W['g_q_nope'], W['g_q_rope2'])


V_ROWS = V_HEAD + 16


def _kvup_kernel(ckv_ref, kpe2_ref, w_ref, gn_ref, gr2_ref, k_ref, v_ref):
    ckv = ckv_ref[...]
    kpe2 = kpe2_ref[...]
    lo, hi = _half_masks()
    ss_pe = jnp.sum(kpe2 * kpe2 * lo, axis=-1, keepdims=True)
    tm = ckv.shape[0]
    for p in range(N_PAIRS):
        kv = jnp.dot(ckv, w_ref[p], preferred_element_type=F32)
        for e, msk in enumerate((lo, hi)):
            hd = 2 * p + e
            kn = kv[:, e * LANE:(e + 1) * LANE]
            r = lax.rsqrt((jnp.sum(kn * kn, axis=-1, keepdims=True) + ss_pe) * (1.0 / QK_HEAD) + EPS)
            k_ref[hd, :, :QK_NOPE] = (kn * r * gn_ref[...]).astype(BF16)
            k_ref[hd, :, QK_NOPE:] = (kpe2 * msk * r * gr2_ref[...]).astype(BF16)
            v_ref[hd, 0, :V_HEAD, :] = kv[:, (2 + e) * LANE:(3 + e) * LANE].T.astype(BF16)
            v_ref[hd, 0, V_HEAD:, :] = jnp.ones((V_ROWS - V_HEAD, tm), BF16)


def _kv_up(ckvb, kpe2, W, tm):
    T = ckvb.shape[0]
    row = lambda i: (i, 0)
    const = lambda i: (0, 0)
    return pl.pallas_call(
        _kvup_kernel,
        grid=(T // tm,),
        in_specs=[
            pl.BlockSpec((tm, KV_RANK), row),
            pl.BlockSpec((tm, LANE), row),
            pl.BlockSpec((N_PAIRS, KV_RANK, 4 * LANE), lambda i: (0, 0, 0)),
            pl.BlockSpec((1, LANE), const),
            pl.BlockSpec((1, LANE), const),
        ],
        out_specs=[
            pl.BlockSpec((N_HEADS, tm, QK_PAD), lambda i: (0, i, 0)),
            pl.BlockSpec((N_HEADS, 1, V_ROWS, tm), lambda i: (0, i, 0, 0)),
        ],
        out_shape=[
            jax.ShapeDtypeStruct((N_HEADS, T, QK_PAD), BF16),
            jax.ShapeDtypeStruct((N_HEADS, T // tm, V_ROWS, tm), BF16),
        ],
        compiler_params=_params(("arbitrary",), 48),
        name="kv_up",
    )(ckvb, kpe2, W['w_kv'], W['g_k_nope'], W['g_k_rope2'])


def _online_softmax(s, m_ref, l_ref):
    m_prev = m_ref[...]
    blocks = [s[:, c:c + LANE] for c in range(0, s.shape[1], LANE)]
    m_next = jnp.maximum(m_prev, jnp.max(functools.reduce(jnp.maximum, blocks), axis=-1, keepdims=True))
    blocks = [jnp.exp2(b - m_next) for b in blocks]
    alpha = jnp.exp2(m_prev - m_next)
    l_ref[...] = alpha * l_ref[...] + functools.reduce(jnp.add, blocks)
    m_ref[...] = m_next
    return jnp.concatenate(blocks, axis=-1).astype(BF16), alpha


N_CHAINS = 2
FLASH_UNROLLS = (4, 2, 1)
FLASH_Q_PER_STEP = 4


def _flash_kernel(qt_ref, k_ref, vt_ref, o_ref, s_ref, m_ref, acc_ref, *, tq):
    def scores(qt, j, chain):
        start = pl.multiple_of(j * tq, tq)
        s_ref[chain] = jnp.dot(k_ref[0, pl.ds(start, tq), :], qt, preferred_element_type=F32)

    def consume(j, chain, diagonal=False):
        s = s_ref[chain]
        if diagonal:
            kchunk = lax.broadcasted_iota(jnp.int32, (tq, tq), 0) // CHUNK
            qchunk = lax.broadcasted_iota(jnp.int32, (tq, tq), 1) // CHUNK
            s = jnp.where(kchunk <= qchunk, s, NEG_INF)
        m_prev = m_ref[chain]
        m_next = jnp.maximum(m_prev, jnp.max(s, axis=0, keepdims=True))
        p = jnp.exp2(s - m_next).astype(BF16)
        alpha = jnp.exp2(m_prev - m_next)
        acc_ref[chain] = alpha * acc_ref[chain] + jnp.dot(vt_ref[0, j], p, preferred_element_type=F32)
        m_ref[chain] = m_next

    def q_tile(u, carry):
        n = pl.program_id(1) * FLASH_Q_PER_STEP + u + 1
        qt = qt_ref[0, u]
        m_ref[...] = jnp.full(m_ref.shape, -jnp.inf, F32)
        acc_ref[...] = jnp.zeros(acc_ref.shape, F32)
        scores(qt, 0, 0)

        @pl.when(n >= 2)
        def _():
            scores(qt, 1, 1)

        def pair(jj):
            consume(2 * jj, 0)
            scores(qt, 2 * jj + 2, 0)
            consume(2 * jj + 1, 1)
            scores(qt, 2 * jj + 3, 1)

        def pairs(count):
            def body(jj, c):
                for r in range(count):
                    pair(count * jj + r)
                return c
            return body

        n_pairs = jnp.maximum(n // 2 - 1, 0)
        done = 0
        for count in FLASH_UNROLLS:
            iters = (n_pairs - done) // count
            lax.fori_loop(done // count, done // count + iters, pairs(count), 0)
            done = done + iters * count

        @pl.when(n % 2 == 0)
        def _():
            consume(n - 2, 0)
            consume(n - 1, 1, diagonal=True)

        @pl.when(jnp.logical_and(n % 2 == 1, n >= 3))
        def _():
            consume(n - 3, 0)
            scores(qt, n - 1, 0)
            consume(n - 2, 1)
            consume(n - 1, 0, diagonal=True)

        @pl.when(n == 1)
        def _():
            consume(0, 0, diagonal=True)

        m = jnp.maximum(m_ref[0], m_ref[1])
        acc = jnp.exp2(m_ref[0] - m) * acc_ref[0] + jnp.exp2(m_ref[1] - m) * acc_ref[1]
        out_t = acc[:V_HEAD] / acc[V_HEAD:V_HEAD + 1]
        o_ref[pl.ds(pl.multiple_of(u * tq, tq), tq), :] = out_t.T.astype(BF16)
        return carry

    lax.fori_loop(0, FLASH_Q_PER_STEP, q_tile, 0)


def _flash_prompt(qt, k, vt):
    H, n_q, _, tq = qt.shape
    S = k.shape[1]
    per_step = FLASH_Q_PER_STEP
    return pl.pallas_call(
        functools.partial(_flash_kernel, tq=tq),
        grid=(H, n_q // per_step),
        in_specs=[
            pl.BlockSpec((1, per_step, QK_PAD, tq), lambda h, i: (h, i, 0, 0)),
            pl.BlockSpec((1, S, QK_PAD), lambda h, i: (h, 0, 0)),
            pl.BlockSpec((1, n_q, V_ROWS, tq), lambda h, i: (h, 0, 0, 0)),
        ],
        out_specs=pl.BlockSpec((per_step * tq, V_HEAD), lambda h, i: (i, h)),
        out_shape=jax.ShapeDtypeStruct((S, H * V_HEAD), BF16),
        scratch_shapes=[pltpu.VMEM((N_CHAINS, tq, tq), F32),
                        pltpu.VMEM((N_CHAINS, 1, tq), F32),
                        pltpu.VMEM((N_CHAINS, V_ROWS, tq), F32)],
        compiler_params=_params(("arbitrary", "arbitrary"), 48),
        name="flash_prompt",
    )(qt, k, vt)


Q_LAT = KV_RANK + LANE


def _qabsorb_kernel(q_ref, wkt_ref, gn_ref, gr2_ref, o_ref):
    q = q_ref[0]
    qn = (q[:, :QK_NOPE].astype(F32) * gn_ref[...]).astype(BF16)
    o_ref[0, :, :KV_RANK] = jnp.dot(qn, wkt_ref[...], preferred_element_type=F32).astype(BF16)
    o_ref[0, :, KV_RANK:] = (q[:, QK_NOPE:].astype(F32) * gr2_ref[...]).astype(BF16)


def _q_absorb(q, W):
    H, T, _ = q.shape
    return pl.pallas_call(
        _qabsorb_kernel,
        grid=(H,),
        in_specs=[
            pl.BlockSpec((1, T, QK_PAD), lambda h: (h, 0, 0)),
            pl.BlockSpec((QK_NOPE, KV_RANK), lambda h: (h, 0)),
            pl.BlockSpec((1, LANE), lambda h: (0, 0)),
            pl.BlockSpec((1, LANE), lambda h: (0, 0)),
        ],
        out_specs=pl.BlockSpec((1, T, Q_LAT), lambda h: (h, 0, 0)),
        out_shape=jax.ShapeDtypeStruct((H, T, Q_LAT), BF16),
        compiler_params=_params(("arbitrary",), 32),
        name="q_absorb",
    )(q, W['w_kT'], W['g_k_nope'], W['g_k_rope2'])


def _decode_kernel(q_ref, ckvn_ref, kpetn_ref, ckvc_ref, kpetc_ref, wkt_ref, wv_ref, o_ref,
                   m_ref, l_ref, acc_ref, p_ref, alpha_ref, ckvpad_ref, kpetpad_ref, *, n_tiles, t_new):
    j = pl.program_id(1)
    rows = N_HEADS * t_new
    qa = q_ref[...].reshape(rows, Q_LAT)

    def attend(ckv, kpet2, n_valid):
        n = ckv.shape[0]
        knt = lax.dot_general(wkt_ref[...], ckv, _NT, preferred_element_type=F32)
        t = (lax.dot_general(qa[:, :KV_RANK], ckv, _NT, preferred_element_type=F32)
             + jnp.dot(qa[:, KV_RANK:], kpet2.astype(BF16), preferred_element_type=F32))
        kpet = kpet2[:QK_ROPE]
        ss_pe = jnp.sum(kpet * kpet, axis=0, keepdims=True)
        valid = lax.broadcasted_iota(jnp.int32, (1, n), 1) < n_valid
        for hd in range(N_HEADS):
            kn = knt[hd * QK_NOPE:(hd + 1) * QK_NOPE]
            r = lax.rsqrt((jnp.sum(kn * kn, axis=0, keepdims=True) + ss_pe) * (1.0 / QK_HEAD) + EPS)
            s = t[hd * t_new:(hd + 1) * t_new] * r
            if n_valid < n:
                s = jnp.where(valid, s, NEG_INF)
            p, alpha = _online_softmax(s, m_ref.at[hd], l_ref.at[hd])
            p_ref[hd * t_new:(hd + 1) * t_new, :n] = p
            alpha_ref[hd * t_new:(hd + 1) * t_new, :] = alpha
        pv = jnp.dot(p_ref[:, :n], ckv, preferred_element_type=F32)
        alpha = alpha_ref[...]
        acc_ref[...] = jnp.concatenate([alpha] * (KV_RANK // LANE), axis=1) * acc_ref[...] + pv

    @pl.when(j == 0)
    def _():
        m_ref[...] = jnp.full(m_ref.shape, -jnp.inf, F32)
        l_ref[...] = jnp.zeros(l_ref.shape, F32)
        acc_ref[...] = jnp.zeros(acc_ref.shape, F32)
        ckvpad_ref[...] = jnp.zeros(ckvpad_ref.shape, BF16)
        ckvpad_ref[:t_new, :] = ckvn_ref[...]
        kpetpad_ref[...] = jnp.zeros(kpetpad_ref.shape, F32)
        kpetpad_ref[:, :t_new] = kpetn_ref[0]
        attend(ckvpad_ref[...], kpetpad_ref[...], t_new)

    tk = ckvc_ref.shape[1]
    attend(ckvc_ref[0].astype(BF16), kpetc_ref[0], tk)

    @pl.when(j == n_tiles - 1)
    def _():
        for hd in range(N_HEADS):
            l = jnp.sum(l_ref[hd], axis=-1, keepdims=True)
            lat = (acc_ref[hd * t_new:(hd + 1) * t_new, :] / l).astype(BF16)
            o_ref[:, hd * V_HEAD:(hd + 1) * V_HEAD] = jnp.dot(
                lat, wv_ref[hd], preferred_element_type=F32).astype(BF16)


def _decode_attn(qa, ckvb_new, kpet2_new, cache_ckv, cache_kpet2, W, tk):
    B, P, _ = cache_ckv.shape
    T = ckvb_new.shape[0]
    t_new = T // B
    n_tiles = P // tk
    rows = N_HEADS * t_new
    return pl.pallas_call(
        functools.partial(_decode_kernel, n_tiles=n_tiles, t_new=t_new),
        grid=(B, n_tiles),
        in_specs=[
            pl.BlockSpec((N_HEADS, t_new, Q_LAT), lambda b, j: (0, b, 0)),
            pl.BlockSpec((t_new, KV_RANK), lambda b, j: (b, 0)),
            pl.BlockSpec((1, LANE, t_new), lambda b, j: (b, 0, 0)),
            pl.BlockSpec((1, tk, KV_RANK), lambda b, j: (b, j, 0)),
            pl.BlockSpec((1, LANE, tk), lambda b, j: (b, 0, j)),
            pl.BlockSpec((N_HEADS * QK_NOPE, KV_RANK), lambda b, j: (0, 0)),
            pl.BlockSpec((N_HEADS, KV_RANK, V_HEAD), lambda b, j: (0, 0, 0)),
        ],
        out_specs=pl.BlockSpec((t_new, N_HEADS * V_HEAD), lambda b, j: (b, 0)),
        out_shape=jax.ShapeDtypeStruct((T, N_HEADS * V_HEAD), BF16),
        scratch_shapes=[
            pltpu.VMEM((N_HEADS, t_new, LANE), F32),
            pltpu.VMEM((N_HEADS, t_new, LANE), F32),
            pltpu.VMEM((rows, KV_RANK), F32),
            pltpu.VMEM((rows, tk), BF16),
            pltpu.VMEM((rows, LANE), F32),
            pltpu.VMEM((LANE, KV_RANK), BF16),
            pltpu.VMEM((LANE, LANE), F32),
        ],
        compiler_params=_params(("arbitrary", "arbitrary"), 56),
        name="decode_attn",
    )(qa, ckvb_new, kpet2_new, cache_ckv, cache_kpet2, W['w_kT'], W['w_v'])


CONV_ROWS = 64
CONV_COLS = 256
SUBLANE = 8


def _conv_kernel(h_ref, past_ref, wa_ref, wg_ref, ba_ref, bg_ref, wdw_ref, bdw_ref, gln_ref, bln_ref,
                 wpw_ref, bpw_ref, yb_ref, state_ref, buf_ref, shift_ref, y_ref, *, tm, n_t):
    t = pl.program_id(1)
    hist = CONV_WIDTH - 1
    off = PAST_PAD - hist

    @pl.when(t == 0)
    def _():
        buf_ref[0:PAST_PAD, :] = jnp.zeros((PAST_PAD, CONV_CH), F32)
        buf_ref[off:PAST_PAD, :] = past_ref[0]

    @pl.when(t > 0)
    def _():
        buf_ref[0:PAST_PAD, :] = buf_ref[tm:tm + PAST_PAD, :]

    h = h_ref[0]
    a = jnp.dot(h, wa_ref[...], preferred_element_type=F32) + ba_ref[...]
    g = jnp.dot(h, wg_ref[...], preferred_element_type=F32) + bg_ref[...]
    buf_ref[PAST_PAD:PAST_PAD + tm, :] = a * jax.nn.sigmoid(g)

    span = shift_ref.shape[1]
    for r in range(1, SUBLANE):
        shift_ref[r - 1] = buf_ref[r:r + span, :]
    rows = min(CONV_ROWS, tm)
    for r0 in range(0, tm, rows):
        for c0 in range(0, CONV_CH, CONV_COLS):
            cols = slice(c0, c0 + CONV_COLS)
            acc = jnp.zeros((rows, CONV_COLS), F32)
            for q in range(off, PAST_PAD + 1):
                r, base = q % SUBLANE, r0 + q - q % SUBLANE
                src = buf_ref if r == 0 else shift_ref.at[r - 1]
                acc = acc + wdw_ref[q - off:q - off + 1, cols] * src[base:base + rows, cols]
            y_ref[r0:r0 + rows, cols] = acc

    y = y_ref[...] + bdw_ref[...]
    yc = y - jnp.mean(y, axis=-1, keepdims=True)
    y = yc * lax.rsqrt(jnp.mean(yc * yc, axis=-1, keepdims=True) + EPS) * gln_ref[...] + bln_ref[...]
    y = y * jax.nn.sigmoid(y)
    yb = jnp.dot(y.astype(BF16), wpw_ref[...], preferred_element_type=F32) + bpw_ref[...]
    yb_ref[0] = yb.astype(BF16)

    @pl.when(t == n_t - 1)
    def _():
        state_ref[0] = buf_ref[tm + off:tm + PAST_PAD, :]


def _conv_branch(h3d, past, W, tm):
    B, S, D = h3d.shape
    n_t = S // tm
    hist = CONV_WIDTH - 1
    const = lambda b, t: (0, 0)
    return pl.pallas_call(
        functools.partial(_conv_kernel, tm=tm, n_t=n_t),
        grid=(B, n_t),
        in_specs=[
            pl.BlockSpec((1, tm, D), lambda b, t: (b, t, 0)),
            pl.BlockSpec((1, hist, CONV_CH), lambda b, t: (b, 0, 0)),
            pl.BlockSpec((D, CONV_CH), const),
            pl.BlockSpec((D, CONV_CH), const),
            pl.BlockSpec((1, CONV_CH), const),
            pl.BlockSpec((1, CONV_CH), const),
            pl.BlockSpec((CONV_WIDTH, CONV_CH), const),
            pl.BlockSpec((1, CONV_CH), const),
            pl.BlockSpec((1, CONV_CH), const),
            pl.BlockSpec((1, CONV_CH), const),
            pl.BlockSpec((CONV_CH, D), const),
            pl.BlockSpec((1, D), const),
        ],
        out_specs=[
            pl.BlockSpec((1, tm, D), lambda b, t: (b, t, 0)),
            pl.BlockSpec((1, hist, CONV_CH), lambda b, t: (b, 0, 0)),
        ],
        out_shape=[
            jax.ShapeDtypeStruct((B, S, D), BF16),
            jax.ShapeDtypeStruct((B, hist, CONV_CH), F32),
        ],
        scratch_shapes=[pltpu.VMEM((PAST_PAD + tm, CONV_CH), F32),
                        pltpu.VMEM((SUBLANE - 1, PAST_PAD + tm - SUBLANE, CONV_CH), F32),
                        pltpu.VMEM((tm, CONV_CH), F32)],
        compiler_params=_params(("arbitrary", "arbitrary"), 48),
        name="conv_branch",
    )(h3d, past, W['w_glu_a'], W['w_glu_g'], W['b_glu_a'], W['b_glu_g'], W['w_dw'], W['b_dw'],
      W['g_ln'], W['b_ln'], W['w_pw'], W['b_pw'])


def _mix_kernel(h_ref, attn_ref, yb_ref, wga_ref, wgb_ref, wao_ref, bga_ref, bgb_ref, m_ref):
    h = h_ref[...]
    ga = jax.nn.sigmoid(lax.dot_general(h, wga_ref[...], _NT, preferred_element_type=F32) + bga_ref[...])
    gb = jax.nn.sigmoid(lax.dot_general(h, wgb_ref[...], _NT, preferred_element_type=F32) + bgb_ref[...])
    ya = jnp.dot(attn_ref[...], wao_ref[...], preferred_element_type=F32)
    m_ref[...] = (ga * ya + gb * yb_ref[...].astype(F32)).astype(BF16)


def _mix(h, attn, yb, W, tm, tn):
    T, D = h.shape
    row = lambda i, j: (i, 0)
    col = lambda i, j: (0, j)
    blk = lambda i, j: (i, j)
    return pl.pallas_call(
        _mix_kernel,
        grid=(T // tm, D // tn),
        in_specs=[
            pl.BlockSpec((tm, D), row),
            pl.BlockSpec((tm, D), row),
            pl.BlockSpec((tm, tn), blk),
            pl.BlockSpec((tn, D), lambda i, j: (j, 0)),
            pl.BlockSpec((tn, D), lambda i, j: (j, 0)),
            pl.BlockSpec((D, tn), col),
            pl.BlockSpec((1, tn), col),
            pl.BlockSpec((1, tn), col),
        ],
        out_specs=pl.BlockSpec((tm, tn), blk),
        out_shape=jax.ShapeDtypeStruct((T, D), BF16),
        compiler_params=_params(("arbitrary", "arbitrary"), 48),
        name="gated_mix",
    )(h, attn, yb, W['w_gate_a_t'], W['w_gate_b_t'], W['w_attn_out'], W['b_gate_a'], W['b_gate_b'])


def _outproj_kernel(x_ref, m_ref, w_ref, o_ref):
    o_ref[...] = x_ref[...] + jnp.dot(m_ref[...], w_ref[...], preferred_element_type=F32)


def _out_proj(x2d, m, W, tm):
    T, D = x2d.shape
    return pl.pallas_call(
        _outproj_kernel,
        grid=(T // tm,),
        in_specs=[
            pl.BlockSpec((tm, D), lambda i: (i, 0)),
            pl.BlockSpec((tm, D), lambda i: (i, 0)),
            pl.BlockSpec((D, D), lambda i: (0, 0)),
        ],
        out_specs=pl.BlockSpec((tm, D), lambda i: (i, 0)),
        out_shape=jax.ShapeDtypeStruct((T, D), F32),
        compiler_params=_params(("arbitrary",), 48),
        name="out_proj",
    )(x2d, m, W['w_out'])


def _ffn_kernel(x_ref, g_ref, wg_ref, wu_ref, wd_ref, o_ref, h_ref):
    @pl.when(pl.program_id(1) == 0)
    def _():
        x = x_ref[...]
        h_ref[...] = (x * _rms_scale(x, x.shape[-1]) * g_ref[...]).astype(BF16)
        o_ref[...] = x

    h = h_ref[...]
    gate = jnp.dot(h, wg_ref[...], preferred_element_type=F32)
    up = jnp.dot(h, wu_ref[...], preferred_element_type=F32)
    act = (gate * jax.nn.sigmoid(gate) * up).astype(BF16)
    o_ref[...] += jnp.dot(act, wd_ref[...], preferred_element_type=F32)


def _ffn(x2d, W, tm, tf):
    T, D = x2d.shape
    d_ff = W['w_ffn_gate'].shape[1]
    return pl.pallas_call(
        _ffn_kernel,
        grid=(T // tm, d_ff // tf),
        in_specs=[
            pl.BlockSpec((tm, D), lambda i, j: (i, 0)),
            pl.BlockSpec((1, D), lambda i, j: (0, 0)),
            pl.BlockSpec((D, tf), lambda i, j: (0, j)),
            pl.BlockSpec((D, tf), lambda i, j: (0, j)),
            pl.BlockSpec((tf, D), lambda i, j: (j, 0)),
        ],
        out_specs=pl.BlockSpec((tm, D), lambda i, j: (i, 0)),
        out_shape=jax.ShapeDtypeStruct((T, D), F32),
        scratch_shapes=[pltpu.VMEM((tm, D), BF16)],
        compiler_params=_params(("arbitrary", "arbitrary"), 60),
        name="ffn",
    )(x2d, W['g_ffn'], W['w_ffn_gate'], W['w_ffn_up'], W['w_ffn_down'])


def _rot_half_cols(w):
    half = QK_ROPE // 2
    return jnp.concatenate([-w[..., half:], w[..., :half]], axis=-1)


def _prep_weights(lw):
    (g_mix_norm, w_in, b_glu, b_gate, g_q_a, w_q_up, g_q_norm, g_kv_a, w_kv_up, g_k_norm,
     w_attn_out, w_dw, b_dw, g_conv_ln, b_conv_ln, w_conv_out, b_conv_out, w_out,
     g_ffn_norm, w_ffn_gate, w_ffn_up, w_ffn_down) = lw
    D = w_in.shape[0]
    o_kv = Q_RANK
    o_pe = o_kv + KV_RANK
    o_glu = o_pe + QK_ROPE
    o_gate = o_glu + 2 * CONV_CH
    w_t = w_in.T
    w_pe_t = w_t[o_pe:o_glu]
    half = QK_ROPE // 2
    w_pe_rot_t = jnp.concatenate([-w_pe_t[half:], w_pe_t[:half]], axis=0)
    row = lambda v: v.reshape(1, -1).astype(F32)
    W = {
        'g_mix': row(g_mix_norm),
        'w_small_t': jnp.concatenate([w_t[:o_pe], w_pe_t, w_pe_t, w_pe_rot_t, w_pe_rot_t], axis=0).astype(BF16),
        'g_q_a': row(g_q_a),
        'g_kv_a': row(g_kv_a),
        'w_glu_a': w_t[o_glu:o_glu + CONV_CH].T.astype(BF16),
        'w_glu_g': w_t[o_glu + CONV_CH:o_gate].T.astype(BF16),
        'b_glu_a': row(b_glu[:CONV_CH]),
        'b_glu_g': row(b_glu[CONV_CH:]),
        'w_gate_a_t': w_t[o_gate:o_gate + D].astype(BF16),
        'w_gate_b_t': w_t[o_gate + D:].astype(BF16),
        'b_gate_a': row(b_gate[:D]),
        'b_gate_b': row(b_gate[D:]),
        'w_attn_out': w_attn_out.astype(BF16),
        'w_dw': w_dw.astype(F32),
        'b_dw': row(b_dw),
        'g_ln': row(g_conv_ln),
        'b_ln': row(b_conv_ln),
        'w_pw': w_conv_out.astype(BF16),
        'b_pw': row(b_conv_out),
        'w_out': w_out.astype(BF16),
        'g_ffn': row(g_ffn_norm),
        'w_ffn_gate': w_ffn_gate.astype(BF16),
        'w_ffn_up': w_ffn_up.astype(BF16),
        'w_ffn_down': w_ffn_down.astype(BF16),
    }
    wq = w_q_up.reshape(Q_RANK, N_PAIRS, 2, QK_HEAD)
    wq_nope = wq[..., :QK_NOPE].reshape(Q_RANK, N_PAIRS, 2 * QK_NOPE)
    wq_rope = wq[..., QK_NOPE:]
    W['w_q'] = jnp.concatenate(
        [wq_nope, wq_rope.reshape(Q_RANK, N_PAIRS, 2 * QK_ROPE),
         _rot_half_cols(wq_rope).reshape(Q_RANK, N_PAIRS, 2 * QK_ROPE)], axis=-1
    ).reshape(Q_RANK, N_PAIRS * 4 * LANE).astype(BF16)
    wkv = w_kv_up.reshape(KV_RANK, N_PAIRS, 2, QK_NOPE + V_HEAD)
    W['w_kv'] = jnp.concatenate(
        [wkv[..., :QK_NOPE].reshape(KV_RANK, N_PAIRS, 2 * QK_NOPE),
         wkv[..., QK_NOPE:].reshape(KV_RANK, N_PAIRS, 2 * V_HEAD)], axis=-1
    ).transpose(1, 0, 2).astype(BF16)
    W['w_kT'] = wkv[..., :QK_NOPE].reshape(KV_RANK, N_HEADS * QK_NOPE).T.astype(BF16)
    W['w_v'] = wkv[..., QK_NOPE:].reshape(KV_RANK, N_HEADS, V_HEAD).transpose(1, 0, 2).astype(BF16)
    dup = lambda v: jnp.concatenate([v, v]).reshape(1, LANE).astype(F32)
    W['g_q_nope'] = row(g_q_norm[:QK_NOPE])
    W['g_q_rope2'] = dup(g_q_norm[QK_NOPE:])
    W['g_k_nope'] = row(g_k_norm[:QK_NOPE])
    W['g_k_rope2'] = dup(g_k_norm[QK_NOPE:])
    return W


def _rope_tables(pos):
    inv_freq = 1.0 / (ROPE_THETA ** (jnp.arange(0, QK_ROPE, 2, dtype=F32) / QK_ROPE))
    ang = pos.astype(F32)[:, None] * inv_freq[None, :]
    return jnp.tile(jnp.cos(ang), (1, 4)), jnp.tile(jnp.sin(ang), (1, 4))


ROW_TILE = 512
WIDE_ROW_TILE = 1024
COL_TILE = 512
FFN_TILE = 512
CONV_ROW_TILE = 256
DECODE_KEY_TILE = 1024


def _layer(x, pos, past_ckv, past_kpe, past_conv, W):
    B, S, D = x.shape
    T = B * S
    x2d = x.reshape(T, D)
    c2, s2 = _rope_tables(pos)
    if B > 1:
        c2, s2 = jnp.tile(c2, (B, 1)), jnp.tile(s2, (B, 1))
    tm = min(T, ROW_TILE)
    tm_wide = min(T, WIDE_ROW_TILE)
    h, cq, ckv, ckvb, kpe, kpe2 = _inproj(x2d, c2, s2, W, tm)
    if past_ckv is None:
        qt = _q_up(cq, c2, s2, W, tm, transposed=True)
        k, vt = _kv_up(ckvb, kpe2, W, tm)
        attn = _flash_prompt(qt, k, vt)
        past_conv = jnp.zeros((B, CONV_WIDTH - 1, CONV_CH), x.dtype)
    else:
        q = _q_up(cq, c2, s2, W, tm, transposed=False)
        cache_kpet = jnp.swapaxes(past_kpe, 1, 2)
        cache_kpet2 = jnp.concatenate([cache_kpet, cache_kpet], axis=1)
        kpet2_new = jnp.swapaxes(kpe2.reshape(B, S, LANE), 1, 2)
        attn = _decode_attn(_q_absorb(q, W), ckvb, kpet2_new, past_ckv, cache_kpet2, W, DECODE_KEY_TILE)
    yb, conv_state = _conv_branch(h.reshape(B, S, D), past_conv, W, min(S, CONV_ROW_TILE))
    m = _mix(h, attn, yb.reshape(T, D), W, tm_wide, COL_TILE)
    x1 = _out_proj(x2d, m, W, tm)
    y = _ffn(x1, W, tm_wide, FFN_TILE)
    return (y.reshape(B, S, D), ckv.reshape(B, S, KV_RANK), kpe.reshape(B, S, QK_ROPE), conv_state)


def kernel(x_prompt, x_sample, cache_ckv, cache_kpe, state_conv, g_mix_norm, w_in, b_glu, b_gate, g_q_a,
           w_q_up, g_q_norm, g_kv_a, w_kv_up, g_k_norm, w_attn_out, w_dw, b_dw, g_conv_ln, b_conv_ln,
           w_conv_out, b_conv_out, w_out, g_ffn_norm, w_ffn_gate, w_ffn_up, w_ffn_down):
    weights = (g_mix_norm, w_in, b_glu, b_gate, g_q_a, w_q_up, g_q_norm, g_kv_a, w_kv_up, g_k_norm,
               w_attn_out, w_dw, b_dw, g_conv_ln, b_conv_ln, w_conv_out, b_conv_out, w_out,
               g_ffn_norm, w_ffn_gate, w_ffn_up, w_ffn_down)
    depth = w_in.shape[0]
    pos_prompt = jnp.arange(x_prompt.shape[1])
    pos_sample = cache_ckv.shape[2] + jnp.arange(x_sample.shape[1])
    y_prompt, y_sample = x_prompt, x_sample
    outs = [[] for _ in range(6)]
    for l in range(depth):
        W = _prep_weights(tuple(w[l] for w in weights))
        y_prompt, ckv, kpe, conv = _layer(y_prompt, pos_prompt, None, None, None, W)
        outs[0].append(ckv); outs[1].append(kpe); outs[2].append(conv)
        y_sample, ckv, kpe, conv = _layer(y_sample, pos_sample, cache_ckv[l], cache_kpe[l], state_conv[l], W)
        outs[3].append(ckv); outs[4].append(kpe); outs[5].append(conv)
    return (y_prompt, y_sample) + tuple(jnp.stack(o) for o in outs)
```

```python
import functools
import math

import jax
import jax.numpy as jnp
from jax import lax
from jax.experimental import pallas as pl
from jax.experimental.pallas import tpu as pltpu

F32 = jnp.float32
BF16 = jnp.bfloat16

CHUNK = 64
N_HEADS = 16
Q_RANK = 512
KV_RANK = 512
QK_NOPE = 128
QK_ROPE = 64
QK_HEAD = QK_NOPE + QK_ROPE
V_HEAD = 128
CONV_CH = 1024
CONV_WIDTH = 31
ROPE_THETA = 10000.0
EPS = 1e-6
NEG_INF = -1e30
SCALE = QK_HEAD ** -0.5
LOG2E = math.log2(math.e)

LANE = 128
QK_PAD = 2 * LANE
N_PAIRS = N_HEADS // 2
PAST_PAD = 32
MIB = 1024 * 1024
_NT = (((1,), (1,)), ((), ()))


def _params(semantics, vmem_mib):
    return pltpu.CompilerParams(dimension_semantics=semantics, vmem_limit_bytes=vmem_mib * MIB)


def _rms_scale(v, n):
    return lax.rsqrt(jnp.sum(v * v, axis=-1, keepdims=True) * (1.0 / n) + EPS)


def _half_masks():
    lane = lax.broadcasted_iota(jnp.int32, (1, LANE), 1)
    lo = (lane < QK_ROPE).astype(F32)
    return lo, 1.0 - lo


def _inproj_kernel(x_ref, g_ref, w_ref, gq_ref, gkv_ref, c2_ref, s2_ref,
                   h_ref, cq_ref, ckv_ref, ckvb_ref, kpe_ref, kpe2_ref):
    x = x_ref[...]
    h = (x * _rms_scale(x, x.shape[-1]) * g_ref[...]).astype(BF16)
    h_ref[...] = h
    z = lax.dot_general(h, w_ref[...], _NT, preferred_element_type=F32)
    cq = z[:, :Q_RANK]
    cq_ref[...] = (cq * _rms_scale(cq, Q_RANK) * gq_ref[...]).astype(BF16)
    ckv = z[:, Q_RANK:Q_RANK + KV_RANK]
    ckv = ckv * _rms_scale(ckv, KV_RANK) * gkv_ref[...]
    ckv_ref[...] = ckv
    ckvb_ref[...] = ckv.astype(BF16)
    base = Q_RANK + KV_RANK
    kpe2 = z[:, base:base + LANE] * c2_ref[...] + z[:, base + LANE:base + 2 * LANE] * s2_ref[...]
    kpe2_ref[...] = kpe2
    kpe_ref[...] = kpe2[:, :QK_ROPE]


def _inproj(x2d, c2, s2, W, tm):
    T, D = x2d.shape
    n_in = W['w_small_t'].shape[0]
    row = lambda i: (i, 0)
    const = lambda i: (0, 0)
    return pl.pallas_call(
        _inproj_kernel,
        grid=(T // tm,),
        in_specs=[
            pl.BlockSpec((tm, D), row),
            pl.BlockSpec((1, D), const),
            pl.BlockSpec((n_in, D), const),
            pl.BlockSpec((1, Q_RANK), const),
            pl.BlockSpec((1, KV_RANK), const),
            pl.BlockSpec((tm, LANE), row),
            pl.BlockSpec((tm, LANE), row),
        ],
        out_specs=[
            pl.BlockSpec((tm, D), row),
            pl.BlockSpec((tm, Q_RANK), row),
            pl.BlockSpec((tm, KV_RANK), row),
            pl.BlockSpec((tm, KV_RANK), row),
            pl.BlockSpec((tm, QK_ROPE), row),
            pl.BlockSpec((tm, LANE), row),
        ],
        out_shape=[
            jax.ShapeDtypeStruct((T, D), BF16),
            jax.ShapeDtypeStruct((T, Q_RANK), BF16),
            jax.ShapeDtypeStruct((T, KV_RANK), F32),
            jax.ShapeDtypeStruct((T, KV_RANK), BF16),
            jax.ShapeDtypeStruct((T, QK_ROPE), F32),
            jax.ShapeDtypeStruct((T, LANE), F32),
        ],
        compiler_params=_params(("arbitrary",), 48),
        name="inproj",
    )(x2d, W['g_mix'], W['w_small_t'], W['g_q_a'], W['g_kv_a'], c2, s2)


def _qup_kernel(cq_ref, w_ref, c2_ref, s2_ref, gn_ref, gr2_ref, q_ref, *, transposed):
    cq = cq_ref[...]
    masks = _half_masks()
    for p in range(N_PAIRS):
        z = jnp.dot(cq, w_ref[:, p * 4 * LANE:(p + 1) * 4 * LANE], preferred_element_type=F32)
        rope2 = z[:, 2 * LANE:3 * LANE] * c2_ref[...] + z[:, 3 * LANE:] * s2_ref[...]
        for e, msk in enumerate(masks):
            nope = z[:, e * LANE:(e + 1) * LANE]
            rope = rope2 * msk
            ss = jnp.sum(nope * nope, axis=-1, keepdims=True) + jnp.sum(rope * rope, axis=-1, keepdims=True)
            r = lax.rsqrt(ss * (1.0 / QK_HEAD) + EPS) * (SCALE * LOG2E)
            if transposed:
                q_ref[2 * p + e, 0, :QK_NOPE, :] = (nope * r * gn_ref[...]).T.astype(BF16)
                q_ref[2 * p + e, 0, QK_NOPE:, :] = (rope * r * gr2_ref[...]).T.astype(BF16)
            else:
                q_ref[2 * p + e, :, :QK_NOPE] = (nope * r * gn_ref[...]).astype(BF16)
                q_ref[2 * p + e, :, QK_NOPE:] = (rope * r * gr2_ref[...]).astype(BF16)


def _q_up(cq, c2, s2, W, tm, transposed):
    T = cq.shape[0]
    row = lambda i: (i, 0)
    const = lambda i: (0, 0)
    if transposed:
        out_spec = pl.BlockSpec((N_HEADS, 1, QK_PAD, tm), lambda i: (0, i, 0, 0))
        out_shape = jax.ShapeDtypeStruct((N_HEADS, T // tm, QK_PAD, tm), BF16)
    else:
        out_spec = pl.BlockSpec((N_HEADS, tm, QK_PAD), lambda i: (0, i, 0))
        out_shape = jax.ShapeDtypeStruct((N_HEADS, T, QK_PAD), BF16)
    return pl.pallas_call(
        functools.partial(_qup_kernel, transposed=transposed),
        grid=(T // tm,),
        in_specs=[
            pl.BlockSpec((tm, Q_RANK), row),
            pl.BlockSpec((Q_RANK, N_PAIRS * 4 * LANE), const),
            pl.BlockSpec((tm, LANE), row),
            pl.BlockSpec((tm, LANE), row),
            pl.BlockSpec((1, LANE), const),
            pl.BlockSpec((1, LANE), const),
        ],
        out_specs=out_spec,
        out_shape=out_shape,
        compiler_params=_params(("arbitrary",), 48),
        name="q_up",
    )(cq, W['w_q'], c2, s2, W['g_q_nope'], W['g_q_rope2'])


V_ROWS = V_HEAD + 16


def _kvup_kernel(ckv_ref, kpe2_ref, w_ref, gn_ref, gr2_ref, k_ref, v_ref):
    ckv = ckv_ref[...]
    kpe2 = kpe2_ref[...]
    lo, hi = _half_masks()
    ss_pe = jnp.sum(kpe2 * kpe2 * lo, axis=-1, keepdims=True)
    tm = ckv.shape[0]
    for p in range(N_PAIRS):
        kv = jnp.dot(ckv, w_ref[p], preferred_element_type=F32)
        for e, msk in enumerate((lo, hi)):
            hd = 2 * p + e
            kn = kv[:, e * LANE:(e + 1) * LANE]
            r = lax.rsqrt((jnp.sum(kn * kn, axis=-1, keepdims=True) + ss_pe) * (1.0 / QK_HEAD) + EPS)
            k_ref[hd, :, :QK_NOPE] = (kn * r * gn_ref[...]).astype(BF16)
            k_ref[hd, :, QK_NOPE:] = (kpe2 * msk * r * gr2_ref[...]).astype(BF16)
            v_ref[hd, 0, :V_HEAD, :] = kv[:, (2 + e) * LANE:(3 + e) * LANE].T.astype(BF16)
            v_ref[hd, 0, V_HEAD:, :] = jnp.ones((V_ROWS - V_HEAD, tm), BF16)


def _kv_up(ckvb, kpe2, W, tm):
    T = ckvb.shape[0]
    row = lambda i: (i, 0)
    const = lambda i: (0, 0)
    return pl.pallas_call(
        _kvup_kernel,
        grid=(T // tm,),
        in_specs=[
            pl.BlockSpec((tm, KV_RANK), row),
            pl.BlockSpec((tm, LANE), row),
            pl.BlockSpec((N_PAIRS, KV_RANK, 4 * LANE), lambda i: (0, 0, 0)),
            pl.BlockSpec((1, LANE), const),
            pl.BlockSpec((1, LANE), const),
        ],
        out_specs=[
            pl.BlockSpec((N_HEADS, tm, QK_PAD), lambda i: (0, i, 0)),
            pl.BlockSpec((N_HEADS, 1, V_ROWS, tm), lambda i: (0, i, 0, 0)),
        ],
        out_shape=[
            jax.ShapeDtypeStruct((N_HEADS, T, QK_PAD), BF16),
            jax.ShapeDtypeStruct((N_HEADS, T // tm, V_ROWS, tm), BF16),
        ],
        compiler_params=_params(("arbitrary",), 48),
        name="kv_up",
    )(ckvb, kpe2, W['w_kv'], W['g_k_nope'], W['g_k_rope2'])


def _online_softmax(s, m_ref, l_ref):
    m_prev = m_ref[...]
    blocks = [s[:, c:c + LANE] for c in range(0, s.shape[1], LANE)]
    m_next = jnp.maximum(m_prev, jnp.max(functools.reduce(jnp.maximum, blocks), axis=-1, keepdims=True))
    blocks = [jnp.exp2(b - m_next) for b in blocks]
    alpha = jnp.exp2(m_prev - m_next)
    l_ref[...] = alpha * l_ref[...] + functools.reduce(jnp.add, blocks)
    m_ref[...] = m_next
    return jnp.concatenate(blocks, axis=-1).astype(BF16), alpha


N_CHAINS = 2
FLASH_UNROLLS = (4, 2, 1)
FLASH_Q_PER_STEP = 8


def _flash_kernel(qt_ref, k_ref, vt_ref, o_ref, s_ref, m_ref, acc_ref, *, tq):
    def scores(qt, j, chain):
        start = pl.multiple_of(j * tq, tq)
        s_ref[chain] = jnp.dot(k_ref[0, pl.ds(start, tq), :], qt, preferred_element_type=F32)

    def consume(j, chain, diagonal=False):
        s = s_ref[chain]
        if diagonal:
            kchunk = lax.broadcasted_iota(jnp.int32, (tq, tq), 0) // CHUNK
            qchunk = lax.broadcasted_iota(jnp.int32, (tq, tq), 1) // CHUNK
            s = jnp.where(kchunk <= qchunk, s, NEG_INF)
        m_prev = m_ref[chain]
        m_next = jnp.maximum(m_prev, jnp.max(s, axis=0, keepdims=True))
        p = jnp.exp2(s - m_next).astype(BF16)
        alpha = jnp.exp2(m_prev - m_next)
        acc_ref[chain] = alpha * acc_ref[chain] + jnp.dot(vt_ref[0, j], p, preferred_element_type=F32)
        m_ref[chain] = m_next

    first = pl.program_id(1) * FLASH_Q_PER_STEP

    def q_tile(u, carry):
        n = first + u + 1
        qt = qt_ref[0, u]
        qt_next = qt_ref[0, jnp.minimum(u + 1, FLASH_Q_PER_STEP - 1)]
        m_ref[...] = jnp.full(m_ref.shape, -jnp.inf, F32)
        acc_ref[...] = jnp.zeros(acc_ref.shape, F32)

        def next_tile_scores():
            scores(qt_next, 0, 0)
            scores(qt_next, 1, 1)

        def pair(jj):
            consume(2 * jj, 0)
            scores(qt, 2 * jj + 2, 0)
            consume(2 * jj + 1, 1)
            scores(qt, 2 * jj + 3, 1)

        def pairs(count):
            def body(jj, c):
                for r in range(count):
                    pair(count * jj + r)
                return c
            return body

        n_pairs = jnp.maximum(n // 2 - 1, 0)
        done = 0
        for count in FLASH_UNROLLS:
            iters = (n_pairs - done) // count
            lax.fori_loop(done // count, done // count + iters, pairs(count), 0)
            done = done + iters * count

        @pl.when(n % 2 == 0)
        def _():
            consume(n - 2, 0)
            consume(n - 1, 1, diagonal=True)
            next_tile_scores()

        @pl.when(jnp.logical_and(n % 2 == 1, n >= 3))
        def _():
            consume(n - 3, 0)
            scores(qt, n - 1, 0)
            consume(n - 2, 1)
            consume(n - 1, 0, diagonal=True)
            next_tile_scores()

        @pl.when(n == 1)
        def _():
            consume(0, 0, diagonal=True)
            next_tile_scores()

        m = jnp.maximum(m_ref[0], m_ref[1])
        acc = jnp.exp2(m_ref[0] - m) * acc_ref[0] + jnp.exp2(m_ref[1] - m) * acc_ref[1]
        out_t = acc[:V_HEAD] / acc[V_HEAD:V_HEAD + 1]
        o_ref[pl.ds(pl.multiple_of(u * tq, tq), tq), :] = out_t.T.astype(BF16)
        return carry

    scores(qt_ref[0, 0], 0, 0)

    @pl.when(first >= 1)
    def _():
        scores(qt_ref[0, 0], 1, 1)

    lax.fori_loop(0, FLASH_Q_PER_STEP, q_tile, 0)


def _flash_prompt(qt, k, vt):
    H, n_q, _, tq = qt.shape
    S = k.shape[1]
    per_step = FLASH_Q_PER_STEP
    return pl.pallas_call(
        functools.partial(_flash_kernel, tq=tq),
        grid=(H, n_q // per_step),
        in_specs=[
            pl.BlockSpec((1, per_step, QK_PAD, tq), lambda h, i: (h, i, 0, 0)),
            pl.BlockSpec((1, S, QK_PAD), lambda h, i: (h, 0, 0)),
            pl.BlockSpec((1, n_q, V_ROWS, tq), lambda h, i: (h, 0, 0, 0)),
        ],
        out_specs=pl.BlockSpec((per_step * tq, V_HEAD), lambda h, i: (i, h)),
        out_shape=jax.ShapeDtypeStruct((S, H * V_HEAD), BF16),
        scratch_shapes=[pltpu.VMEM((N_CHAINS, tq, tq), F32),
                        pltpu.VMEM((N_CHAINS, 1, tq), F32),
                        pltpu.VMEM((N_CHAINS, V_ROWS, tq), F32)],
        compiler_params=_params(("arbitrary", "arbitrary"), 48),
        name="flash_prompt",
    )(qt, k, vt)


Q_LAT = KV_RANK + LANE


def _qabsorb_kernel(q_ref, wkt_ref, gn_ref, gr2_ref, o_ref):
    q = q_ref[0]
    qn = (q[:, :QK_NOPE].astype(F32) * gn_ref[...]).astype(BF16)
    o_ref[0, :, :KV_RANK] = jnp.dot(qn, wkt_ref[...], preferred_element_type=F32).astype(BF16)
    o_ref[0, :, KV_RANK:] = (q[:, QK_NOPE:].astype(F32) * gr2_ref[...]).astype(BF16)


def _q_absorb(q, W):
    H, T, _ = q.shape
    return pl.pallas_call(
        _qabsorb_kernel,
        grid=(H,),
        in_specs=[
            pl.BlockSpec((1, T, QK_PAD), lambda h: (h, 0, 0)),
            pl.BlockSpec((QK_NOPE, KV_RANK), lambda h: (h, 0)),
            pl.BlockSpec((1, LANE), lambda h: (0, 0)),
            pl.BlockSpec((1, LANE), lambda h: (0, 0)),
        ],
        out_specs=pl.BlockSpec((1, T, Q_LAT), lambda h: (h, 0, 0)),
        out_shape=jax.ShapeDtypeStruct((H, T, Q_LAT), BF16),
        compiler_params=_params(("arbitrary",), 32),
        name="q_absorb",
    )(q, W['w_kT'], W['g_k_nope'], W['g_k_rope2'])


def _decode_kernel(q_ref, ckvn_ref, kpetn_ref, ckvc_ref, kpetc_ref, wkt_ref, wv_ref, o_ref,
                   m_ref, l_ref, acc_ref, p_ref, alpha_ref, ckvpad_ref, kpetpad_ref, *, n_tiles, t_new):
    j = pl.program_id(1)
    rows = N_HEADS * t_new
    qa = q_ref[...].reshape(rows, Q_LAT)

    def attend(ckv, kpet2, n_valid):
        n = ckv.shape[0]
        knt = lax.dot_general(wkt_ref[...], ckv, _NT, preferred_element_type=F32)
        t = (lax.dot_general(qa[:, :KV_RANK], ckv, _NT, preferred_element_type=F32)
             + jnp.dot(qa[:, KV_RANK:], kpet2.astype(BF16), preferred_element_type=F32))
        kpet = kpet2[:QK_ROPE]
        ss_pe = jnp.sum(kpet * kpet, axis=0, keepdims=True)
        valid = lax.broadcasted_iota(jnp.int32, (1, n), 1) < n_valid
        for hd in range(N_HEADS):
            kn = knt[hd * QK_NOPE:(hd + 1) * QK_NOPE]
            r = lax.rsqrt((jnp.sum(kn * kn, axis=0, keepdims=True) + ss_pe) * (1.0 / QK_HEAD) + EPS)
            s = t[hd * t_new:(hd + 1) * t_new] * r
            if n_valid < n:
                s = jnp.where(valid, s, NEG_INF)
            p, alpha = _online_softmax(s, m_ref.at[hd], l_ref.at[hd])
            p_ref[hd * t_new:(hd + 1) * t_new, :n] = p
            alpha_ref[hd * t_new:(hd + 1) * t_new, :] = alpha
        pv = jnp.dot(p_ref[:, :n], ckv, preferred_element_type=F32)
        alpha = alpha_ref[...]
        acc_ref[...] = jnp.concatenate([alpha] * (KV_RANK // LANE), axis=1) * acc_ref[...] + pv

    @pl.when(j == 0)
    def _():
        m_ref[...] = jnp.full(m_ref.shape, -jnp.inf, F32)
        l_ref[...] = jnp.zeros(l_ref.shape, F32)
        acc_ref[...] = jnp.zeros(acc_ref.shape, F32)
        ckvpad_ref[...] = jnp.zeros(ckvpad_ref.shape, BF16)
        ckvpad_ref[:t_new, :] = ckvn_ref[...]
        kpetpad_ref[...] = jnp.zeros(kpetpad_ref.shape, F32)
        kpetpad_ref[:, :t_new] = kpetn_ref[0]
        attend(ckvpad_ref[...], kpetpad_ref[...], t_new)

    tk = ckvc_ref.shape[1]
    attend(ckvc_ref[0].astype(BF16), kpetc_ref[0], tk)

    @pl.when(j == n_tiles - 1)
    def _():
        for hd in range(N_HEADS):
            l = jnp.sum(l_ref[hd], axis=-1, keepdims=True)
            lat = (acc_ref[hd * t_new:(hd + 1) * t_new, :] / l).astype(BF16)
            o_ref[:, hd * V_HEAD:(hd + 1) * V_HEAD] = jnp.dot(
                lat, wv_ref[hd], preferred_element_type=F32).astype(BF16)


def _decode_attn(qa, ckvb_new, kpet2_new, cache_ckv, cache_kpet2, W, tk):
    B, P, _ = cache_ckv.shape
    T = ckvb_new.shape[0]
    t_new = T // B
    n_tiles = P // tk
    rows = N_HEADS * t_new
    return pl.pallas_call(
        functools.partial(_decode_kernel, n_tiles=n_tiles, t_new=t_new),
        grid=(B, n_tiles),
        in_specs=[
            pl.BlockSpec((N_HEADS, t_new, Q_LAT), lambda b, j: (0, b, 0)),
            pl.BlockSpec((t_new, KV_RANK), lambda b, j: (b, 0)),
            pl.BlockSpec((1, LANE, t_new), lambda b, j: (b, 0, 0)),
            pl.BlockSpec((1, tk, KV_RANK), lambda b, j: (b, j, 0)),
            pl.BlockSpec((1, LANE, tk), lambda b, j: (b, 0, j)),
            pl.BlockSpec((N_HEADS * QK_NOPE, KV_RANK), lambda b, j: (0, 0)),
            pl.BlockSpec((N_HEADS, KV_RANK, V_HEAD), lambda b, j: (0, 0, 0)),
        ],
        out_specs=pl.BlockSpec((t_new, N_HEADS * V_HEAD), lambda b, j: (b, 0)),
        out_shape=jax.ShapeDtypeStruct((T, N_HEADS * V_HEAD), BF16),
        scratch_shapes=[
            pltpu.VMEM((N_HEADS, t_new, LANE), F32),
            pltpu.VMEM((N_HEADS, t_new, LANE), F32),
            pltpu.VMEM((rows, KV_RANK), F32),
            pltpu.VMEM((rows, tk), BF16),
            pltpu.VMEM((rows, LANE), F32),
            pltpu.VMEM((LANE, KV_RANK), BF16),
            pltpu.VMEM((LANE, LANE), F32),
        ],
        compiler_params=_params(("arbitrary", "arbitrary"), 56),
        name="decode_attn",
    )(qa, ckvb_new, kpet2_new, cache_ckv, cache_kpet2, W['w_kT'], W['w_v'])


CONV_ROWS = 64
CONV_COLS = 256
SUBLANE = 8


def _conv_kernel(h_ref, past_ref, wa_ref, wg_ref, ba_ref, bg_ref, wdw_ref, bdw_ref, gln_ref, bln_ref,
                 wpw_ref, bpw_ref, yb_ref, state_ref, buf_ref, shift_ref, y_ref, *, tm, n_t):
    t = pl.program_id(1)
    hist = CONV_WIDTH - 1
    off = PAST_PAD - hist

    @pl.when(t == 0)
    def _():
        buf_ref[0:PAST_PAD, :] = jnp.zeros((PAST_PAD, CONV_CH), F32)
        buf_ref[off:PAST_PAD, :] = past_ref[0]

    @pl.when(t > 0)
    def _():
        buf_ref[0:PAST_PAD, :] = buf_ref[tm:tm + PAST_PAD, :]

    h = h_ref[0]
    a = jnp.dot(h, wa_ref[...], preferred_element_type=F32) + ba_ref[...]
    g = jnp.dot(h, wg_ref[...], preferred_element_type=F32) + bg_ref[...]
    buf_ref[PAST_PAD:PAST_PAD + tm, :] = a * jax.nn.sigmoid(g)

    span = shift_ref.shape[1]
    for r in range(1, SUBLANE):
        shift_ref[r - 1] = buf_ref[r:r + span, :]
    rows = min(CONV_ROWS, tm)
    for r0 in range(0, tm, rows):
        for c0 in range(0, CONV_CH, CONV_COLS):
            cols = slice(c0, c0 + CONV_COLS)
            acc = jnp.zeros((rows, CONV_COLS), F32)
            for q in range(off, PAST_PAD + 1):
                r, base = q % SUBLANE, r0 + q - q % SUBLANE
                src = buf_ref if r == 0 else shift_ref.at[r - 1]
                acc = acc + wdw_ref[q - off:q - off + 1, cols] * src[base:base + rows, cols]
            y_ref[r0:r0 + rows, cols] = acc

    y = y_ref[...] + bdw_ref[...]
    yc = y - jnp.mean(y, axis=-1, keepdims=True)
    y = yc * lax.rsqrt(jnp.mean(yc * yc, axis=-1, keepdims=True) + EPS) * gln_ref[...] + bln_ref[...]
    y = y * jax.nn.sigmoid(y)
    yb = jnp.dot(y.astype(BF16), wpw_ref[...], preferred_element_type=F32) + bpw_ref[...]
    yb_ref[0] = yb.astype(BF16)

    @pl.when(t == n_t - 1)
    def _():
        state_ref[0] = buf_ref[tm + off:tm + PAST_PAD, :]


def _conv_branch(h3d, past, W, tm):
    B, S, D = h3d.shape
    n_t = S // tm
    hist = CONV_WIDTH - 1
    const = lambda b, t: (0, 0)
    return pl.pallas_call(
        functools.partial(_conv_kernel, tm=tm, n_t=n_t),
        grid=(B, n_t),
        in_specs=[
            pl.BlockSpec((1, tm, D), lambda b, t: (b, t, 0)),
            pl.BlockSpec((1, hist, CONV_CH), lambda b, t: (b, 0, 0)),
            pl.BlockSpec((D, CONV_CH), const),
            pl.BlockSpec((D, CONV_CH), const),
            pl.BlockSpec((1, CONV_CH), const),
            pl.BlockSpec((1, CONV_CH), const),
            pl.BlockSpec((CONV_WIDTH, CONV_CH), const),
            pl.BlockSpec((1, CONV_CH), const),
            pl.BlockSpec((1, CONV_CH), const),
            pl.BlockSpec((1, CONV_CH), const),
            pl.BlockSpec((CONV_CH, D), const),
            pl.BlockSpec((1, D), const),
        ],
        out_specs=[
            pl.BlockSpec((1, tm, D), lambda b, t: (b, t, 0)),
            pl.BlockSpec((1, hist, CONV_CH), lambda b, t: (b, 0, 0)),
        ],
        out_shape=[
            jax.ShapeDtypeStruct((B, S, D), BF16),
            jax.ShapeDtypeStruct((B, hist, CONV_CH), F32),
        ],
        scratch_shapes=[pltpu.VMEM((PAST_PAD + tm, CONV_CH), F32),
                        pltpu.VMEM((SUBLANE - 1, PAST_PAD + tm - SUBLANE, CONV_CH), F32),
                        pltpu.VMEM((tm, CONV_CH), F32)],
        compiler_params=_params(("arbitrary", "arbitrary"), 48),
        name="conv_branch",
    )(h3d, past, W['w_glu_a'], W['w_glu_g'], W['b_glu_a'], W['b_glu_g'], W['w_dw'], W['b_dw'],
      W['g_ln'], W['b_ln'], W['w_pw'], W['b_pw'])


def _mix_kernel(h_ref, attn_ref, yb_ref, wga_ref, wgb_ref, wao_ref, bga_ref, bgb_ref, m_ref):
    h = h_ref[...]
    ga = jax.nn.sigmoid(lax.dot_general(h, wga_ref[...], _NT, preferred_element_type=F32) + bga_ref[...])
    gb = jax.nn.sigmoid(lax.dot_general(h, wgb_ref[...], _NT, preferred_element_type=F32) + bgb_ref[...])
    ya = jnp.dot(attn_ref[...], wao_ref[...], preferred_element_type=F32)
    m_ref[...] = (ga * ya + gb * yb_ref[...].astype(F32)).astype(BF16)


def _mix(h, attn, yb, W, tm, tn):
    T, D = h.shape
    row = lambda i, j: (i, 0)
    col = lambda i, j: (0, j)
    blk = lambda i, j: (i, j)
    return pl.pallas_call(
        _mix_kernel,
        grid=(T // tm, D // tn),
        in_specs=[
            pl.BlockSpec((tm, D), row),
            pl.BlockSpec((tm, D), row),
            pl.BlockSpec((tm, tn), blk),
            pl.BlockSpec((tn, D), lambda i, j: (j, 0)),
            pl.BlockSpec((tn, D), lambda i, j: (j, 0)),
            pl.BlockSpec((D, tn), col),
            pl.BlockSpec((1, tn), col),
            pl.BlockSpec((1, tn), col),
        ],
        out_specs=pl.BlockSpec((tm, tn), blk),
        out_shape=jax.ShapeDtypeStruct((T, D), BF16),
        compiler_params=_params(("arbitrary", "arbitrary"), 48),
        name="gated_mix",
    )(h, attn, yb, W['w_gate_a_t'], W['w_gate_b_t'], W['w_attn_out'], W['b_gate_a'], W['b_gate_b'])


def _outproj_kernel(x_ref, m_ref, w_ref, o_ref):
    o_ref[...] = x_ref[...] + jnp.dot(m_ref[...], w_ref[...], preferred_element_type=F32)


def _out_proj(x2d, m, W, tm):
    T, D = x2d.shape
    return pl.pallas_call(
        _outproj_kernel,
        grid=(T // tm,),
        in_specs=[
            pl.BlockSpec((tm, D), lambda i: (i, 0)),
            pl.BlockSpec((tm, D), lambda i: (i, 0)),
            pl.BlockSpec((D, D), lambda i: (0, 0)),
        ],
        out_specs=pl.BlockSpec((tm, D), lambda i: (i, 0)),
        out_shape=jax.ShapeDtypeStruct((T, D), F32),
        compiler_params=_params(("arbitrary",), 48),
        name="out_proj",
    )(x2d, m, W['w_out'])


def _ffn_kernel(x_ref, g_ref, wg_ref, wu_ref, wd_ref, o_ref, h_ref):
    @pl.when(pl.program_id(1) == 0)
    def _():
        x = x_ref[...]
        h_ref[...] = (x * _rms_scale(x, x.shape[-1]) * g_ref[...]).astype(BF16)
        o_ref[...] = x

    h = h_ref[...]
    gate = jnp.dot(h, wg_ref[...], preferred_element_type=F32)
    up = jnp.dot(h, wu_ref[...], preferred_element_type=F32)
    act = (gate * jax.nn.sigmoid(gate) * up).astype(BF16)
    o_ref[...] += jnp.dot(act, wd_ref[...], preferred_element_type=F32)


def _ffn(x2d, W, tm, tf):
    T, D = x2d.shape
    d_ff = W['w_ffn_gate'].shape[1]
    return pl.pallas_call(
        _ffn_kernel,
        grid=(T // tm, d_ff // tf),
        in_specs=[
            pl.BlockSpec((tm, D), lambda i, j: (i, 0)),
            pl.BlockSpec((1, D), lambda i, j: (0, 0)),
            pl.BlockSpec((D, tf), lambda i, j: (0, j)),
            pl.BlockSpec((D, tf), lambda i, j: (0, j)),
            pl.BlockSpec((tf, D), lambda i, j: (j, 0)),
        ],
        out_specs=pl.BlockSpec((tm, D), lambda i, j: (i, 0)),
        out_shape=jax.ShapeDtypeStruct((T, D), F32),
        scratch_shapes=[pltpu.VMEM((tm, D), BF16)],
        compiler_params=_params(("arbitrary", "arbitrary"), 60),
        name="ffn",
    )(x2d, W['g_ffn'], W['w_ffn_gate'], W['w_ffn_up'], W['w_ffn_down'])


def _rot_half_cols(w):
    half = QK_ROPE // 2
    return jnp.concatenate([-w[..., half:], w[..., :half]], axis=-1)


def _prep_weights(lw):
    (g_mix_norm, w_in, b_glu, b_gate, g_q_a, w_q_up, g_q_norm, g_kv_a, w_kv_up, g_k_norm,
     w_attn_out, w_dw, b_dw, g_conv_ln, b_conv_ln, w_conv_out, b_conv_out, w_out,
     g_ffn_norm, w_ffn_gate, w_ffn_up, w_ffn_down) = lw
    D = w_in.shape[0]
    o_kv = Q_RANK
    o_pe = o_kv + KV_RANK
    o_glu = o_pe + QK_ROPE
    o_gate = o_glu + 2 * CONV_CH
    w_t = w_in.T
    w_pe_t = w_t[o_pe:o_glu]
    half = QK_ROPE // 2
    w_pe_rot_t = jnp.concatenate([-w_pe_t[half:], w_pe_t[:half]], axis=0)
    row = lambda v: v.reshape(1, -1).astype(F32)
    W = {
        'g_mix': row(g_mix_norm),
        'w_small_t': jnp.concatenate([w_t[:o_pe], w_pe_t, w_pe_t, w_pe_rot_t, w_pe_rot_t], axis=0).astype(BF16),
        'g_q_a': row(g_q_a),
        'g_kv_a': row(g_kv_a),
        'w_glu_a': w_t[o_glu:o_glu + CONV_CH].T.astype(BF16),
        'w_glu_g': w_t[o_glu + CONV_CH:o_gate].T.astype(BF16),
        'b_glu_a': row(b_glu[:CONV_CH]),
        'b_glu_g': row(b_glu[CONV_CH:]),
        'w_gate_a_t': w_t[o_gate:o_gate + D].astype(BF16),
        'w_gate_b_t': w_t[o_gate + D:].astype(BF16),
        'b_gate_a': row(b_gate[:D]),
        'b_gate_b': row(b_gate[D:]),
        'w_attn_out': w_attn_out.astype(BF16),
        'w_dw': w_dw.astype(F32),
        'b_dw': row(b_dw),
        'g_ln': row(g_conv_ln),
        'b_ln': row(b_conv_ln),
        'w_pw': w_conv_out.astype(BF16),
        'b_pw': row(b_conv_out),
        'w_out': w_out.astype(BF16),
        'g_ffn': row(g_ffn_norm),
        'w_ffn_gate': w_ffn_gate.astype(BF16),
        'w_ffn_up': w_ffn_up.astype(BF16),
        'w_ffn_down': w_ffn_down.astype(BF16),
    }
    wq = w_q_up.reshape(Q_RANK, N_PAIRS, 2, QK_HEAD)
    wq_nope = wq[..., :QK_NOPE].reshape(Q_RANK, N_PAIRS, 2 * QK_NOPE)
    wq_rope = wq[..., QK_NOPE:]
    W['w_q'] = jnp.concatenate(
        [wq_nope, wq_rope.reshape(Q_RANK, N_PAIRS, 2 * QK_ROPE),
         _rot_half_cols(wq_rope).reshape(Q_RANK, N_PAIRS, 2 * QK_ROPE)], axis=-1
    ).reshape(Q_RANK, N_PAIRS * 4 * LANE).astype(BF16)
    wkv = w_kv_up.reshape(KV_RANK, N_PAIRS, 2, QK_NOPE + V_HEAD)
    W['w_kv'] = jnp.concatenate(
        [wkv[..., :QK_NOPE].reshape(KV_RANK, N_PAIRS, 2 * QK_NOPE),
         wkv[..., QK_NOPE:].reshape(KV_RANK, N_PAIRS, 2 * V_HEAD)], axis=-1
    ).transpose(1, 0, 2).astype(BF16)
    W['w_kT'] = wkv[..., :QK_NOPE].reshape(KV_RANK, N_HEADS * QK_NOPE).T.astype(BF16)
    W['w_v'] = wkv[..., QK_NOPE:].reshape(KV_RANK, N_HEADS, V_HEAD).transpose(1, 0, 2).astype(BF16)
    dup = lambda v: jnp.concatenate([v, v]).reshape(1, LANE).astype(F32)
    W['g_q_nope'] = row(g_q_norm[:QK_NOPE])
    W['g_q_rope2'] = dup(g_q_norm[QK_NOPE:])
    W['g_k_nope'] = row(g_k_norm[:QK_NOPE])
    W['g_k_rope2'] = dup(g_k_norm[QK_NOPE:])
    return W


def _rope_tables(pos):
    inv_freq = 1.0 / (ROPE_THETA ** (jnp.arange(0, QK_ROPE, 2, dtype=F32) / QK_ROPE))
    ang = pos.astype(F32)[:, None] * inv_freq[None, :]
    return jnp.tile(jnp.cos(ang), (1, 4)), jnp.tile(jnp.sin(ang), (1, 4))


ROW_TILE = 512
WIDE_ROW_TILE = 1024
COL_TILE = 512
FFN_TILE = 512
CONV_ROW_TILE = 256
DECODE_KEY_TILE = 1024


def _layer(x, pos, past_ckv, past_kpe, past_conv, W):
    B, S, D = x.shape
    T = B * S
    x2d = x.reshape(T, D)
    c2, s2 = _rope_tables(pos)
    if B > 1:
        c2, s2 = jnp.tile(c2, (B, 1)), jnp.tile(s2, (B, 1))
    tm = min(T, ROW_TILE)
    tm_wide = min(T, WIDE_ROW_TILE)
    h, cq, ckv, ckvb, kpe, kpe2 = _inproj(x2d, c2, s2, W, tm)
    if past_ckv is None:
        qt = _q_up(cq, c2, s2, W, tm, transposed=True)
        k, vt = _kv_up(ckvb, kpe2, W, tm)
        attn = _flash_prompt(qt, k, vt)
        past_conv = jnp.zeros((B, CONV_WIDTH - 1, CONV_CH), x.dtype)
    else:
        q = _q_up(cq, c2, s2, W, tm, transposed=False)
        cache_kpet = jnp.swapaxes(past_kpe, 1, 2)
        cache_kpet2 = jnp.concatenate([cache_kpet, cache_kpet], axis=1)
        kpet2_new = jnp.swapaxes(kpe2.reshape(B, S, LANE), 1, 2)
        attn = _decode_attn(_q_absorb(q, W), ckvb, kpet2_new, past_ckv, cache_kpet2, W, DECODE_KEY_TILE)
    yb, conv_state = _conv_branch(h.reshape(B, S, D), past_conv, W, min(S, CONV_ROW_TILE))
    m = _mix(h, attn, yb.reshape(T, D), W, tm_wide, COL_TILE)
    x1 = _out_proj(x2d, m, W, tm)
    y = _ffn(x1, W, tm_wide, FFN_TILE)
    return (y.reshape(B, S, D), ckv.reshape(B, S, KV_RANK), kpe.reshape(B, S, QK_ROPE), conv_state)


def kernel(x_prompt, x_sample, cache_ckv, cache_kpe, state_conv, g_mix_norm, w_in, b_glu, b_gate, g_q_a,
           w_q_up, g_q_norm, g_kv_a, w_kv_up, g_k_norm, w_attn_out, w_dw, b_dw, g_conv_ln, b_conv_ln,
           w_conv_out, b_conv_out, w_out, g_ffn_norm, w_ffn_gate, w_ffn_up, w_ffn_down):
    weights = (g_mix_norm, w_in, b_glu, b_gate, g_q_a, w_q_up, g_q_norm, g_kv_a, w_kv_up, g_k_norm,
               w_attn_out, w_dw, b_dw, g_conv_ln, b_conv_ln, w_conv_out, b_conv_out, w_out,
               g_ffn_norm, w_ffn_gate, w_ffn_up, w_ffn_down)
    depth = w_in.shape[0]
    pos_prompt = jnp.arange(x_prompt.shape[1])
    pos_sample = cache_ckv.shape[2] + jnp.arange(x_sample.shape[1])
    y_prompt, y_sample = x_prompt, x_sample
    outs = [[] for _ in range(6)]
    for l in range(depth):
        W = _prep_weights(tuple(w[l] for w in weights))
        y_prompt, ckv, kpe, conv = _layer(y_prompt, pos_prompt, None, None, None, W)
        outs[0].append(ckv); outs[1].append(kpe); outs[2].append(conv)
        y_sample, ckv, kpe, conv = _layer(y_sample, pos_sample, cache_ckv[l], cache_kpe[l], state_conv[l], W)
        outs[3].append(ckv); outs[4].append(kpe); outs[5].append(conv)
    return (y_prompt, y_sample) + tuple(jnp.stack(o) for o in outs)
```

```python
import functools
import math

import jax
import jax.numpy as jnp
from jax import lax
from jax.experimental import pallas as pl
from jax.experimental.pallas import tpu as pltpu

F32 = jnp.float32
BF16 = jnp.bfloat16

CHUNK = 64
N_HEADS = 16
Q_RANK = 512
KV_RANK = 512
QK_NOPE = 128
QK_ROPE = 64
QK_HEAD = QK_NOPE + QK_ROPE
V_HEAD = 128
CONV_CH = 1024
CONV_WIDTH = 31
ROPE_THETA = 10000.0
EPS = 1e-6
NEG_INF = -1e30
SCALE = QK_HEAD ** -0.5
LOG2E = math.log2(math.e)

LANE = 128
QK_PAD = 2 * LANE
N_PAIRS = N_HEADS // 2
PAST_PAD = 32
MIB = 1024 * 1024
_NT = (((1,), (1,)), ((), ()))


def _params(semantics, vmem_mib):
    return pltpu.CompilerParams(dimension_semantics=semantics, vmem_limit_bytes=vmem_mib * MIB)


def _rms_scale(v, n):
    return lax.rsqrt(jnp.sum(v * v, axis=-1, keepdims=True) * (1.0 / n) + EPS)


def _half_masks():
    lane = lax.broadcasted_iota(jnp.int32, (1, LANE), 1)
    lo = (lane < QK_ROPE).astype(F32)
    return lo, 1.0 - lo


def _inproj_kernel(x_ref, g_ref, w_ref, gq_ref, gkv_ref, c2_ref, s2_ref,
                   h_ref, cq_ref, ckv_ref, ckvb_ref, kpe_ref, kpe2_ref):
    x = x_ref[...]
    h = (x * _rms_scale(x, x.shape[-1]) * g_ref[...]).astype(BF16)
    h_ref[...] = h
    z = lax.dot_general(h, w_ref[...], _NT, preferred_element_type=F32)
    cq = z[:, :Q_RANK]
    cq_ref[...] = (cq * _rms_scale(cq, Q_RANK) * gq_ref[...]).astype(BF16)
    ckv = z[:, Q_RANK:Q_RANK + KV_RANK]
    ckv = ckv * _rms_scale(ckv, KV_RANK) * gkv_ref[...]
    ckv_ref[...] = ckv
    ckvb_ref[...] = ckv.astype(BF16)
    base = Q_RANK + KV_RANK
    kpe2 = z[:, base:base + LANE] * c2_ref[...] + z[:, base + LANE:base + 2 * LANE] * s2_ref[...]
    kpe2_ref[...] = kpe2
    kpe_ref[...] = kpe2[:, :QK_ROPE]


def _inproj(x2d, c2, s2, W, tm):
    T, D = x2d.shape
    n_in = W['w_small_t'].shape[0]
    row = lambda i: (i, 0)
    const = lambda i: (0, 0)
    return pl.pallas_call(
        _inproj_kernel,
        grid=(T // tm,),
        in_specs=[
            pl.BlockSpec((tm, D), row),
            pl.BlockSpec((1, D), const),
            pl.BlockSpec((n_in, D), const),
            pl.BlockSpec((1, Q_RANK), const),
            pl.BlockSpec((1, KV_RANK), const),
            pl.BlockSpec((tm, LANE), row),
            pl.BlockSpec((tm, LANE), row),
        ],
        out_specs=[
            pl.BlockSpec((tm, D), row),
            pl.BlockSpec((tm, Q_RANK), row),
            pl.BlockSpec((tm, KV_RANK), row),
            pl.BlockSpec((tm, KV_RANK), row),
            pl.BlockSpec((tm, QK_ROPE), row),
            pl.BlockSpec((tm, LANE), row),
        ],
        out_shape=[
            jax.ShapeDtypeStruct((T, D), BF16),
            jax.ShapeDtypeStruct((T, Q_RANK), BF16),
            jax.ShapeDtypeStruct((T, KV_RANK), F32),
            jax.ShapeDtypeStruct((T, KV_RANK), BF16),
            jax.ShapeDtypeStruct((T, QK_ROPE), F32),
            jax.ShapeDtypeStruct((T, LANE), F32),
        ],
        compiler_params=_params(("arbitrary",), 48),
        name="inproj",
    )(x2d, W['g_mix'], W['w_small_t'], W['g_q_a'], W['g_kv_a'], c2, s2)


def _qup_kernel(cq_ref, w_ref, c2_ref, s2_ref, gn_ref, gr2_ref, q_ref, *, transposed):
    cq = cq_ref[...]
    masks = _half_masks()
    for p in range(N_PAIRS):
        z = jnp.dot(cq, w_ref[:, p * 4 * LANE:(p + 1) * 4 * LANE], preferred_element_type=F32)
        rope2 = z[:, 2 * LANE:3 * LANE] * c2_ref[...] + z[:, 3 * LANE:] * s2_ref[...]
        for e, msk in enumerate(masks):
            nope = z[:, e * LANE:(e + 1) * LANE]
            rope = rope2 * msk
            ss = jnp.sum(nope * nope, axis=-1, keepdims=True) + jnp.sum(rope * rope, axis=-1, keepdims=True)
            r = lax.rsqrt(ss * (1.0 / QK_HEAD) + EPS) * (SCALE * LOG2E)
            if transposed:
                q_ref[2 * p + e, 0, :QK_NOPE, :] = (nope * r * gn_ref[...]).T.astype(BF16)
                q_ref[2 * p + e, 0, QK_NOPE:, :] = (rope * r * gr2_ref[...]).T.astype(BF16)
            else:
                q_ref[2 * p + e, :, :QK_NOPE] = (nope * r * gn_ref[...]).astype(BF16)
                q_ref[2 * p + e, :, QK_NOPE:] = (rope * r * gr2_ref[...]).astype(BF16)


def _q_up(cq, c2, s2, W, tm, transposed):
    T = cq.shape[0]
    row = lambda i: (i, 0)
    const = lambda i: (0, 0)
    if transposed:
        out_spec = pl.BlockSpec((N_HEADS, 1, QK_PAD, tm), lambda i: (0, i, 0, 0))
        out_shape = jax.ShapeDtypeStruct((N_HEADS, T // tm, QK_PAD, tm), BF16)
    else:
        out_spec = pl.BlockSpec((N_HEADS, tm, QK_PAD), lambda i: (0, i, 0))
        out_shape = jax.ShapeDtypeStruct((N_HEADS, T, QK_PAD), BF16)
    return pl.pallas_call(
        functools.partial(_qup_kernel, transposed=transposed),
        grid=(T // tm,),
        in_specs=[
            pl.BlockSpec((tm, Q_RANK), row),
            pl.BlockSpec((Q_RANK, N_PAIRS * 4 * LANE), const),
            pl.BlockSpec((tm, LANE), row),
            pl.BlockSpec((tm, LANE), row),
            pl.BlockSpec((1, LANE), const),
            pl.BlockSpec((1, LANE), const),
        ],
        out_specs=out_spec,
        out_shape=out_shape,
        compiler_params=_params(("arbitrary",), 48),
        name="q_up",
    )(cq, W['w_q'], c2, s2, W['g_q_nope'], W['g_q_rope2'])


V_ROWS = V_HEAD + 16


def _kvup_kernel(ckv_ref, kpe2_ref, w_ref, gn_ref, gr2_ref, k_ref, v_ref):
    ckv = ckv_ref[...]
    kpe2 = kpe2_ref[...]
    lo, hi = _half_masks()
    ss_pe = jnp.sum(kpe2 * kpe2 * lo, axis=-1, keepdims=True)
    tm = ckv.shape[0]
    for p in range(N_PAIRS):
        kv = jnp.dot(ckv, w_ref[p], preferred_element_type=F32)
        for e, msk in enumerate((lo, hi)):
            hd = 2 * p + e
            kn = kv[:, e * LANE:(e + 1) * LANE]
            r = lax.rsqrt((jnp.sum(kn * kn, axis=-1, keepdims=True) + ss_pe) * (1.0 / QK_HEAD) + EPS)
            k_ref[hd, :, :QK_NOPE] = (kn * r * gn_ref[...]).astype(BF16)
            k_ref[hd, :, QK_NOPE:] = (kpe2 * msk * r * gr2_ref[...]).astype(BF16)
            v_ref[hd, 0, :V_HEAD, :] = kv[:, (2 + e) * LANE:(3 + e) * LANE].T.astype(BF16)
            v_ref[hd, 0, V_HEAD:, :] = jnp.ones((V_ROWS - V_HEAD, tm), BF16)


def _kv_up(ckvb, kpe2, W, tm):
    T = ckvb.shape[0]
    row = lambda i: (i, 0)
    const = lambda i: (0, 0)
    return pl.pallas_call(
        _kvup_kernel,
        grid=(T // tm,),
        in_specs=[
            pl.BlockSpec((tm, KV_RANK), row),
            pl.BlockSpec((tm, LANE), row),
            pl.BlockSpec((N_PAIRS, KV_RANK, 4 * LANE), lambda i: (0, 0, 0)),
            pl.BlockSpec((1, LANE), const),
            pl.BlockSpec((1, LANE), const),
        ],
        out_specs=[
            pl.BlockSpec((N_HEADS, tm, QK_PAD), lambda i: (0, i, 0)),
            pl.BlockSpec((N_HEADS, 1, V_ROWS, tm), lambda i: (0, i, 0, 0)),
        ],
        out_shape=[
            jax.ShapeDtypeStruct((N_HEADS, T, QK_PAD), BF16),
            jax.ShapeDtypeStruct((N_HEADS, T // tm, V_ROWS, tm), BF16),
        ],
        compiler_params=_params(("arbitrary",), 48),
        name="kv_up",
    )(ckvb, kpe2, W['w_kv'], W['g_k_nope'], W['g_k_rope2'])


def _online_softmax(s, m_ref, l_ref):
    m_prev = m_ref[...]
    blocks = [s[:, c:c + LANE] for c in range(0, s.shape[1], LANE)]
    m_next = jnp.maximum(m_prev, jnp.max(functools.reduce(jnp.maximum, blocks), axis=-1, keepdims=True))
    blocks = [jnp.exp2(b - m_next) for b in blocks]
    alpha = jnp.exp2(m_prev - m_next)
    l_ref[...] = alpha * l_ref[...] + functools.reduce(jnp.add, blocks)
    m_ref[...] = m_next
    return jnp.concatenate(blocks, axis=-1).astype(BF16), alpha


N_CHAINS = 2
FLASH_UNROLLS = (8, 4, 2, 1)
FLASH_Q_PER_STEP = 8


def _flash_kernel(qt_ref, k_ref, vt_ref, o_ref, s_ref, m_ref, acc_ref, *, tq):
    def scores(qt, j, chain):
        start = pl.multiple_of(j * tq, tq)
        s_ref[chain] = jnp.dot(k_ref[0, pl.ds(start, tq), :], qt, preferred_element_type=F32)

    def consume(j, chain, diagonal=False):
        s = s_ref[chain]
        if diagonal:
            kchunk = lax.broadcasted_iota(jnp.int32, (tq, tq), 0) // CHUNK
            qchunk = lax.broadcasted_iota(jnp.int32, (tq, tq), 1) // CHUNK
            s = jnp.where(kchunk <= qchunk, s, NEG_INF)
        m_prev = m_ref[chain]
        m_next = jnp.maximum(m_prev, jnp.max(s, axis=0, keepdims=True))
        p = jnp.exp2(s - m_next).astype(BF16)
        alpha = jnp.exp2(m_prev - m_next)
        acc_ref[chain] = alpha * acc_ref[chain] + jnp.dot(vt_ref[0, j], p, preferred_element_type=F32)
        m_ref[chain] = m_next

    first = pl.program_id(1) * FLASH_Q_PER_STEP

    def q_tile(u, carry):
        n = first + u + 1
        qt = qt_ref[0, u]
        qt_next = qt_ref[0, jnp.minimum(u + 1, FLASH_Q_PER_STEP - 1)]
        m_ref[...] = jnp.full(m_ref.shape, -jnp.inf, F32)
        acc_ref[...] = jnp.zeros(acc_ref.shape, F32)

        def next_tile_scores():
            scores(qt_next, 0, 0)
            scores(qt_next, 1, 1)

        def pair(jj):
            consume(2 * jj, 0)
            scores(qt, 2 * jj + 2, 0)
            consume(2 * jj + 1, 1)
            scores(qt, 2 * jj + 3, 1)

        def pairs(count):
            def body(jj, c):
                for r in range(count):
                    pair(count * jj + r)
                return c
            return body

        n_pairs = jnp.maximum(n // 2 - 1, 0)
        done = 0
        for count in FLASH_UNROLLS:
            iters = (n_pairs - done) // count
            lax.fori_loop(done // count, done // count + iters, pairs(count), 0)
            done = done + iters * count

        @pl.when(n % 2 == 0)
        def _():
            consume(n - 2, 0)
            consume(n - 1, 1, diagonal=True)
            next_tile_scores()

        @pl.when(jnp.logical_and(n % 2 == 1, n >= 3))
        def _():
            consume(n - 3, 0)
            scores(qt, n - 1, 0)
            consume(n - 2, 1)
            consume(n - 1, 0, diagonal=True)
            next_tile_scores()

        @pl.when(n == 1)
        def _():
            consume(0, 0, diagonal=True)
            next_tile_scores()

        m = jnp.maximum(m_ref[0], m_ref[1])
        acc = jnp.exp2(m_ref[0] - m) * acc_ref[0] + jnp.exp2(m_ref[1] - m) * acc_ref[1]
        out_t = acc[:V_HEAD] / acc[V_HEAD:V_HEAD + 1]
        o_ref[pl.ds(pl.multiple_of(u * tq, tq), tq), :] = out_t.T.astype(BF16)
        return carry

    scores(qt_ref[0, 0], 0, 0)

    @pl.when(first >= 1)
    def _():
        scores(qt_ref[0, 0], 1, 1)

    lax.fori_loop(0, FLASH_Q_PER_STEP, q_tile, 0)


def _flash_prompt(qt, k, vt):
    H, n_q, _, tq = qt.shape
    S = k.shape[1]
    per_step = FLASH_Q_PER_STEP
    return pl.pallas_call(
        functools.partial(_flash_kernel, tq=tq),
        grid=(H, n_q // per_step),
        in_specs=[
            pl.BlockSpec((1, per_step, QK_PAD, tq), lambda h, i: (h, i, 0, 0)),
            pl.BlockSpec((1, S, QK_PAD), lambda h, i: (h, 0, 0)),
            pl.BlockSpec((1, n_q, V_ROWS, tq), lambda h, i: (h, 0, 0, 0)),
        ],
        out_specs=pl.BlockSpec((per_step * tq, V_HEAD), lambda h, i: (i, h)),
        out_shape=jax.ShapeDtypeStruct((S, H * V_HEAD), BF16),
        scratch_shapes=[pltpu.VMEM((N_CHAINS, tq, tq), F32),
                        pltpu.VMEM((N_CHAINS, 1, tq), F32),
                        pltpu.VMEM((N_CHAINS, V_ROWS, tq), F32)],
        compiler_params=_params(("arbitrary", "arbitrary"), 48),
        name="flash_prompt",
    )(qt, k, vt)


Q_LAT = KV_RANK + LANE


def _qabsorb_kernel(q_ref, wkt_ref, gn_ref, gr2_ref, o_ref):
    q = q_ref[0]
    qn = (q[:, :QK_NOPE].astype(F32) * gn_ref[...]).astype(BF16)
    o_ref[0, :, :KV_RANK] = jnp.dot(qn, wkt_ref[...], preferred_element_type=F32).astype(BF16)
    o_ref[0, :, KV_RANK:] = (q[:, QK_NOPE:].astype(F32) * gr2_ref[...]).astype(BF16)


def _q_absorb(q, W):
    H, T, _ = q.shape
    return pl.pallas_call(
        _qabsorb_kernel,
        grid=(H,),
        in_specs=[
            pl.BlockSpec((1, T, QK_PAD), lambda h: (h, 0, 0)),
            pl.BlockSpec((QK_NOPE, KV_RANK), lambda h: (h, 0)),
            pl.BlockSpec((1, LANE), lambda h: (0, 0)),
            pl.BlockSpec((1, LANE), lambda h: (0, 0)),
        ],
        out_specs=pl.BlockSpec((1, T, Q_LAT), lambda h: (h, 0, 0)),
        out_shape=jax.ShapeDtypeStruct((H, T, Q_LAT), BF16),
        compiler_params=_params(("arbitrary",), 32),
        name="q_absorb",
    )(q, W['w_kT'], W['g_k_nope'], W['g_k_rope2'])


def _decode_kernel(q_ref, ckvn_ref, kpetn_ref, ckvc_ref, kpetc_ref, wkt_ref, wv_ref, o_ref,
                   m_ref, l_ref, acc_ref, p_ref, alpha_ref, ckvpad_ref, kpetpad_ref, *, n_tiles, t_new):
    j = pl.program_id(1)
    rows = N_HEADS * t_new
    qa = q_ref[...].reshape(rows, Q_LAT)

    def attend(ckv, kpet2, n_valid):
        n = ckv.shape[0]
        knt = lax.dot_general(wkt_ref[...], ckv, _NT, preferred_element_type=F32)
        t = (lax.dot_general(qa[:, :KV_RANK], ckv, _NT, preferred_element_type=F32)
             + jnp.dot(qa[:, KV_RANK:], kpet2.astype(BF16), preferred_element_type=F32))
        kpet = kpet2[:QK_ROPE]
        ss_pe = jnp.sum(kpet * kpet, axis=0, keepdims=True)
        valid = lax.broadcasted_iota(jnp.int32, (1, n), 1) < n_valid
        for hd in range(N_HEADS):
            kn = knt[hd * QK_NOPE:(hd + 1) * QK_NOPE]
            r = lax.rsqrt((jnp.sum(kn * kn, axis=0, keepdims=True) + ss_pe) * (1.0 / QK_HEAD) + EPS)
            s = t[hd * t_new:(hd + 1) * t_new] * r
            if n_valid < n:
                s = jnp.where(valid, s, NEG_INF)
            p, alpha = _online_softmax(s, m_ref.at[hd], l_ref.at[hd])
            p_ref[hd * t_new:(hd + 1) * t_new, :n] = p
            alpha_ref[hd * t_new:(hd + 1) * t_new, :] = alpha
        pv = jnp.dot(p_ref[:, :n], ckv, preferred_element_type=F32)
        alpha = alpha_ref[...]
        acc_ref[...] = jnp.concatenate([alpha] * (KV_RANK // LANE), axis=1) * acc_ref[...] + pv

    @pl.when(j == 0)
    def _():
        m_ref[...] = jnp.full(m_ref.shape, -jnp.inf, F32)
        l_ref[...] = jnp.zeros(l_ref.shape, F32)
        acc_ref[...] = jnp.zeros(acc_ref.shape, F32)
        ckvpad_ref[...] = jnp.zeros(ckvpad_ref.shape, BF16)
        ckvpad_ref[:t_new, :] = ckvn_ref[...]
        kpetpad_ref[...] = jnp.zeros(kpetpad_ref.shape, F32)
        kpetpad_ref[:, :t_new] = kpetn_ref[0]
        attend(ckvpad_ref[...], kpetpad_ref[...], t_new)

    tk = ckvc_ref.shape[1]
    attend(ckvc_ref[0].astype(BF16), kpetc_ref[0], tk)

    @pl.when(j == n_tiles - 1)
    def _():
        for hd in range(N_HEADS):
            l = jnp.sum(l_ref[hd], axis=-1, keepdims=True)
            lat = (acc_ref[hd * t_new:(hd + 1) * t_new, :] / l).astype(BF16)
            o_ref[:, hd * V_HEAD:(hd + 1) * V_HEAD] = jnp.dot(
                lat, wv_ref[hd], preferred_element_type=F32).astype(BF16)


def _decode_attn(qa, ckvb_new, kpet2_new, cache_ckv, cache_kpet2, W, tk):
    B, P, _ = cache_ckv.shape
    T = ckvb_new.shape[0]
    t_new = T // B
    n_tiles = P // tk
    rows = N_HEADS * t_new
    return pl.pallas_call(
        functools.partial(_decode_kernel, n_tiles=n_tiles, t_new=t_new),
        grid=(B, n_tiles),
        in_specs=[
            pl.BlockSpec((N_HEADS, t_new, Q_LAT), lambda b, j: (0, b, 0)),
            pl.BlockSpec((t_new, KV_RANK), lambda b, j: (b, 0)),
            pl.BlockSpec((1, LANE, t_new), lambda b, j: (b, 0, 0)),
            pl.BlockSpec((1, tk, KV_RANK), lambda b, j: (b, j, 0)),
            pl.BlockSpec((1, LANE, tk), lambda b, j: (b, 0, j)),
            pl.BlockSpec((N_HEADS * QK_NOPE, KV_RANK), lambda b, j: (0, 0)),
            pl.BlockSpec((N_HEADS, KV_RANK, V_HEAD), lambda b, j: (0, 0, 0)),
        ],
        out_specs=pl.BlockSpec((t_new, N_HEADS * V_HEAD), lambda b, j: (b, 0)),
        out_shape=jax.ShapeDtypeStruct((T, N_HEADS * V_HEAD), BF16),
        scratch_shapes=[
            pltpu.VMEM((N_HEADS, t_new, LANE), F32),
            pltpu.VMEM((N_HEADS, t_new, LANE), F32),
            pltpu.VMEM((rows, KV_RANK), F32),
            pltpu.VMEM((rows, tk), BF16),
            pltpu.VMEM((rows, LANE), F32),
            pltpu.VMEM((LANE, KV_RANK), BF16),
            pltpu.VMEM((LANE, LANE), F32),
        ],
        compiler_params=_params(("arbitrary", "arbitrary"), 56),
        name="decode_attn",
    )(qa, ckvb_new, kpet2_new, cache_ckv, cache_kpet2, W['w_kT'], W['w_v'])


CONV_ROWS = 64
CONV_COLS = 256
SUBLANE = 8


def _conv_kernel(h_ref, past_ref, wa_ref, wg_ref, ba_ref, bg_ref, wdw_ref, bdw_ref, gln_ref, bln_ref,
                 wpw_ref, bpw_ref, yb_ref, state_ref, buf_ref, shift_ref, y_ref, *, tm, n_t):
    t = pl.program_id(1)
    hist = CONV_WIDTH - 1
    off = PAST_PAD - hist

    @pl.when(t == 0)
    def _():
        buf_ref[0:PAST_PAD, :] = jnp.zeros((PAST_PAD, CONV_CH), F32)
        buf_ref[off:PAST_PAD, :] = past_ref[0]

    @pl.when(t > 0)
    def _():
        buf_ref[0:PAST_PAD, :] = buf_ref[tm:tm + PAST_PAD, :]

    h = h_ref[0]
    a = jnp.dot(h, wa_ref[...], preferred_element_type=F32) + ba_ref[...]
    g = jnp.dot(h, wg_ref[...], preferred_element_type=F32) + bg_ref[...]
    buf_ref[PAST_PAD:PAST_PAD + tm, :] = a * jax.nn.sigmoid(g)

    span = shift_ref.shape[1]
    for r in range(1, SUBLANE):
        shift_ref[r - 1] = buf_ref[r:r + span, :]
    rows = min(CONV_ROWS, tm)
    for r0 in range(0, tm, rows):
        for c0 in range(0, CONV_CH, CONV_COLS):
            cols = slice(c0, c0 + CONV_COLS)
            acc = jnp.zeros((rows, CONV_COLS), F32)
            for q in range(off, PAST_PAD + 1):
                r, base = q % SUBLANE, r0 + q - q % SUBLANE
                src = buf_ref if r == 0 else shift_ref.at[r - 1]
                w_tap = pltpu.repeat(wdw_ref[q - off, :, cols], rows // SUBLANE, axis=0)
                acc = acc + w_tap * src[base:base + rows, cols]
            y_ref[r0:r0 + rows, cols] = acc

    y = y_ref[...] + bdw_ref[...]
    yc = y - jnp.mean(y, axis=-1, keepdims=True)
    y = yc * lax.rsqrt(jnp.mean(yc * yc, axis=-1, keepdims=True) + EPS) * gln_ref[...] + bln_ref[...]
    y = y * jax.nn.sigmoid(y)
    yb = jnp.dot(y.astype(BF16), wpw_ref[...], preferred_element_type=F32) + bpw_ref[...]
    yb_ref[0] = yb.astype(BF16)

    @pl.when(t == n_t - 1)
    def _():
        state_ref[0] = buf_ref[tm + off:tm + PAST_PAD, :]


def _conv_branch(h3d, past, W, tm):
    B, S, D = h3d.shape
    n_t = S // tm
    hist = CONV_WIDTH - 1
    const = lambda b, t: (0, 0)
    return pl.pallas_call(
        functools.partial(_conv_kernel, tm=tm, n_t=n_t),
        grid=(B, n_t),
        in_specs=[
            pl.BlockSpec((1, tm, D), lambda b, t: (b, t, 0)),
            pl.BlockSpec((1, hist, CONV_CH), lambda b, t: (b, 0, 0)),
            pl.BlockSpec((D, CONV_CH), const),
            pl.BlockSpec((D, CONV_CH), const),
            pl.BlockSpec((1, CONV_CH), const),
            pl.BlockSpec((1, CONV_CH), const),
            pl.BlockSpec((CONV_WIDTH, SUBLANE, CONV_CH), lambda b, t: (0, 0, 0)),
            pl.BlockSpec((1, CONV_CH), const),
            pl.BlockSpec((1, CONV_CH), const),
            pl.BlockSpec((1, CONV_CH), const),
            pl.BlockSpec((CONV_CH, D), const),
            pl.BlockSpec((1, D), const),
        ],
        out_specs=[
            pl.BlockSpec((1, tm, D), lambda b, t: (b, t, 0)),
            pl.BlockSpec((1, hist, CONV_CH), lambda b, t: (b, 0, 0)),
        ],
        out_shape=[
            jax.ShapeDtypeStruct((B, S, D), BF16),
            jax.ShapeDtypeStruct((B, hist, CONV_CH), F32),
        ],
        scratch_shapes=[pltpu.VMEM((PAST_PAD + tm, CONV_CH), F32),
                        pltpu.VMEM((SUBLANE - 1, PAST_PAD + tm - SUBLANE, CONV_CH), F32),
                        pltpu.VMEM((tm, CONV_CH), F32)],
        compiler_params=_params(("arbitrary", "arbitrary"), 48),
        name="conv_branch",
    )(h3d, past, W['w_glu_a'], W['w_glu_g'], W['b_glu_a'], W['b_glu_g'], W['w_dw'], W['b_dw'],
      W['g_ln'], W['b_ln'], W['w_pw'], W['b_pw'])


def _mix_kernel(h_ref, attn_ref, yb_ref, wga_ref, wgb_ref, wao_ref, bga_ref, bgb_ref, m_ref):
    h = h_ref[...]
    ga = jax.nn.sigmoid(lax.dot_general(h, wga_ref[...], _NT, preferred_element_type=F32) + bga_ref[...])
    gb = jax.nn.sigmoid(lax.dot_general(h, wgb_ref[...], _NT, preferred_element_type=F32) + bgb_ref[...])
    ya = jnp.dot(attn_ref[...], wao_ref[...], preferred_element_type=F32)
    m_ref[...] = (ga * ya + gb * yb_ref[...].astype(F32)).astype(BF16)


def _mix(h, attn, yb, W, tm, tn):
    T, D = h.shape
    row = lambda i, j: (i, 0)
    col = lambda i, j: (0, j)
    blk = lambda i, j: (i, j)
    return pl.pallas_call(
        _mix_kernel,
        grid=(T // tm, D // tn),
        in_specs=[
            pl.BlockSpec((tm, D), row),
            pl.BlockSpec((tm, D), row),
            pl.BlockSpec((tm, tn), blk),
            pl.BlockSpec((tn, D), lambda i, j: (j, 0)),
            pl.BlockSpec((tn, D), lambda i, j: (j, 0)),
            pl.BlockSpec((D, tn), col),
            pl.BlockSpec((1, tn), col),
            pl.BlockSpec((1, tn), col),
        ],
        out_specs=pl.BlockSpec((tm, tn), blk),
        out_shape=jax.ShapeDtypeStruct((T, D), BF16),
        compiler_params=_params(("arbitrary", "arbitrary"), 48),
        name="gated_mix",
    )(h, attn, yb, W['w_gate_a_t'], W['w_gate_b_t'], W['w_attn_out'], W['b_gate_a'], W['b_gate_b'])


def _outproj_kernel(x_ref, m_ref, w_ref, o_ref):
    o_ref[...] = x_ref[...] + jnp.dot(m_ref[...], w_ref[...], preferred_element_type=F32)


def _out_proj(x2d, m, W, tm):
    T, D = x2d.shape
    return pl.pallas_call(
        _outproj_kernel,
        grid=(T // tm,),
        in_specs=[
            pl.BlockSpec((tm, D), lambda i: (i, 0)),
            pl.BlockSpec((tm, D), lambda i: (i, 0)),
            pl.BlockSpec((D, D), lambda i: (0, 0)),
        ],
        out_specs=pl.BlockSpec((tm, D), lambda i: (i, 0)),
        out_shape=jax.ShapeDtypeStruct((T, D), F32),
        compiler_params=_params(("arbitrary",), 48),
        name="out_proj",
    )(x2d, m, W['w_out'])


def _ffn_kernel(x_ref, g_ref, wg_ref, wu_ref, wd_ref, o_ref, h_ref):
    @pl.when(pl.program_id(1) == 0)
    def _():
        x = x_ref[...]
        h_ref[...] = (x * _rms_scale(x, x.shape[-1]) * g_ref[...]).astype(BF16)
        o_ref[...] = x

    h = h_ref[...]
    gate = jnp.dot(h, wg_ref[...], preferred_element_type=F32)
    up = jnp.dot(h, wu_ref[...], preferred_element_type=F32)
    act = (gate * jax.nn.sigmoid(gate) * up).astype(BF16)
    o_ref[...] += jnp.dot(act, wd_ref[...], preferred_element_type=F32)


def _ffn(x2d, W, tm, tf):
    T, D = x2d.shape
    d_ff = W['w_ffn_gate'].shape[1]
    return pl.pallas_call(
        _ffn_kernel,
        grid=(T // tm, d_ff // tf),
        in_specs=[
            pl.BlockSpec((tm, D), lambda i, j: (i, 0)),
            pl.BlockSpec((1, D), lambda i, j: (0, 0)),
            pl.BlockSpec((D, tf), lambda i, j: (0, j)),
            pl.BlockSpec((D, tf), lambda i, j: (0, j)),
            pl.BlockSpec((tf, D), lambda i, j: (j, 0)),
        ],
        out_specs=pl.BlockSpec((tm, D), lambda i, j: (i, 0)),
        out_shape=jax.ShapeDtypeStruct((T, D), F32),
        scratch_shapes=[pltpu.VMEM((tm, D), BF16)],
        compiler_params=_params(("arbitrary", "arbitrary"), 60),
        name="ffn",
    )(x2d, W['g_ffn'], W['w_ffn_gate'], W['w_ffn_up'], W['w_ffn_down'])


def _rot_half_cols(w):
    half = QK_ROPE // 2
    return jnp.concatenate([-w[..., half:], w[..., :half]], axis=-1)


def _prep_weights(lw):
    (g_mix_norm, w_in, b_glu, b_gate, g_q_a, w_q_up, g_q_norm, g_kv_a, w_kv_up, g_k_norm,
     w_attn_out, w_dw, b_dw, g_conv_ln, b_conv_ln, w_conv_out, b_conv_out, w_out,
     g_ffn_norm, w_ffn_gate, w_ffn_up, w_ffn_down) = lw
    D = w_in.shape[0]
    o_kv = Q_RANK
    o_pe = o_kv + KV_RANK
    o_glu = o_pe + QK_ROPE
    o_gate = o_glu + 2 * CONV_CH
    w_t = w_in.T
    w_pe_t = w_t[o_pe:o_glu]
    half = QK_ROPE // 2
    w_pe_rot_t = jnp.concatenate([-w_pe_t[half:], w_pe_t[:half]], axis=0)
    row = lambda v: v.reshape(1, -1).astype(F32)
    W = {
        'g_mix': row(g_mix_norm),
        'w_small_t': jnp.concatenate([w_t[:o_pe], w_pe_t, w_pe_t, w_pe_rot_t, w_pe_rot_t], axis=0).astype(BF16),
        'g_q_a': row(g_q_a),
        'g_kv_a': row(g_kv_a),
        'w_glu_a': w_t[o_glu:o_glu + CONV_CH].T.astype(BF16),
        'w_glu_g': w_t[o_glu + CONV_CH:o_gate].T.astype(BF16),
        'b_glu_a': row(b_glu[:CONV_CH]),
        'b_glu_g': row(b_glu[CONV_CH:]),
        'w_gate_a_t': w_t[o_gate:o_gate + D].astype(BF16),
        'w_gate_b_t': w_t[o_gate + D:].astype(BF16),
        'b_gate_a': row(b_gate[:D]),
        'b_gate_b': row(b_gate[D:]),
        'w_attn_out': w_attn_out.astype(BF16),
        'w_dw': jnp.broadcast_to(w_dw.astype(F32)[:, None, :], (CONV_WIDTH, 8, CONV_CH)),
        'b_dw': row(b_dw),
        'g_ln': row(g_conv_ln),
        'b_ln': row(b_conv_ln),
        'w_pw': w_conv_out.astype(BF16),
        'b_pw': row(b_conv_out),
        'w_out': w_out.astype(BF16),
        'g_ffn': row(g_ffn_norm),
        'w_ffn_gate': w_ffn_gate.astype(BF16),
        'w_ffn_up': w_ffn_up.astype(BF16),
        'w_ffn_down': w_ffn_down.astype(BF16),
    }
    wq = w_q_up.reshape(Q_RANK, N_PAIRS, 2, QK_HEAD)
    wq_nope = wq[..., :QK_NOPE].reshape(Q_RANK, N_PAIRS, 2 * QK_NOPE)
    wq_rope = wq[..., QK_NOPE:]
    W['w_q'] = jnp.concatenate(
        [wq_nope, wq_rope.reshape(Q_RANK, N_PAIRS, 2 * QK_ROPE),
         _rot_half_cols(wq_rope).reshape(Q_RANK, N_PAIRS, 2 * QK_ROPE)], axis=-1
    ).reshape(Q_RANK, N_PAIRS * 4 * LANE).astype(BF16)
    wkv = w_kv_up.reshape(KV_RANK, N_PAIRS, 2, QK_NOPE + V_HEAD)
    W['w_kv'] = jnp.concatenate(
        [wkv[..., :QK_NOPE].reshape(KV_RANK, N_PAIRS, 2 * QK_NOPE),
         wkv[..., QK_NOPE:].reshape(KV_RANK, N_PAIRS, 2 * V_HEAD)], axis=-1
    ).transpose(1, 0, 2).astype(BF16)
    W['w_kT'] = wkv[..., :QK_NOPE].reshape(KV_RANK, N_HEADS * QK_NOPE).T.astype(BF16)
    W['w_v'] = wkv[..., QK_NOPE:].reshape(KV_RANK, N_HEADS, V_HEAD).transpose(1, 0, 2).astype(BF16)
    dup = lambda v: jnp.concatenate([v, v]).reshape(1, LANE).astype(F32)
    W['g_q_nope'] = row(g_q_norm[:QK_NOPE])
    W['g_q_rope2'] = dup(g_q_norm[QK_NOPE:])
    W['g_k_nope'] = row(g_k_norm[:QK_NOPE])
    W['g_k_rope2'] = dup(g_k_norm[QK_NOPE:])
    return W


def _rope_tables(pos):
    inv_freq = 1.0 / (ROPE_THETA ** (jnp.arange(0, QK_ROPE, 2, dtype=F32) / QK_ROPE))
    ang = pos.astype(F32)[:, None] * inv_freq[None, :]
    return jnp.tile(jnp.cos(ang), (1, 4)), jnp.tile(jnp.sin(ang), (1, 4))


ROW_TILE = 512
WIDE_ROW_TILE = 1024
COL_TILE = 512
FFN_TILE = 512
CONV_ROW_TILE = 256
DECODE_KEY_TILE = 1024


def _layer(x, pos, past_ckv, past_kpe, past_conv, W):
    B, S, D = x.shape
    T = B * S
    x2d = x.reshape(T, D)
    c2, s2 = _rope_tables(pos)
    if B > 1:
        c2, s2 = jnp.tile(c2, (B, 1)), jnp.tile(s2, (B, 1))
    tm = min(T, ROW_TILE)
    tm_wide = min(T, WIDE_ROW_TILE)
    h, cq, ckv, ckvb, kpe, kpe2 = _inproj(x2d, c2, s2, W, tm)
    if past_ckv is None:
        qt = _q_up(cq, c2, s2, W, tm, transposed=True)
        k, vt = _kv_up(ckvb, kpe2, W, tm)
        attn = _flash_prompt(qt, k, vt)
        past_conv = jnp.zeros((B, CONV_WIDTH - 1, CONV_CH), x.dtype)
    else:
        q = _q_up(cq, c2, s2, W, tm, transposed=False)
        cache_kpet = jnp.swapaxes(past_kpe, 1, 2)
        cache_kpet2 = jnp.concatenate([cache_kpet, cache_kpet], axis=1)
        kpet2_new = jnp.swapaxes(kpe2.reshape(B, S, LANE), 1, 2)
        attn = _decode_attn(_q_absorb(q, W), ckvb, kpet2_new, past_ckv, cache_kpet2, W, DECODE_KEY_TILE)
    yb, conv_state = _conv_branch(h.reshape(B, S, D), past_conv, W, min(S, CONV_ROW_TILE))
    m = _mix(h, attn, yb.reshape(T, D), W, tm_wide, COL_TILE)
    x1 = _out_proj(x2d, m, W, tm)
    y = _ffn(x1, W, tm_wide, FFN_TILE)
    return (y.reshape(B, S, D), ckv.reshape(B, S, KV_RANK), kpe.reshape(B, S, QK_ROPE), conv_state)


def kernel(x_prompt, x_sample, cache_ckv, cache_kpe, state_conv, g_mix_norm, w_in, b_glu, b_gate, g_q_a,
           w_q_up, g_q_norm, g_kv_a, w_kv_up, g_k_norm, w_attn_out, w_dw, b_dw, g_conv_ln, b_conv_ln,
           w_conv_out, b_conv_out, w_out, g_ffn_norm, w_ffn_gate, w_ffn_up, w_ffn_down):
    weights = (g_mix_norm, w_in, b_glu, b_gate, g_q_a, w_q_up, g_q_norm, g_kv_a, w_kv_up, g_k_norm,
               w_attn_out, w_dw, b_dw, g_conv_ln, b_conv_ln, w_conv_out, b_conv_out, w_out,
               g_ffn_norm, w_ffn_gate, w_ffn_up, w_ffn_down)
    depth = w_in.shape[0]
    pos_prompt = jnp.arange(x_prompt.shape[1])
    pos_sample = cache_ckv.shape[2] + jnp.arange(x_sample.shape[1])
    y_prompt, y_sample = x_prompt, x_sample
    outs = [[] for _ in range(6)]
    for l in range(depth):
        W = _prep_weights(tuple(w[l] for w in weights))
        y_prompt, ckv, kpe, conv = _layer(y_prompt, pos_prompt, None, None, None, W)
        outs[0].append(ckv); outs[1].append(kpe); outs[2].append(conv)
        y_sample, ckv, kpe, conv = _layer(y_sample, pos_sample, cache_ckv[l], cache_kpe[l], state_conv[l], W)
        outs[3].append(ckv); outs[4].append(kpe); outs[5].append(conv)
    return (y_prompt, y_sample) + tuple(jnp.stack(o) for o in outs)
```

```python
import functools
import math

import jax
import jax.numpy as jnp
from jax import lax
from jax.experimental import pallas as pl
from jax.experimental.pallas import tpu as pltpu

F32 = jnp.float32
BF16 = jnp.bfloat16

CHUNK = 64
N_HEADS = 16
Q_RANK = 512
KV_RANK = 512
QK_NOPE = 128
QK_ROPE = 64
QK_HEAD = QK_NOPE + QK_ROPE
V_HEAD = 128
CONV_CH = 1024
CONV_WIDTH = 31
ROPE_THETA = 10000.0
EPS = 1e-6
NEG_INF = -1e30
SCALE = QK_HEAD ** -0.5
LOG2E = math.log2(math.e)

LANE = 128
QK_PAD = 2 * LANE
N_PAIRS = N_HEADS // 2
PAST_PAD = 32
MIB = 1024 * 1024
_NT = (((1,), (1,)), ((), ()))


def _params(semantics, vmem_mib):
    return pltpu.CompilerParams(dimension_semantics=semantics, vmem_limit_bytes=vmem_mib * MIB)


def _rms_scale(v, n):
    return lax.rsqrt(jnp.sum(v * v, axis=-1, keepdims=True) * (1.0 / n) + EPS)


def _half_masks():
    lane = lax.broadcasted_iota(jnp.int32, (1, LANE), 1)
    lo = (lane < QK_ROPE).astype(F32)
    return lo, 1.0 - lo


def _inproj_kernel(x_ref, g_ref, w_ref, gq_ref, gkv_ref, c2_ref, s2_ref,
                   h_ref, cq_ref, ckv_ref, ckvb_ref, kpe_ref, kpe2_ref):
    x = x_ref[...]
    h = (x * _rms_scale(x, x.shape[-1]) * g_ref[...]).astype(BF16)
    h_ref[...] = h
    z = lax.dot_general(h, w_ref[...], _NT, preferred_element_type=F32)
    cq = z[:, :Q_RANK]
    cq_ref[...] = (cq * _rms_scale(cq, Q_RANK) * gq_ref[...]).astype(BF16)
    ckv = z[:, Q_RANK:Q_RANK + KV_RANK]
    ckv = ckv * _rms_scale(ckv, KV_RANK) * gkv_ref[...]
    ckv_ref[...] = ckv
    ckvb_ref[...] = ckv.astype(BF16)
    base = Q_RANK + KV_RANK
    kpe2 = z[:, base:base + LANE] * c2_ref[...] + z[:, base + LANE:base + 2 * LANE] * s2_ref[...]
    kpe2_ref[...] = kpe2
    kpe_ref[...] = kpe2[:, :QK_ROPE]


def _inproj(x2d, c2, s2, W, tm):
    T, D = x2d.shape
    n_in = W['w_small_t'].shape[0]
    row = lambda i: (i, 0)
    const = lambda i: (0, 0)
    return pl.pallas_call(
        _inproj_kernel,
        grid=(T // tm,),
        in_specs=[
            pl.BlockSpec((tm, D), row),
            pl.BlockSpec((1, D), const),
            pl.BlockSpec((n_in, D), const),
            pl.BlockSpec((1, Q_RANK), const),
            pl.BlockSpec((1, KV_RANK), const),
            pl.BlockSpec((tm, LANE), row),
            pl.BlockSpec((tm, LANE), row),
        ],
        out_specs=[
            pl.BlockSpec((tm, D), row),
            pl.BlockSpec((tm, Q_RANK), row),
            pl.BlockSpec((tm, KV_RANK), row),
            pl.BlockSpec((tm, KV_RANK), row),
            pl.BlockSpec((tm, QK_ROPE), row),
            pl.BlockSpec((tm, LANE), row),
        ],
        out_shape=[
            jax.ShapeDtypeStruct((T, D), BF16),
            jax.ShapeDtypeStruct((T, Q_RANK), BF16),
            jax.ShapeDtypeStruct((T, KV_RANK), F32),
            jax.ShapeDtypeStruct((T, KV_RANK), BF16),
            jax.ShapeDtypeStruct((T, QK_ROPE), F32),
            jax.ShapeDtypeStruct((T, LANE), F32),
        ],
        compiler_params=_params(("arbitrary",), 48),
        name="inproj",
    )(x2d, W['g_mix'], W['w_small_t'], W['g_q_a'], W['g_kv_a'], c2, s2)


def _qup_kernel(cq_ref, w_ref, c2_ref, s2_ref, gn_ref, gr2_ref, q_ref, *, transposed):
    cq = cq_ref[...]
    masks = _half_masks()
    for p in range(N_PAIRS):
        z = jnp.dot(cq, w_ref[:, p * 4 * LANE:(p + 1) * 4 * LANE], preferred_element_type=F32)
        rope2 = z[:, 2 * LANE:3 * LANE] * c2_ref[...] + z[:, 3 * LANE:] * s2_ref[...]
        for e, msk in enumerate(masks):
            nope = z[:, e * LANE:(e + 1) * LANE]
            rope = rope2 * msk
            ss = jnp.sum(nope * nope, axis=-1, keepdims=True) + jnp.sum(rope * rope, axis=-1, keepdims=True)
            r = lax.rsqrt(ss * (1.0 / QK_HEAD) + EPS) * (SCALE * LOG2E)
            if transposed:
                q_ref[2 * p + e, 0, :QK_NOPE, :] = (nope * r * gn_ref[...]).T.astype(BF16)
                q_ref[2 * p + e, 0, QK_NOPE:, :] = (rope * r * gr2_ref[...]).T.astype(BF16)
            else:
                q_ref[2 * p + e, :, :QK_NOPE] = (nope * r * gn_ref[...]).astype(BF16)
                q_ref[2 * p + e, :, QK_NOPE:] = (rope * r * gr2_ref[...]).astype(BF16)


def _q_up(cq, c2, s2, W, tm, transposed):
    T = cq.shape[0]
    row = lambda i: (i, 0)
    const = lambda i: (0, 0)
    if transposed:
        out_spec = pl.BlockSpec((N_HEADS, 1, QK_PAD, tm), lambda i: (0, i, 0, 0))
        out_shape = jax.ShapeDtypeStruct((N_HEADS, T // tm, QK_PAD, tm), BF16)
    else:
        out_spec = pl.BlockSpec((N_HEADS, tm, QK_PAD), lambda i: (0, i, 0))
        out_shape = jax.ShapeDtypeStruct((N_HEADS, T, QK_PAD), BF16)
    return pl.pallas_call(
        functools.partial(_qup_kernel, transposed=transposed),
        grid=(T // tm,),
        in_specs=[
            pl.BlockSpec((tm, Q_RANK), row),
            pl.BlockSpec((Q_RANK, N_PAIRS * 4 * LANE), const),
            pl.BlockSpec((tm, LANE), row),
            pl.BlockSpec((tm, LANE), row),
            pl.BlockSpec((1, LANE), const),
            pl.BlockSpec((1, LANE), const),
        ],
        out_specs=out_spec,
        out_shape=out_shape,
        compiler_params=_params(("arbitrary",), 48),
        name="q_up",
    )(cq, W['w_q'], c2, s2, W['g_q_nope'], W['g_q_rope2'])


V_ROWS = V_HEAD + 16


def _kvup_kernel(ckv_ref, kpe2_ref, w_ref, gn_ref, gr2_ref, k_ref, v_ref):
    ckv = ckv_ref[...]
    kpe2 = kpe2_ref[...]
    lo, hi = _half_masks()
    ss_pe = jnp.sum(kpe2 * kpe2 * lo, axis=-1, keepdims=True)
    tm = ckv.shape[0]
    for p in range(N_PAIRS):
        kv = jnp.dot(ckv, w_ref[p], preferred_element_type=F32)
        for e, msk in enumerate((lo, hi)):
            hd = 2 * p + e
            kn = kv[:, e * LANE:(e + 1) * LANE]
            r = lax.rsqrt((jnp.sum(kn * kn, axis=-1, keepdims=True) + ss_pe) * (1.0 / QK_HEAD) + EPS)
            k_ref[hd, :, :QK_NOPE] = (kn * r * gn_ref[...]).astype(BF16)
            k_ref[hd, :, QK_NOPE:] = (kpe2 * msk * r * gr2_ref[...]).astype(BF16)
            v_ref[hd, 0, :V_HEAD, :] = kv[:, (2 + e) * LANE:(3 + e) * LANE].T.astype(BF16)
            v_ref[hd, 0, V_HEAD:, :] = jnp.ones((V_ROWS - V_HEAD, tm), BF16)


def _kv_up(ckvb, kpe2, W, tm):
    T = ckvb.shape[0]
    row = lambda i: (i, 0)
    const = lambda i: (0, 0)
    return pl.pallas_call(
        _kvup_kernel,
        grid=(T // tm,),
        in_specs=[
            pl.BlockSpec((tm, KV_RANK), row),
            pl.BlockSpec((tm, LANE), row),
            pl.BlockSpec((N_PAIRS, KV_RANK, 4 * LANE), lambda i: (0, 0, 0)),
            pl.BlockSpec((1, LANE), const),
            pl.BlockSpec((1, LANE), const),
        ],
        out_specs=[
            pl.BlockSpec((N_HEADS, tm, QK_PAD), lambda i: (0, i, 0)),
            pl.BlockSpec((N_HEADS, 1, V_ROWS, tm), lambda i: (0, i, 0, 0)),
        ],
        out_shape=[
            jax.ShapeDtypeStruct((N_HEADS, T, QK_PAD), BF16),
            jax.ShapeDtypeStruct((N_HEADS, T // tm, V_ROWS, tm), BF16),
        ],
        compiler_params=_params(("arbitrary",), 48),
        name="kv_up",
    )(ckvb, kpe2, W['w_kv'], W['g_k_nope'], W['g_k_rope2'])


def _online_softmax(s, m_ref, l_ref):
    m_prev = m_ref[...]
    blocks = [s[:, c:c + LANE] for c in range(0, s.shape[1], LANE)]
    m_next = jnp.maximum(m_prev, jnp.max(functools.reduce(jnp.maximum, blocks), axis=-1, keepdims=True))
    blocks = [jnp.exp2(b - m_next) for b in blocks]
    alpha = jnp.exp2(m_prev - m_next)
    l_ref[...] = alpha * l_ref[...] + functools.reduce(jnp.add, blocks)
    m_ref[...] = m_next
    return jnp.concatenate(blocks, axis=-1).astype(BF16), alpha


N_CHAINS = 2
FLASH_UNROLLS = (8, 4, 2, 1)
FLASH_Q_PER_STEP = 8


def _flash_kernel(qt_ref, k_ref, vt_ref, o_ref, s_ref, smax_ref, m_ref, acc_ref, *, tq):
    def scores(qt, j, chain):
        start = pl.multiple_of(j * tq, tq)
        s = jnp.dot(k_ref[0, pl.ds(start, tq), :], qt, preferred_element_type=F32)
        s_ref[chain] = s
        smax_ref[chain] = jnp.max(s, axis=0, keepdims=True)

    def consume(j, chain, diagonal=False):
        s = s_ref[chain]
        if diagonal:
            kchunk = lax.broadcasted_iota(jnp.int32, (tq, tq), 0) // CHUNK
            qchunk = lax.broadcasted_iota(jnp.int32, (tq, tq), 1) // CHUNK
            s = jnp.where(kchunk <= qchunk, s, NEG_INF)
            s_max = jnp.max(s, axis=0, keepdims=True)
        else:
            s_max = smax_ref[chain]
        m_prev = m_ref[chain]
        m_next = jnp.maximum(m_prev, s_max)
        p = jnp.exp2(s - m_next).astype(BF16)
        alpha = jnp.exp2(m_prev - m_next)
        acc_ref[chain] = alpha * acc_ref[chain] + jnp.dot(vt_ref[0, j], p, preferred_element_type=F32)
        m_ref[chain] = m_next

    first = pl.program_id(1) * FLASH_Q_PER_STEP

    def q_tile(u, carry):
        n = first + u + 1
        qt = qt_ref[0, u]
        qt_next = qt_ref[0, jnp.minimum(u + 1, FLASH_Q_PER_STEP - 1)]
        m_ref[...] = jnp.full(m_ref.shape, -jnp.inf, F32)
        acc_ref[...] = jnp.zeros(acc_ref.shape, F32)

        def next_tile_scores():
            scores(qt_next, 0, 0)
            scores(qt_next, 1, 1)

        def pair(jj):
            consume(2 * jj, 0)
            scores(qt, 2 * jj + 2, 0)
            consume(2 * jj + 1, 1)
            scores(qt, 2 * jj + 3, 1)

        def pairs(count):
            def body(jj, c):
                for r in range(count):
                    pair(count * jj + r)
                return c
            return body

        n_pairs = jnp.maximum(n // 2 - 1, 0)
        done = 0
        for count in FLASH_UNROLLS:
            iters = (n_pairs - done) // count
            lax.fori_loop(done // count, done // count + iters, pairs(count), 0)
            done = done + iters * count

        @pl.when(n % 2 == 0)
        def _():
            consume(n - 2, 0)
            consume(n - 1, 1, diagonal=True)
            next_tile_scores()

        @pl.when(jnp.logical_and(n % 2 == 1, n >= 3))
        def _():
            consume(n - 3, 0)
            scores(qt, n - 1, 0)
            consume(n - 2, 1)
            consume(n - 1, 0, diagonal=True)
            next_tile_scores()

        @pl.when(n == 1)
        def _():
            consume(0, 0, diagonal=True)
            next_tile_scores()

        m = jnp.maximum(m_ref[0], m_ref[1])
        acc = jnp.exp2(m_ref[0] - m) * acc_ref[0] + jnp.exp2(m_ref[1] - m) * acc_ref[1]
        out_t = acc[:V_HEAD] / acc[V_HEAD:V_HEAD + 1]
        o_ref[pl.ds(pl.multiple_of(u * tq, tq), tq), :] = out_t.T.astype(BF16)
        return carry

    scores(qt_ref[0, 0], 0, 0)

    @pl.when(first >= 1)
    def _():
        scores(qt_ref[0, 0], 1, 1)

    lax.fori_loop(0, FLASH_Q_PER_STEP, q_tile, 0)


def _flash_prompt(qt, k, vt):
    H, n_q, _, tq = qt.shape
    S = k.shape[1]
    per_step = FLASH_Q_PER_STEP
    return pl.pallas_call(
        functools.partial(_flash_kernel, tq=tq),
        grid=(H, n_q // per_step),
        in_specs=[
            pl.BlockSpec((1, per_step, QK_PAD, tq), lambda h, i: (h, i, 0, 0)),
            pl.BlockSpec((1, S, QK_PAD), lambda h, i: (h, 0, 0)),
            pl.BlockSpec((1, n_q, V_ROWS, tq), lambda h, i: (h, 0, 0, 0)),
        ],
        out_specs=pl.BlockSpec((per_step * tq, V_HEAD), lambda h, i: (i, h)),
        out_shape=jax.ShapeDtypeStruct((S, H * V_HEAD), BF16),
        scratch_shapes=[pltpu.VMEM((N_CHAINS, tq, tq), F32),
                        pltpu.VMEM((N_CHAINS, 1, tq), F32),
                        pltpu.VMEM((N_CHAINS, 1, tq), F32),
                        pltpu.VMEM((N_CHAINS, V_ROWS, tq), F32)],
        compiler_params=_params(("arbitrary", "arbitrary"), 48),
        name="flash_prompt",
    )(qt, k, vt)


Q_LAT = KV_RANK + LANE


def _qabsorb_kernel(q_ref, wkt_ref, gn_ref, gr2_ref, o_ref):
    q = q_ref[0]
    qn = (q[:, :QK_NOPE].astype(F32) * gn_ref[...]).astype(BF16)
    o_ref[0, :, :KV_RANK] = jnp.dot(qn, wkt_ref[...], preferred_element_type=F32).astype(BF16)
    o_ref[0, :, KV_RANK:] = (q[:, QK_NOPE:].astype(F32) * gr2_ref[...]).astype(BF16)


def _q_absorb(q, W):
    H, T, _ = q.shape
    return pl.pallas_call(
        _qabsorb_kernel,
        grid=(H,),
        in_specs=[
            pl.BlockSpec((1, T, QK_PAD), lambda h: (h, 0, 0)),
            pl.BlockSpec((QK_NOPE, KV_RANK), lambda h: (h, 0)),
            pl.BlockSpec((1, LANE), lambda h: (0, 0)),
            pl.BlockSpec((1, LANE), lambda h: (0, 0)),
        ],
        out_specs=pl.BlockSpec((1, T, Q_LAT), lambda h: (h, 0, 0)),
        out_shape=jax.ShapeDtypeStruct((H, T, Q_LAT), BF16),
        compiler_params=_params(("arbitrary",), 32),
        name="q_absorb",
    )(q, W['w_kT'], W['g_k_nope'], W['g_k_rope2'])


def _decode_kernel(q_ref, ckvn_ref, kpetn_ref, ckvc_ref, kpetc_ref, wkt_ref, wv_ref, o_ref,
                   m_ref, l_ref, acc_ref, p_ref, alpha_ref, ckvpad_ref, kpetpad_ref, *, n_tiles, t_new):
    j = pl.program_id(1)
    rows = N_HEADS * t_new
    qa = q_ref[...].reshape(rows, Q_LAT)

    def attend(ckv, kpet2, n_valid):
        n = ckv.shape[0]
        knt = lax.dot_general(wkt_ref[...], ckv, _NT, preferred_element_type=F32)
        t = (lax.dot_general(qa[:, :KV_RANK], ckv, _NT, preferred_element_type=F32)
             + jnp.dot(qa[:, KV_RANK:], kpet2.astype(BF16), preferred_element_type=F32))
        kpet = kpet2[:QK_ROPE]
        ss_pe = jnp.sum(kpet * kpet, axis=0, keepdims=True)
        valid = lax.broadcasted_iota(jnp.int32, (1, n), 1) < n_valid
        for hd in range(N_HEADS):
            kn = knt[hd * QK_NOPE:(hd + 1) * QK_NOPE]
            r = lax.rsqrt((jnp.sum(kn * kn, axis=0, keepdims=True) + ss_pe) * (1.0 / QK_HEAD) + EPS)
            s = t[hd * t_new:(hd + 1) * t_new] * r
            if n_valid < n:
                s = jnp.where(valid, s, NEG_INF)
            p, alpha = _online_softmax(s, m_ref.at[hd], l_ref.at[hd])
            p_ref[hd * t_new:(hd + 1) * t_new, :n] = p
            alpha_ref[hd * t_new:(hd + 1) * t_new, :] = alpha
        pv = jnp.dot(p_ref[:, :n], ckv, preferred_element_type=F32)
        alpha = alpha_ref[...]
        acc_ref[...] = jnp.concatenate([alpha] * (KV_RANK // LANE), axis=1) * acc_ref[...] + pv

    @pl.when(j == 0)
    def _():
        m_ref[...] = jnp.full(m_ref.shape, -jnp.inf, F32)
        l_ref[...] = jnp.zeros(l_ref.shape, F32)
        acc_ref[...] = jnp.zeros(acc_ref.shape, F32)
        ckvpad_ref[...] = jnp.zeros(ckvpad_ref.shape, BF16)
        ckvpad_ref[:t_new, :] = ckvn_ref[...]
        kpetpad_ref[...] = jnp.zeros(kpetpad_ref.shape, F32)
        kpetpad_ref[:, :t_new] = kpetn_ref[0]
        attend(ckvpad_ref[...], kpetpad_ref[...], t_new)

    tk = ckvc_ref.shape[1]
    attend(ckvc_ref[0].astype(BF16), kpetc_ref[0], tk)

    @pl.when(j == n_tiles - 1)
    def _():
        for hd in range(N_HEADS):
            l = jnp.sum(l_ref[hd], axis=-1, keepdims=True)
            lat = (acc_ref[hd * t_new:(hd + 1) * t_new, :] / l).astype(BF16)
            o_ref[:, hd * V_HEAD:(hd + 1) * V_HEAD] = jnp.dot(
                lat, wv_ref[hd], preferred_element_type=F32).astype(BF16)


def _decode_attn(qa, ckvb_new, kpet2_new, cache_ckv, cache_kpet2, W, tk):
    B, P, _ = cache_ckv.shape
    T = ckvb_new.shape[0]
    t_new = T // B
    n_tiles = P // tk
    rows = N_HEADS * t_new
    return pl.pallas_call(
        functools.partial(_decode_kernel, n_tiles=n_tiles, t_new=t_new),
        grid=(B, n_tiles),
        in_specs=[
            pl.BlockSpec((N_HEADS, t_new, Q_LAT), lambda b, j: (0, b, 0)),
            pl.BlockSpec((t_new, KV_RANK), lambda b, j: (b, 0)),
            pl.BlockSpec((1, LANE, t_new), lambda b, j: (b, 0, 0)),
            pl.BlockSpec((1, tk, KV_RANK), lambda b, j: (b, j, 0)),
            pl.BlockSpec((1, LANE, tk), lambda b, j: (b, 0, j)),
            pl.BlockSpec((N_HEADS * QK_NOPE, KV_RANK), lambda b, j: (0, 0)),
            pl.BlockSpec((N_HEADS, KV_RANK, V_HEAD), lambda b, j: (0, 0, 0)),
        ],
        out_specs=pl.BlockSpec((t_new, N_HEADS * V_HEAD), lambda b, j: (b, 0)),
        out_shape=jax.ShapeDtypeStruct((T, N_HEADS * V_HEAD), BF16),
        scratch_shapes=[
            pltpu.VMEM((N_HEADS, t_new, LANE), F32),
            pltpu.VMEM((N_HEADS, t_new, LANE), F32),
            pltpu.VMEM((rows, KV_RANK), F32),
            pltpu.VMEM((rows, tk), BF16),
            pltpu.VMEM((rows, LANE), F32),
            pltpu.VMEM((LANE, KV_RANK), BF16),
            pltpu.VMEM((LANE, LANE), F32),
        ],
        compiler_params=_params(("arbitrary", "arbitrary"), 56),
        name="decode_attn",
    )(qa, ckvb_new, kpet2_new, cache_ckv, cache_kpet2, W['w_kT'], W['w_v'])


CONV_ROWS = 64
CONV_COLS = 256
SUBLANE = 8


def _conv_kernel(h_ref, past_ref, wa_ref, wg_ref, ba_ref, bg_ref, wdw_ref, bdw_ref, gln_ref, bln_ref,
                 wpw_ref, bpw_ref, yb_ref, state_ref, buf_ref, shift_ref, y_ref, *, tm, n_t):
    t = pl.program_id(1)
    hist = CONV_WIDTH - 1
    off = PAST_PAD - hist

    @pl.when(t == 0)
    def _():
        buf_ref[0:PAST_PAD, :] = jnp.zeros((PAST_PAD, CONV_CH), F32)
        buf_ref[off:PAST_PAD, :] = past_ref[0]

    @pl.when(t > 0)
    def _():
        buf_ref[0:PAST_PAD, :] = buf_ref[tm:tm + PAST_PAD, :]

    h = h_ref[0]
    a = jnp.dot(h, wa_ref[...], preferred_element_type=F32) + ba_ref[...]
    g = jnp.dot(h, wg_ref[...], preferred_element_type=F32) + bg_ref[...]
    buf_ref[PAST_PAD:PAST_PAD + tm, :] = a * jax.nn.sigmoid(g)

    span = shift_ref.shape[1]
    for r in range(1, SUBLANE):
        shift_ref[r - 1] = buf_ref[r:r + span, :]
    rows = min(CONV_ROWS, tm)
    for r0 in range(0, tm, rows):
        for c0 in range(0, CONV_CH, CONV_COLS):
            cols = slice(c0, c0 + CONV_COLS)
            acc = jnp.zeros((rows, CONV_COLS), F32)
            for q in range(off, PAST_PAD + 1):
                r, base = q % SUBLANE, r0 + q - q % SUBLANE
                src = buf_ref if r == 0 else shift_ref.at[r - 1]
                acc = acc + wdw_ref[q - off:q - off + 1, cols] * src[base:base + rows, cols]
            y_ref[r0:r0 + rows, cols] = acc

    y = y_ref[...] + bdw_ref[...]
    yc = y - jnp.mean(y, axis=-1, keepdims=True)
    y = yc * lax.rsqrt(jnp.mean(yc * yc, axis=-1, keepdims=True) + EPS) * gln_ref[...] + bln_ref[...]
    y = y * jax.nn.sigmoid(y)
    yb = jnp.dot(y.astype(BF16), wpw_ref[...], preferred_element_type=F32) + bpw_ref[...]
    yb_ref[0] = yb.astype(BF16)

    @pl.when(t == n_t - 1)
    def _():
        state_ref[0] = buf_ref[tm + off:tm + PAST_PAD, :]


def _conv_branch(h3d, past, W, tm):
    B, S, D = h3d.shape
    n_t = S // tm
    hist = CONV_WIDTH - 1
    const = lambda b, t: (0, 0)
    return pl.pallas_call(
        functools.partial(_conv_kernel, tm=tm, n_t=n_t),
        grid=(B, n_t),
        in_specs=[
            pl.BlockSpec((1, tm, D), lambda b, t: (b, t, 0)),
            pl.BlockSpec((1, hist, CONV_CH), lambda b, t: (b, 0, 0)),
            pl.BlockSpec((D, CONV_CH), const),
            pl.BlockSpec((D, CONV_CH), const),
            pl.BlockSpec((1, CONV_CH), const),
            pl.BlockSpec((1, CONV_CH), const),
            pl.BlockSpec((CONV_WIDTH, CONV_CH), const),
            pl.BlockSpec((1, CONV_CH), const),
            pl.BlockSpec((1, CONV_CH), const),
            pl.BlockSpec((1, CONV_CH), const),
            pl.BlockSpec((CONV_CH, D), const),
            pl.BlockSpec((1, D), const),
        ],
        out_specs=[
            pl.BlockSpec((1, tm, D), lambda b, t: (b, t, 0)),
            pl.BlockSpec((1, hist, CONV_CH), lambda b, t: (b, 0, 0)),
        ],
        out_shape=[
            jax.ShapeDtypeStruct((B, S, D), BF16),
            jax.ShapeDtypeStruct((B, hist, CONV_CH), F32),
        ],
        scratch_shapes=[pltpu.VMEM((PAST_PAD + tm, CONV_CH), F32),
                        pltpu.VMEM((SUBLANE - 1, PAST_PAD + tm - SUBLANE, CONV_CH), F32),
                        pltpu.VMEM((tm, CONV_CH), F32)],
        compiler_params=_params(("arbitrary", "arbitrary"), 48),
        name="conv_branch",
    )(h3d, past, W['w_glu_a'], W['w_glu_g'], W['b_glu_a'], W['b_glu_g'], W['w_dw'], W['b_dw'],
      W['g_ln'], W['b_ln'], W['w_pw'], W['b_pw'])


def _mix_kernel(h_ref, attn_ref, yb_ref, wga_ref, wgb_ref, wao_ref, bga_ref, bgb_ref, m_ref):
    h = h_ref[...]
    ga = jax.nn.sigmoid(lax.dot_general(h, wga_ref[...], _NT, preferred_element_type=F32) + bga_ref[...])
    gb = jax.nn.sigmoid(lax.dot_general(h, wgb_ref[...], _NT, preferred_element_type=F32) + bgb_ref[...])
    ya = jnp.dot(attn_ref[...], wao_ref[...], preferred_element_type=F32)
    m_ref[...] = (ga * ya + gb * yb_ref[...].astype(F32)).astype(BF16)


def _mix(h, attn, yb, W, tm, tn):
    T, D = h.shape
    row = lambda i, j: (i, 0)
    col = lambda i, j: (0, j)
    blk = lambda i, j: (i, j)
    return pl.pallas_call(
        _mix_kernel,
        grid=(T // tm, D // tn),
        in_specs=[
            pl.BlockSpec((tm, D), row),
            pl.BlockSpec((tm, D), row),
            pl.BlockSpec((tm, tn), blk),
            pl.BlockSpec((tn, D), lambda i, j: (j, 0)),
            pl.BlockSpec((tn, D), lambda i, j: (j, 0)),
            pl.BlockSpec((D, tn), col),
            pl.BlockSpec((1, tn), col),
            pl.BlockSpec((1, tn), col),
        ],
        out_specs=pl.BlockSpec((tm, tn), blk),
        out_shape=jax.ShapeDtypeStruct((T, D), BF16),
        compiler_params=_params(("arbitrary", "arbitrary"), 48),
        name="gated_mix",
    )(h, attn, yb, W['w_gate_a_t'], W['w_gate_b_t'], W['w_attn_out'], W['b_gate_a'], W['b_gate_b'])


def _outproj_kernel(x_ref, m_ref, w_ref, o_ref):
    o_ref[...] = x_ref[...] + jnp.dot(m_ref[...], w_ref[...], preferred_element_type=F32)


def _out_proj(x2d, m, W, tm):
    T, D = x2d.shape
    return pl.pallas_call(
        _outproj_kernel,
        grid=(T // tm,),
        in_specs=[
            pl.BlockSpec((tm, D), lambda i: (i, 0)),
            pl.BlockSpec((tm, D), lambda i: (i, 0)),
            pl.BlockSpec((D, D), lambda i: (0, 0)),
        ],
        out_specs=pl.BlockSpec((tm, D), lambda i: (i, 0)),
        out_shape=jax.ShapeDtypeStruct((T, D), F32),
        compiler_params=_params(("arbitrary",), 48),
        name="out_proj",
    )(x2d, m, W['w_out'])


def _ffn_kernel(x_ref, g_ref, wg_ref, wu_ref, wd_ref, o_ref, h_ref):
    @pl.when(pl.program_id(1) == 0)
    def _():
        x = x_ref[...]
        h_ref[...] = (x * _rms_scale(x, x.shape[-1]) * g_ref[...]).astype(BF16)
        o_ref[...] = x

    h = h_ref[...]
    gate = jnp.dot(h, wg_ref[...], preferred_element_type=F32)
    up = jnp.dot(h, wu_ref[...], preferred_element_type=F32)
    act = (gate * jax.nn.sigmoid(gate) * up).astype(BF16)
    o_ref[...] += jnp.dot(act, wd_ref[...], preferred_element_type=F32)


def _ffn(x2d, W, tm, tf):
    T, D = x2d.shape
    d_ff = W['w_ffn_gate'].shape[1]
    return pl.pallas_call(
        _ffn_kernel,
        grid=(T // tm, d_ff // tf),
        in_specs=[
            pl.BlockSpec((tm, D), lambda i, j: (i, 0)),
            pl.BlockSpec((1, D), lambda i, j: (0, 0)),
            pl.BlockSpec((D, tf), lambda i, j: (0, j)),
            pl.BlockSpec((D, tf), lambda i, j: (0, j)),
            pl.BlockSpec((tf, D), lambda i, j: (j, 0)),
        ],
        out_specs=pl.BlockSpec((tm, D), lambda i, j: (i, 0)),
        out_shape=jax.ShapeDtypeStruct((T, D), F32),
        scratch_shapes=[pltpu.VMEM((tm, D), BF16)],
        compiler_params=_params(("arbitrary", "arbitrary"), 60),
        name="ffn",
    )(x2d, W['g_ffn'], W['w_ffn_gate'], W['w_ffn_up'], W['w_ffn_down'])


def _rot_half_cols(w):
    half = QK_ROPE // 2
    return jnp.concatenate([-w[..., half:], w[..., :half]], axis=-1)


def _prep_weights(lw):
    (g_mix_norm, w_in, b_glu, b_gate, g_q_a, w_q_up, g_q_norm, g_kv_a, w_kv_up, g_k_norm,
     w_attn_out, w_dw, b_dw, g_conv_ln, b_conv_ln, w_conv_out, b_conv_out, w_out,
     g_ffn_norm, w_ffn_gate, w_ffn_up, w_ffn_down) = lw
    D = w_in.shape[0]
    o_kv = Q_RANK
    o_pe = o_kv + KV_RANK
    o_glu = o_pe + QK_ROPE
    o_gate = o_glu + 2 * CONV_CH
    w_t = w_in.T
    w_pe_t = w_t[o_pe:o_glu]
    half = QK_ROPE // 2
    w_pe_rot_t = jnp.concatenate([-w_pe_t[half:], w_pe_t[:half]], axis=0)
    row = lambda v: v.reshape(1, -1).astype(F32)
    W = {
        'g_mix': row(g_mix_norm),
        'w_small_t': jnp.concatenate([w_t[:o_pe], w_pe_t, w_pe_t, w_pe_rot_t, w_pe_rot_t], axis=0).astype(BF16),
        'g_q_a': row(g_q_a),
        'g_kv_a': row(g_kv_a),
        'w_glu_a': w_t[o_glu:o_glu + CONV_CH].T.astype(BF16),
        'w_glu_g': w_t[o_glu + CONV_CH:o_gate].T.astype(BF16),
        'b_glu_a': row(b_glu[:CONV_CH]),
        'b_glu_g': row(b_glu[CONV_CH:]),
        'w_gate_a_t': w_t[o_gate:o_gate + D].astype(BF16),
        'w_gate_b_t': w_t[o_gate + D:].astype(BF16),
        'b_gate_a': row(b_gate[:D]),
        'b_gate_b': row(b_gate[D:]),
        'w_attn_out': w_attn_out.astype(BF16),
        'w_dw': w_dw.astype(F32),
        'b_dw': row(b_dw),
        'g_ln': row(g_conv_ln),
        'b_ln': row(b_conv_ln),
        'w_pw': w_conv_out.astype(BF16),
        'b_pw': row(b_conv_out),
        'w_out': w_out.astype(BF16),
        'g_ffn': row(g_ffn_norm),
        'w_ffn_gate': w_ffn_gate.astype(BF16),
        'w_ffn_up': w_ffn_up.astype(BF16),
        'w_ffn_down': w_ffn_down.astype(BF16),
    }
    wq = w_q_up.reshape(Q_RANK, N_PAIRS, 2, QK_HEAD)
    wq_nope = wq[..., :QK_NOPE].reshape(Q_RANK, N_PAIRS, 2 * QK_NOPE)
    wq_rope = wq[..., QK_NOPE:]
    W['w_q'] = jnp.concatenate(
        [wq_nope, wq_rope.reshape(Q_RANK, N_PAIRS, 2 * QK_ROPE),
         _rot_half_cols(wq_rope).reshape(Q_RANK, N_PAIRS, 2 * QK_ROPE)], axis=-1
    ).reshape(Q_RANK, N_PAIRS * 4 * LANE).astype(BF16)
    wkv = w_kv_up.reshape(KV_RANK, N_PAIRS, 2, QK_NOPE + V_HEAD)
    W['w_kv'] = jnp.concatenate(
        [wkv[..., :QK_NOPE].reshape(KV_RANK, N_PAIRS, 2 * QK_NOPE),
         wkv[..., QK_NOPE:].reshape(KV_RANK, N_PAIRS, 2 * V_HEAD)], axis=-1
    ).transpose(1, 0, 2).astype(BF16)
    W['w_kT'] = wkv[..., :QK_NOPE].reshape(KV_RANK, N_HEADS * QK_NOPE).T.astype(BF16)
    W['w_v'] = wkv[..., QK_NOPE:].reshape(KV_RANK, N_HEADS, V_HEAD).transpose(1, 0, 2).astype(BF16)
    dup = lambda v: jnp.concatenate([v, v]).reshape(1, LANE).astype(F32)
    W['g_q_nope'] = row(g_q_norm[:QK_NOPE])
    W['g_q_rope2'] = dup(g_q_norm[QK_NOPE:])
    W['g_k_nope'] = row(g_k_norm[:QK_NOPE])
    W['g_k_rope2'] = dup(g_k_norm[QK_NOPE:])
    return W


def _rope_tables(pos):
    inv_freq = 1.0 / (ROPE_THETA ** (jnp.arange(0, QK_ROPE, 2, dtype=F32) / QK_ROPE))
    ang = pos.astype(F32)[:, None] * inv_freq[None, :]
    return jnp.tile(jnp.cos(ang), (1, 4)), jnp.tile(jnp.sin(ang), (1, 4))


ROW_TILE = 512
WIDE_ROW_TILE = 1024
COL_TILE = 512
FFN_TILE = 512
CONV_ROW_TILE = 256
DECODE_KEY_TILE = 1024


def _layer(x, pos, past_ckv, past_kpe, past_conv, W):
    B, S, D = x.shape
    T = B * S
    x2d = x.reshape(T, D)
    c2, s2 = _rope_tables(pos)
    if B > 1:
        c2, s2 = jnp.tile(c2, (B, 1)), jnp.tile(s2, (B, 1))
    tm = min(T, ROW_TILE)
    tm_wide = min(T, WIDE_ROW_TILE)
    h, cq, ckv, ckvb, kpe, kpe2 = _inproj(x2d, c2, s2, W, tm)
    if past_ckv is None:
        qt = _q_up(cq, c2, s2, W, tm, transposed=True)
        k, vt = _kv_up(ckvb, kpe2, W, tm)
        attn = _flash_prompt(qt, k, vt)
        past_conv = jnp.zeros((B, CONV_WIDTH - 1, CONV_CH), x.dtype)
    else:
        q = _q_up(cq, c2, s2, W, tm, transposed=False)
        cache_kpet = jnp.swapaxes(past_kpe, 1, 2)
        cache_kpet2 = jnp.concatenate([cache_kpet, cache_kpet], axis=1)
        kpet2_new = jnp.swapaxes(kpe2.reshape(B, S, LANE), 1, 2)
        attn = _decode_attn(_q_absorb(q, W), ckvb, kpet2_new, past_ckv, cache_kpet2, W, DECODE_KEY_TILE)
    yb, conv_state = _conv_branch(h.reshape(B, S, D), past_conv, W, min(S, CONV_ROW_TILE))
    m = _mix(h, attn, yb.reshape(T, D), W, tm_wide, COL_TILE)
    x1 = _out_proj(x2d, m, W, tm)
    y = _ffn(x1, W, tm_wide, FFN_TILE)
    return (y.reshape(B, S, D), ckv.reshape(B, S, KV_RANK), kpe.reshape(B, S, QK_ROPE), conv_state)


def kernel(x_prompt, x_sample, cache_ckv, cache_kpe, state_conv, g_mix_norm, w_in, b_glu, b_gate, g_q_a,
           w_q_up, g_q_norm, g_kv_a, w_kv_up, g_k_norm, w_attn_out, w_dw, b_dw, g_conv_ln, b_conv_ln,
           w_conv_out, b_conv_out, w_out, g_ffn_norm, w_ffn_gate, w_ffn_up, w_ffn_down):
    weights = (g_mix_norm, w_in, b_glu, b_gate, g_q_a, w_q_up, g_q_norm, g_kv_a, w_kv_up, g_k_norm,
               w_attn_out, w_dw, b_dw, g_conv_ln, b_conv_ln, w_conv_out, b_conv_out, w_out,
               g_ffn_norm, w_ffn_gate, w_ffn_up, w_ffn_down)
    depth = w_in.shape[0]
    pos_prompt = jnp.arange(x_prompt.shape[1])
    pos_sample = cache_ckv.shape[2] + jnp.arange(x_sample.shape[1])
    y_prompt, y_sample = x_prompt, x_sample
    outs = [[] for _ in range(6)]
    for l in range(depth):
        W = _prep_weights(tuple(w[l] for w in weights))
        y_prompt, ckv, kpe, conv = _layer(y_prompt, pos_prompt, None, None, None, W)
        outs[0].append(ckv); outs[1].append(kpe); outs[2].append(conv)
        y_sample, ckv, kpe, conv = _layer(y_sample, pos_sample, cache_ckv[l], cache_kpe[l], state_conv[l], W)
        outs[3].append(ckv); outs[4].append(kpe); outs[5].append(conv)
    return (y_prompt, y_sample) + tuple(jnp.stack(o) for o in outs)
```

```python
import functools
import math

import jax
import jax.numpy as jnp
from jax import lax
from jax.experimental import pallas as pl
from jax.experimental.pallas import tpu as pltpu

F32 = jnp.float32
BF16 = jnp.bfloat16

CHUNK = 64
N_HEADS = 16
Q_RANK = 512
KV_RANK = 512
QK_NOPE = 128
QK_ROPE = 64
QK_HEAD = QK_NOPE + QK_ROPE
V_HEAD = 128
CONV_CH = 1024
CONV_WIDTH = 31
ROPE_THETA = 10000.0
EPS = 1e-6
NEG_INF = -1e30
SCALE = QK_HEAD ** -0.5
LOG2E = math.log2(math.e)

LANE = 128
QK_PAD = 2 * LANE
N_PAIRS = N_HEADS // 2
PAST_PAD = 32
MIB = 1024 * 1024
_NT = (((1,), (1,)), ((), ()))


def _params(semantics, vmem_mib):
    return pltpu.CompilerParams(dimension_semantics=semantics, vmem_limit_bytes=vmem_mib * MIB)


def _rms_scale(v, n):
    return lax.rsqrt(jnp.sum(v * v, axis=-1, keepdims=True) * (1.0 / n) + EPS)


def _half_masks():
    lane = lax.broadcasted_iota(jnp.int32, (1, LANE), 1)
    lo = (lane < QK_ROPE).astype(F32)
    return lo, 1.0 - lo


def _inproj_kernel(x_ref, g_ref, w_ref, gq_ref, gkv_ref, c2_ref, s2_ref,
                   h_ref, cq_ref, ckv_ref, ckvb_ref, kpe_ref, kpe2_ref):
    x = x_ref[...]
    h = (x * _rms_scale(x, x.shape[-1]) * g_ref[...]).astype(BF16)
    h_ref[...] = h
    z = lax.dot_general(h, w_ref[...], _NT, preferred_element_type=F32)
    cq = z[:, :Q_RANK]
    cq_ref[...] = (cq * _rms_scale(cq, Q_RANK) * gq_ref[...]).astype(BF16)
    ckv = z[:, Q_RANK:Q_RANK + KV_RANK]
    ckv = ckv * _rms_scale(ckv, KV_RANK) * gkv_ref[...]
    ckv_ref[...] = ckv
    ckvb_ref[...] = ckv.astype(BF16)
    base = Q_RANK + KV_RANK
    kpe2 = z[:, base:base + LANE] * c2_ref[...] + z[:, base + LANE:base + 2 * LANE] * s2_ref[...]
    kpe2_ref[...] = kpe2
    kpe_ref[...] = kpe2[:, :QK_ROPE]


def _inproj(x2d, c2, s2, W, tm):
    T, D = x2d.shape
    n_in = W['w_small_t'].shape[0]
    row = lambda i: (i, 0)
    const = lambda i: (0, 0)
    return pl.pallas_call(
        _inproj_kernel,
        grid=(T // tm,),
        in_specs=[
            pl.BlockSpec((tm, D), row),
            pl.BlockSpec((1, D), const),
            pl.BlockSpec((n_in, D), const),
            pl.BlockSpec((1, Q_RANK), const),
            pl.BlockSpec((1, KV_RANK), const),
            pl.BlockSpec((tm, LANE), row),
            pl.BlockSpec((tm, LANE), row),
        ],
        out_specs=[
            pl.BlockSpec((tm, D), row),
            pl.BlockSpec((tm, Q_RANK), row),
            pl.BlockSpec((tm, KV_RANK), row),
            pl.BlockSpec((tm, KV_RANK), row),
            pl.BlockSpec((tm, QK_ROPE), row),
            pl.BlockSpec((tm, LANE), row),
        ],
        out_shape=[
            jax.ShapeDtypeStruct((T, D), BF16),
            jax.ShapeDtypeStruct((T, Q_RANK), BF16),
            jax.ShapeDtypeStruct((T, KV_RANK), F32),
            jax.ShapeDtypeStruct((T, KV_RANK), BF16),
            jax.ShapeDtypeStruct((T, QK_ROPE), F32),
            jax.ShapeDtypeStruct((T, LANE), F32),
        ],
        compiler_params=_params(("arbitrary",), 48),
        name="inproj",
    )(x2d, W['g_mix'], W['w_small_t'], W['g_q_a'], W['g_kv_a'], c2, s2)


def _qup_kernel(cq_ref, w_ref, c2_ref, s2_ref, gn_ref, gr2_ref, q_ref, *, transposed):
    cq = cq_ref[...]
    masks = _half_masks()
    for p in range(N_PAIRS):
        z = jnp.dot(cq, w_ref[:, p * 4 * LANE:(p + 1) * 4 * LANE], preferred_element_type=F32)
        rope2 = z[:, 2 * LANE:3 * LANE] * c2_ref[...] + z[:, 3 * LANE:] * s2_ref[...]
        for e, msk in enumerate(masks):
            nope = z[:, e * LANE:(e + 1) * LANE]
            rope = rope2 * msk
            ss = jnp.sum(nope * nope, axis=-1, keepdims=True) + jnp.sum(rope * rope, axis=-1, keepdims=True)
            r = lax.rsqrt(ss * (1.0 / QK_HEAD) + EPS) * (SCALE * LOG2E)
            if transposed:
                q_ref[2 * p + e, 0, :QK_NOPE, :] = (nope * r * gn_ref[...]).T.astype(BF16)
                q_ref[2 * p + e, 0, QK_NOPE:, :] = (rope * r * gr2_ref[...]).T.astype(BF16)
            else:
                q_ref[2 * p + e, :, :QK_NOPE] = (nope * r * gn_ref[...]).astype(BF16)
                q_ref[2 * p + e, :, QK_NOPE:] = (rope * r * gr2_ref[...]).astype(BF16)


def _q_up(cq, c2, s2, W, tm, transposed):
    T = cq.shape[0]
    row = lambda i: (i, 0)
    const = lambda i: (0, 0)
    if transposed:
        out_spec = pl.BlockSpec((N_HEADS, 1, QK_PAD, tm), lambda i: (0, i, 0, 0))
        out_shape = jax.ShapeDtypeStruct((N_HEADS, T // tm, QK_PAD, tm), BF16)
    else:
        out_spec = pl.BlockSpec((N_HEADS, tm, QK_PAD), lambda i: (0, i, 0))
        out_shape = jax.ShapeDtypeStruct((N_HEADS, T, QK_PAD), BF16)
    return pl.pallas_call(
        functools.partial(_qup_kernel, transposed=transposed),
        grid=(T // tm,),
        in_specs=[
            pl.BlockSpec((tm, Q_RANK), row),
            pl.BlockSpec((Q_RANK, N_PAIRS * 4 * LANE), const),
            pl.BlockSpec((tm, LANE), row),
            pl.BlockSpec((tm, LANE), row),
            pl.BlockSpec((1, LANE), const),
            pl.BlockSpec((1, LANE), const),
        ],
        out_specs=out_spec,
        out_shape=out_shape,
        compiler_params=_params(("arbitrary",), 48),
        name="q_up",
    )(cq, W['w_q'], c2, s2, W['g_q_nope'], W['g_q_rope2'])


V_ROWS = V_HEAD + 16


def _kvup_kernel(ckv_ref, kpe2_ref, w_ref, gn_ref, gr2_ref, k_ref, v_ref):
    ckv = ckv_ref[...]
    kpe2 = kpe2_ref[...]
    lo, hi = _half_masks()
    ss_pe = jnp.sum(kpe2 * kpe2 * lo, axis=-1, keepdims=True)
    tm = ckv.shape[0]
    for p in range(N_PAIRS):
        kv = jnp.dot(ckv, w_ref[p], preferred_element_type=F32)
        for e, msk in enumerate((lo, hi)):
            hd = 2 * p + e
            kn = kv[:, e * LANE:(e + 1) * LANE]
            r = lax.rsqrt((jnp.sum(kn * kn, axis=-1, keepdims=True) + ss_pe) * (1.0 / QK_HEAD) + EPS)
            k_ref[hd, :, :QK_NOPE] = (kn * r * gn_ref[...]).astype(BF16)
            k_ref[hd, :, QK_NOPE:] = (kpe2 * msk * r * gr2_ref[...]).astype(BF16)
            v_ref[hd, 0, :V_HEAD, :] = kv[:, (2 + e) * LANE:(3 + e) * LANE].T.astype(BF16)
            v_ref[hd, 0, V_HEAD:, :] = jnp.ones((V_ROWS - V_HEAD, tm), BF16)


def _kv_up(ckvb, kpe2, W, tm):
    T = ckvb.shape[0]
    row = lambda i: (i, 0)
    const = lambda i: (0, 0)
    return pl.pallas_call(
        _kvup_kernel,
        grid=(T // tm,),
        in_specs=[
            pl.BlockSpec((tm, KV_RANK), row),
            pl.BlockSpec((tm, LANE), row),
            pl.BlockSpec((N_PAIRS, KV_RANK, 4 * LANE), lambda i: (0, 0, 0)),
            pl.BlockSpec((1, LANE), const),
            pl.BlockSpec((1, LANE), const),
        ],
        out_specs=[
            pl.BlockSpec((N_HEADS, tm, QK_PAD), lambda i: (0, i, 0)),
            pl.BlockSpec((N_HEADS, 1, V_ROWS, tm), lambda i: (0, i, 0, 0)),
        ],
        out_shape=[
            jax.ShapeDtypeStruct((N_HEADS, T, QK_PAD), BF16),
            jax.ShapeDtypeStruct((N_HEADS, T // tm, V_ROWS, tm), BF16),
        ],
        compiler_params=_params(("arbitrary",), 48),
        name="kv_up",
    )(ckvb, kpe2, W['w_kv'], W['g_k_nope'], W['g_k_rope2'])


def _online_softmax(s, m_ref, l_ref):
    m_prev = m_ref[...]
    blocks = [s[:, c:c + LANE] for c in range(0, s.shape[1], LANE)]
    m_next = jnp.maximum(m_prev, jnp.max(functools.reduce(jnp.maximum, blocks), axis=-1, keepdims=True))
    blocks = [jnp.exp2(b - m_next) for b in blocks]
    alpha = jnp.exp2(m_prev - m_next)
    l_ref[...] = alpha * l_ref[...] + functools.reduce(jnp.add, blocks)
    m_ref[...] = m_next
    return jnp.concatenate(blocks, axis=-1).astype(BF16), alpha


N_CHAINS = 2
FLASH_UNROLLS = (8, 4, 2, 1)
FLASH_Q_PER_STEP = 8


def _flash_kernel(qt_ref, k_ref, vt_ref, o_ref, s_ref, smax_ref, m_ref, acc_ref, *, tq):
    def scores(qt, j, chain):
        start = pl.multiple_of(j * tq, tq)
        s = jnp.dot(k_ref[0, pl.ds(start, tq), :], qt, preferred_element_type=F32)
        s_ref[chain] = s
        smax_ref[chain] = jnp.max(s, axis=0, keepdims=True)

    def consume(j, chain, diagonal=False):
        s = s_ref[chain]
        if diagonal:
            kchunk = lax.broadcasted_iota(jnp.int32, (tq, tq), 0) // CHUNK
            qchunk = lax.broadcasted_iota(jnp.int32, (tq, tq), 1) // CHUNK
            s = jnp.where(kchunk <= qchunk, s, NEG_INF)
            s_max = jnp.max(s, axis=0, keepdims=True)
        else:
            s_max = smax_ref[chain]
        m_prev = m_ref[chain]
        m_next = jnp.maximum(m_prev, s_max)
        p = jnp.exp2(s - m_next).astype(BF16)
        alpha = jnp.exp2(m_prev - m_next)
        acc_ref[chain] = alpha * acc_ref[chain] + jnp.dot(vt_ref[0, j], p, preferred_element_type=F32)
        m_ref[chain] = m_next

    first = pl.program_id(1) * FLASH_Q_PER_STEP

    def q_tile(u, carry):
        n = first + u + 1
        qt = qt_ref[0, u]
        qt_next = qt_ref[0, jnp.minimum(u + 1, FLASH_Q_PER_STEP - 1)]
        m_ref[...] = jnp.full(m_ref.shape, -jnp.inf, F32)
        acc_ref[...] = jnp.zeros(acc_ref.shape, F32)

        def next_tile_scores(chain):
            scores(qt_next, chain, chain)

        def pair(jj):
            consume(2 * jj, 0)
            scores(qt, 2 * jj + 2, 0)
            consume(2 * jj + 1, 1)
            scores(qt, 2 * jj + 3, 1)

        def pairs(count):
            def body(jj, c):
                for r in range(count):
                    pair(count * jj + r)
                return c
            return body

        n_pairs = jnp.maximum(n // 2 - 1, 0)
        done = 0
        for count in FLASH_UNROLLS:
            iters = (n_pairs - done) // count
            lax.fori_loop(done // count, done // count + iters, pairs(count), 0)
            done = done + iters * count

        @pl.when(n % 2 == 0)
        def _():
            consume(n - 2, 0)
            next_tile_scores(0)
            consume(n - 1, 1, diagonal=True)
            next_tile_scores(1)

        @pl.when(jnp.logical_and(n % 2 == 1, n >= 3))
        def _():
            consume(n - 3, 0)
            scores(qt, n - 1, 0)
            consume(n - 2, 1)
            next_tile_scores(1)
            consume(n - 1, 0, diagonal=True)
            next_tile_scores(0)

        @pl.when(n == 1)
        def _():
            consume(0, 0, diagonal=True)
            next_tile_scores(0)
            next_tile_scores(1)

        m = jnp.maximum(m_ref[0], m_ref[1])
        acc = jnp.exp2(m_ref[0] - m) * acc_ref[0] + jnp.exp2(m_ref[1] - m) * acc_ref[1]
        out_t = acc[:V_HEAD] / acc[V_HEAD:V_HEAD + 1]
        o_ref[pl.ds(pl.multiple_of(u * tq, tq), tq), :] = out_t.T.astype(BF16)
        return carry

    scores(qt_ref[0, 0], 0, 0)

    @pl.when(first >= 1)
    def _():
        scores(qt_ref[0, 0], 1, 1)

    lax.fori_loop(0, FLASH_Q_PER_STEP, q_tile, 0)


def _flash_prompt(qt, k, vt):
    H, n_q, _, tq = qt.shape
    S = k.shape[1]
    per_step = FLASH_Q_PER_STEP
    return pl.pallas_call(
        functools.partial(_flash_kernel, tq=tq),
        grid=(H, n_q // per_step),
        in_specs=[
            pl.BlockSpec((1, per_step, QK_PAD, tq), lambda h, i: (h, i, 0, 0)),
            pl.BlockSpec((1, S, QK_PAD), lambda h, i: (h, 0, 0)),
            pl.BlockSpec((1, n_q, V_ROWS, tq), lambda h, i: (h, 0, 0, 0)),
        ],
        out_specs=pl.BlockSpec((per_step * tq, V_HEAD), lambda h, i: (i, h)),
        out_shape=jax.ShapeDtypeStruct((S, H * V_HEAD), BF16),
        scratch_shapes=[pltpu.VMEM((N_CHAINS, tq, tq), F32),
                        pltpu.VMEM((N_CHAINS, 1, tq), F32),
                        pltpu.VMEM((N_CHAINS, 1, tq), F32),
                        pltpu.VMEM((N_CHAINS, V_ROWS, tq), F32)],
        compiler_params=_params(("arbitrary", "arbitrary"), 48),
        name="flash_prompt",
    )(qt, k, vt)


Q_LAT = KV_RANK + LANE


def _qabsorb_kernel(q_ref, wkt_ref, gn_ref, gr2_ref, o_ref):
    q = q_ref[0]
    qn = (q[:, :QK_NOPE].astype(F32) * gn_ref[...]).astype(BF16)
    o_ref[0, :, :KV_RANK] = jnp.dot(qn, wkt_ref[...], preferred_element_type=F32).astype(BF16)
    o_ref[0, :, KV_RANK:] = (q[:, QK_NOPE:].astype(F32) * gr2_ref[...]).astype(BF16)


def _q_absorb(q, W):
    H, T, _ = q.shape
    return pl.pallas_call(
        _qabsorb_kernel,
        grid=(H,),
        in_specs=[
            pl.BlockSpec((1, T, QK_PAD), lambda h: (h, 0, 0)),
            pl.BlockSpec((QK_NOPE, KV_RANK), lambda h: (h, 0)),
            pl.BlockSpec((1, LANE), lambda h: (0, 0)),
            pl.BlockSpec((1, LANE), lambda h: (0, 0)),
        ],
        out_specs=pl.BlockSpec((1, T, Q_LAT), lambda h: (h, 0, 0)),
        out_shape=jax.ShapeDtypeStruct((H, T, Q_LAT), BF16),
        compiler_params=_params(("arbitrary",), 32),
        name="q_absorb",
    )(q, W['w_kT'], W['g_k_nope'], W['g_k_rope2'])


def _decode_kernel(q_ref, ckvn_ref, kpetn_ref, ckvc_ref, kpetc_ref, wkt_ref, wv_ref, o_ref,
                   m_ref, l_ref, acc_ref, p_ref, alpha_ref, ckvpad_ref, kpetpad_ref, *, n_tiles, t_new):
    j = pl.program_id(1)
    rows = N_HEADS * t_new
    qa = q_ref[...].reshape(rows, Q_LAT)

    def attend(ckv, kpet2, n_valid):
        n = ckv.shape[0]
        knt = lax.dot_general(wkt_ref[...], ckv, _NT, preferred_element_type=F32)
        t = (lax.dot_general(qa[:, :KV_RANK], ckv, _NT, preferred_element_type=F32)
             + jnp.dot(qa[:, KV_RANK:], kpet2.astype(BF16), preferred_element_type=F32))
        kpet = kpet2[:QK_ROPE]
        ss_pe = jnp.sum(kpet * kpet, axis=0, keepdims=True)
        valid = lax.broadcasted_iota(jnp.int32, (1, n), 1) < n_valid
        for hd in range(N_HEADS):
            kn = knt[hd * QK_NOPE:(hd + 1) * QK_NOPE]
            r = lax.rsqrt((jnp.sum(kn * kn, axis=0, keepdims=True) + ss_pe) * (1.0 / QK_HEAD) + EPS)
            s = t[hd * t_new:(hd + 1) * t_new] * r
            if n_valid < n:
                s = jnp.where(valid, s, NEG_INF)
            p, alpha = _online_softmax(s, m_ref.at[hd], l_ref.at[hd])
            p_ref[hd * t_new:(hd + 1) * t_new, :n] = p
            alpha_ref[hd * t_new:(hd + 1) * t_new, :] = alpha
        pv = jnp.dot(p_ref[:, :n], ckv, preferred_element_type=F32)
        alpha = alpha_ref[...]
        acc_ref[...] = jnp.concatenate([alpha] * (KV_RANK // LANE), axis=1) * acc_ref[...] + pv

    @pl.when(j == 0)
    def _():
        m_ref[...] = jnp.full(m_ref.shape, -jnp.inf, F32)
        l_ref[...] = jnp.zeros(l_ref.shape, F32)
        acc_ref[...] = jnp.zeros(acc_ref.shape, F32)
        ckvpad_ref[...] = jnp.zeros(ckvpad_ref.shape, BF16)
        ckvpad_ref[:t_new, :] = ckvn_ref[...]
        kpetpad_ref[...] = jnp.zeros(kpetpad_ref.shape, F32)
        kpetpad_ref[:, :t_new] = kpetn_ref[0]
        attend(ckvpad_ref[...], kpetpad_ref[...], t_new)

    tk = ckvc_ref.shape[1]
    attend(ckvc_ref[0].astype(BF16), kpetc_ref[0], tk)

    @pl.when(j == n_tiles - 1)
    def _():
        for hd in range(N_HEADS):
            l = jnp.sum(l_ref[hd], axis=-1, keepdims=True)
            lat = (acc_ref[hd * t_new:(hd + 1) * t_new, :] / l).astype(BF16)
            o_ref[:, hd * V_HEAD:(hd + 1) * V_HEAD] = jnp.dot(
                lat, wv_ref[hd], preferred_element_type=F32).astype(BF16)


def _decode_attn(qa, ckvb_new, kpet2_new, cache_ckv, cache_kpet2, W, tk):
    B, P, _ = cache_ckv.shape
    T = ckvb_new.shape[0]
    t_new = T // B
    n_tiles = P // tk
    rows = N_HEADS * t_new
    return pl.pallas_call(
        functools.partial(_decode_kernel, n_tiles=n_tiles, t_new=t_new),
        grid=(B, n_tiles),
        in_specs=[
            pl.BlockSpec((N_HEADS, t_new, Q_LAT), lambda b, j: (0, b, 0)),
            pl.BlockSpec((t_new, KV_RANK), lambda b, j: (b, 0)),
            pl.BlockSpec((1, LANE, t_new), lambda b, j: (b, 0, 0)),
            pl.BlockSpec((1, tk, KV_RANK), lambda b, j: (b, j, 0)),
            pl.BlockSpec((1, LANE, tk), lambda b, j: (b, 0, j)),
            pl.BlockSpec((N_HEADS * QK_NOPE, KV_RANK), lambda b, j: (0, 0)),
            pl.BlockSpec((N_HEADS, KV_RANK, V_HEAD), lambda b, j: (0, 0, 0)),
        ],
        out_specs=pl.BlockSpec((t_new, N_HEADS * V_HEAD), lambda b, j: (b, 0)),
        out_shape=jax.ShapeDtypeStruct((T, N_HEADS * V_HEAD), BF16),
        scratch_shapes=[
            pltpu.VMEM((N_HEADS, t_new, LANE), F32),
            pltpu.VMEM((N_HEADS, t_new, LANE), F32),
            pltpu.VMEM((rows, KV_RANK), F32),
            pltpu.VMEM((rows, tk), BF16),
            pltpu.VMEM((rows, LANE), F32),
            pltpu.VMEM((LANE, KV_RANK), BF16),
            pltpu.VMEM((LANE, LANE), F32),
        ],
        compiler_params=_params(("arbitrary", "arbitrary"), 56),
        name="decode_attn",
    )(qa, ckvb_new, kpet2_new, cache_ckv, cache_kpet2, W['w_kT'], W['w_v'])


CONV_ROWS = 64
CONV_COLS = 256
SUBLANE = 8


def _conv_kernel(h_ref, past_ref, wa_ref, wg_ref, ba_ref, bg_ref, wdw_ref, bdw_ref, gln_ref, bln_ref,
                 wpw_ref, bpw_ref, yb_ref, state_ref, buf_ref, shift_ref, y_ref, *, tm, n_t):
    t = pl.program_id(1)
    hist = CONV_WIDTH - 1
    off = PAST_PAD - hist

    @pl.when(t == 0)
    def _():
        buf_ref[0:PAST_PAD, :] = jnp.zeros((PAST_PAD, CONV_CH), F32)
        buf_ref[off:PAST_PAD, :] = past_ref[0]

    @pl.when(t > 0)
    def _():
        buf_ref[0:PAST_PAD, :] = buf_ref[tm:tm + PAST_PAD, :]

    h = h_ref[0]
    a = jnp.dot(h, wa_ref[...], preferred_element_type=F32) + ba_ref[...]
    g = jnp.dot(h, wg_ref[...], preferred_element_type=F32) + bg_ref[...]
    buf_ref[PAST_PAD:PAST_PAD + tm, :] = a * jax.nn.sigmoid(g)

    span = shift_ref.shape[1]
    for r in range(1, SUBLANE):
        shift_ref[r - 1] = buf_ref[r:r + span, :]
    rows = min(CONV_ROWS, tm)
    for r0 in range(0, tm, rows):
        for c0 in range(0, CONV_CH, CONV_COLS):
            cols = slice(c0, c0 + CONV_COLS)
            acc = jnp.zeros((rows, CONV_COLS), F32)
            for q in range(off, PAST_PAD + 1):
                r, base = q % SUBLANE, r0 + q - q % SUBLANE
                src = buf_ref if r == 0 else shift_ref.at[r - 1]
                acc = acc + wdw_ref[q - off:q - off + 1, cols] * src[base:base + rows, cols]
            y_ref[r0:r0 + rows, cols] = acc

    y = y_ref[...] + bdw_ref[...]
    yc = y - jnp.mean(y, axis=-1, keepdims=True)
    y = yc * lax.rsqrt(jnp.mean(yc * yc, axis=-1, keepdims=True) + EPS) * gln_ref[...] + bln_ref[...]
    y = y * jax.nn.sigmoid(y)
    yb = jnp.dot(y.astype(BF16), wpw_ref[...], preferred_element_type=F32) + bpw_ref[...]
    yb_ref[0] = yb.astype(BF16)

    @pl.when(t == n_t - 1)
    def _():
        state_ref[0] = buf_ref[tm + off:tm + PAST_PAD, :]


def _conv_branch(h3d, past, W, tm):
    B, S, D = h3d.shape
    n_t = S // tm
    hist = CONV_WIDTH - 1
    const = lambda b, t: (0, 0)
    return pl.pallas_call(
        functools.partial(_conv_kernel, tm=tm, n_t=n_t),
        grid=(B, n_t),
        in_specs=[
            pl.BlockSpec((1, tm, D), lambda b, t: (b, t, 0)),
            pl.BlockSpec((1, hist, CONV_CH), lambda b, t: (b, 0, 0)),
            pl.BlockSpec((D, CONV_CH), const),
            pl.BlockSpec((D, CONV_CH), const),
            pl.BlockSpec((1, CONV_CH), const),
            pl.BlockSpec((1, CONV_CH), const),
            pl.BlockSpec((CONV_WIDTH, CONV_CH), const),
            pl.BlockSpec((1, CONV_CH), const),
            pl.BlockSpec((1, CONV_CH), const),
            pl.BlockSpec((1, CONV_CH), const),
            pl.BlockSpec((CONV_CH, D), const),
            pl.BlockSpec((1, D), const),
        ],
        out_specs=[
            pl.BlockSpec((1, tm, D), lambda b, t: (b, t, 0)),
            pl.BlockSpec((1, hist, CONV_CH), lambda b, t: (b, 0, 0)),
        ],
        out_shape=[
            jax.ShapeDtypeStruct((B, S, D), BF16),
            jax.ShapeDtypeStruct((B, hist, CONV_CH), F32),
        ],
        scratch_shapes=[pltpu.VMEM((PAST_PAD + tm, CONV_CH), F32),
                        pltpu.VMEM((SUBLANE - 1, PAST_PAD + tm - SUBLANE, CONV_CH), F32),
                        pltpu.VMEM((tm, CONV_CH), F32)],
        compiler_params=_params(("arbitrary", "arbitrary"), 48),
        name="conv_branch",
    )(h3d, past, W['w_glu_a'], W['w_glu_g'], W['b_glu_a'], W['b_glu_g'], W['w_dw'], W['b_dw'],
      W['g_ln'], W['b_ln'], W['w_pw'], W['b_pw'])


def _mix_kernel(h_ref, attn_ref, yb_ref, wga_ref, wgb_ref, wao_ref, bga_ref, bgb_ref, m_ref):
    h = h_ref[...]
    ga = jax.nn.sigmoid(lax.dot_general(h, wga_ref[...], _NT, preferred_element_type=F32) + bga_ref[...])
    gb = jax.nn.sigmoid(lax.dot_general(h, wgb_ref[...], _NT, preferred_element_type=F32) + bgb_ref[...])
    ya = jnp.dot(attn_ref[...], wao_ref[...], preferred_element_type=F32)
    m_ref[...] = (ga * ya + gb * yb_ref[...].astype(F32)).astype(BF16)


def _mix(h, attn, yb, W, tm, tn):
    T, D = h.shape
    row = lambda i, j: (i, 0)
    col = lambda i, j: (0, j)
    blk = lambda i, j: (i, j)
    return pl.pallas_call(
        _mix_kernel,
        grid=(T // tm, D // tn),
        in_specs=[
            pl.BlockSpec((tm, D), row),
            pl.BlockSpec((tm, D), row),
            pl.BlockSpec((tm, tn), blk),
            pl.BlockSpec((tn, D), lambda i, j: (j, 0)),
            pl.BlockSpec((tn, D), lambda i, j: (j, 0)),
            pl.BlockSpec((D, tn), col),
            pl.BlockSpec((1, tn), col),
            pl.BlockSpec((1, tn), col),
        ],
        out_specs=pl.BlockSpec((tm, tn), blk),
        out_shape=jax.ShapeDtypeStruct((T, D), BF16),
        compiler_params=_params(("arbitrary", "arbitrary"), 48),
        name="gated_mix",
    )(h, attn, yb, W['w_gate_a_t'], W['w_gate_b_t'], W['w_attn_out'], W['b_gate_a'], W['b_gate_b'])


def _outproj_kernel(x_ref, m_ref, w_ref, o_ref):
    o_ref[...] = x_ref[...] + jnp.dot(m_ref[...], w_ref[...], preferred_element_type=F32)


def _out_proj(x2d, m, W, tm):
    T, D = x2d.shape
    return pl.pallas_call(
        _outproj_kernel,
        grid=(T // tm,),
        in_specs=[
            pl.BlockSpec((tm, D), lambda i: (i, 0)),
            pl.BlockSpec((tm, D), lambda i: (i, 0)),
            pl.BlockSpec((D, D), lambda i: (0, 0)),
        ],
        out_specs=pl.BlockSpec((tm, D), lambda i: (i, 0)),
        out_shape=jax.ShapeDtypeStruct((T, D), F32),
        compiler_params=_params(("arbitrary",), 48),
        name="out_proj",
    )(x2d, m, W['w_out'])


def _ffn_kernel(x_ref, g_ref, wg_ref, wu_ref, wd_ref, o_ref, h_ref):
    @pl.when(pl.program_id(1) == 0)
    def _():
        x = x_ref[...]
        h_ref[...] = (x * _rms_scale(x, x.shape[-1]) * g_ref[...]).astype(BF16)
        o_ref[...] = x

    h = h_ref[...]
    gate = jnp.dot(h, wg_ref[...], preferred_element_type=F32)
    up = jnp.dot(h, wu_ref[...], preferred_element_type=F32)
    act = (gate * jax.nn.sigmoid(gate) * up).astype(BF16)
    o_ref[...] += jnp.dot(act, wd_ref[...], preferred_element_type=F32)


def _ffn(x2d, W, tm, tf):
    T, D = x2d.shape
    d_ff = W['w_ffn_gate'].shape[1]
    return pl.pallas_call(
        _ffn_kernel,
        grid=(T // tm, d_ff // tf),
        in_specs=[
            pl.BlockSpec((tm, D), lambda i, j: (i, 0)),
            pl.BlockSpec((1, D), lambda i, j: (0, 0)),
            pl.BlockSpec((D, tf), lambda i, j: (0, j)),
            pl.BlockSpec((D, tf), lambda i, j: (0, j)),
            pl.BlockSpec((tf, D), lambda i, j: (j, 0)),
        ],
        out_specs=pl.BlockSpec((tm, D), lambda i, j: (i, 0)),
        out_shape=jax.ShapeDtypeStruct((T, D), F32),
        scratch_shapes=[pltpu.VMEM((tm, D), BF16)],
        compiler_params=_params(("arbitrary", "arbitrary"), 60),
        name="ffn",
    )(x2d, W['g_ffn'], W['w_ffn_gate'], W['w_ffn_up'], W['w_ffn_down'])


def _rot_half_cols(w):
    half = QK_ROPE // 2
    return jnp.concatenate([-w[..., half:], w[..., :half]], axis=-1)


def _prep_weights(lw):
    (g_mix_norm, w_in, b_glu, b_gate, g_q_a, w_q_up, g_q_norm, g_kv_a, w_kv_up, g_k_norm,
     w_attn_out, w_dw, b_dw, g_conv_ln, b_conv_ln, w_conv_out, b_conv_out, w_out,
     g_ffn_norm, w_ffn_gate, w_ffn_up, w_ffn_down) = lw
    D = w_in.shape[0]
    o_kv = Q_RANK
    o_pe = o_kv + KV_RANK
    o_glu = o_pe + QK_ROPE
    o_gate = o_glu + 2 * CONV_CH
    w_t = w_in.T
    w_pe_t = w_t[o_pe:o_glu]
    half = QK_ROPE // 2
    w_pe_rot_t = jnp.concatenate([-w_pe_t[half:], w_pe_t[:half]], axis=0)
    row = lambda v: v.reshape(1, -1).astype(F32)
    W = {
        'g_mix': row(g_mix_norm),
        'w_small_t': jnp.concatenate([w_t[:o_pe], w_pe_t, w_pe_t, w_pe_rot_t, w_pe_rot_t], axis=0).astype(BF16),
        'g_q_a': row(g_q_a),
        'g_kv_a': row(g_kv_a),
        'w_glu_a': w_t[o_glu:o_glu + CONV_CH].T.astype(BF16),
        'w_glu_g': w_t[o_glu + CONV_CH:o_gate].T.astype(BF16),
        'b_glu_a': row(b_glu[:CONV_CH]),
        'b_glu_g': row(b_glu[CONV_CH:]),
        'w_gate_a_t': w_t[o_gate:o_gate + D].astype(BF16),
        'w_gate_b_t': w_t[o_gate + D:].astype(BF16),
        'b_gate_a': row(b_gate[:D]),
        'b_gate_b': row(b_gate[D:]),
        'w_attn_out': w_attn_out.astype(BF16),
        'w_dw': w_dw.astype(F32),
        'b_dw': row(b_dw),
        'g_ln': row(g_conv_ln),
        'b_ln': row(b_conv_ln),
        'w_pw': w_conv_out.astype(BF16),
        'b_pw': row(b_conv_out),
        'w_out': w_out.astype(BF16),
        'g_ffn': row(g_ffn_norm),
        'w_ffn_gate': w_ffn_gate.astype(BF16),
        'w_ffn_up': w_ffn_up.astype(BF16),
        'w_ffn_down': w_ffn_down.astype(BF16),
    }
    wq = w_q_up.reshape(Q_RANK, N_PAIRS, 2, QK_HEAD)
    wq_nope = wq[..., :QK_NOPE].reshape(Q_RANK, N_PAIRS, 2 * QK_NOPE)
    wq_rope = wq[..., QK_NOPE:]
    W['w_q'] = jnp.concatenate(
        [wq_nope, wq_rope.reshape(Q_RANK, N_PAIRS, 2 * QK_ROPE),
         _rot_half_cols(wq_rope).reshape(Q_RANK, N_PAIRS, 2 * QK_ROPE)], axis=-1
    ).reshape(Q_RANK, N_PAIRS * 4 * LANE).astype(BF16)
    wkv = w_kv_up.reshape(KV_RANK, N_PAIRS, 2, QK_NOPE + V_HEAD)
    W['w_kv'] = jnp.concatenate(
        [wkv[..., :QK_NOPE].reshape(KV_RANK, N_PAIRS, 2 * QK_NOPE),
         wkv[..., QK_NOPE:].reshape(KV_RANK, N_PAIRS, 2 * V_HEAD)], axis=-1
    ).transpose(1, 0, 2).astype(BF16)
    W['w_kT'] = wkv[..., :QK_NOPE].reshape(KV_RANK, N_HEADS * QK_NOPE).T.astype(BF16)
    W['w_v'] = wkv[..., QK_NOPE:].reshape(KV_RANK, N_HEADS, V_HEAD).transpose(1, 0, 2).astype(BF16)
    dup = lambda v: jnp.concatenate([v, v]).reshape(1, LANE).astype(F32)
    W['g_q_nope'] = row(g_q_norm[:QK_NOPE])
    W['g_q_rope2'] = dup(g_q_norm[QK_NOPE:])
    W['g_k_nope'] = row(g_k_norm[:QK_NOPE])
    W['g_k_rope2'] = dup(g_k_norm[QK_NOPE:])
    return W


def _rope_tables(pos):
    inv_freq = 1.0 / (ROPE_THETA ** (jnp.arange(0, QK_ROPE, 2, dtype=F32) / QK_ROPE))
    ang = pos.astype(F32)[:, None] * inv_freq[None, :]
    return jnp.tile(jnp.cos(ang), (1, 4)), jnp.tile(jnp.sin(ang), (1, 4))


ROW_TILE = 512
WIDE_ROW_TILE = 1024
COL_TILE = 512
FFN_TILE = 512
CONV_ROW_TILE = 256
DECODE_KEY_TILE = 1024


def _layer(x, pos, past_ckv, past_kpe, past_conv, W):
    B, S, D = x.shape
    T = B * S
    x2d = x.reshape(T, D)
    c2, s2 = _rope_tables(pos)
    if B > 1:
        c2, s2 = jnp.tile(c2, (B, 1)), jnp.tile(s2, (B, 1))
    tm = min(T, ROW_TILE)
    tm_wide = min(T, WIDE_ROW_TILE)
    h, cq, ckv, ckvb, kpe, kpe2 = _inproj(x2d, c2, s2, W, tm)
    if past_ckv is None:
        qt = _q_up(cq, c2, s2, W, tm, transposed=True)
        k, vt = _kv_up(ckvb, kpe2, W, tm)
        attn = _flash_prompt(qt, k, vt)
        past_conv = jnp.zeros((B, CONV_WIDTH - 1, CONV_CH), x.dtype)
    else:
        q = _q_up(cq, c2, s2, W, tm, transposed=False)
        cache_kpet = jnp.swapaxes(past_kpe, 1, 2)
        cache_kpet2 = jnp.concatenate([cache_kpet, cache_kpet], axis=1)
        kpet2_new = jnp.swapaxes(kpe2.reshape(B, S, LANE), 1, 2)
        attn = _decode_attn(_q_absorb(q, W), ckvb, kpet2_new, past_ckv, cache_kpet2, W, DECODE_KEY_TILE)
    yb, conv_state = _conv_branch(h.reshape(B, S, D), past_conv, W, min(S, CONV_ROW_TILE))
    m = _mix(h, attn, yb.reshape(T, D), W, tm_wide, COL_TILE)
    x1 = _out_proj(x2d, m, W, tm)
    y = _ffn(x1, W, tm_wide, FFN_TILE)
    return (y.reshape(B, S, D), ckv.reshape(B, S, KV_RANK), kpe.reshape(B, S, QK_ROPE), conv_state)


def kernel(x_prompt, x_sample, cache_ckv, cache_kpe, state_conv, g_mix_norm, w_in, b_glu, b_gate, g_q_a,
           w_q_up, g_q_norm, g_kv_a, w_kv_up, g_k_norm, w_attn_out, w_dw, b_dw, g_conv_ln, b_conv_ln,
           w_conv_out, b_conv_out, w_out, g_ffn_norm, w_ffn_gate, w_ffn_up, w_ffn_down):
    weights = (g_mix_norm, w_in, b_glu, b_gate, g_q_a, w_q_up, g_q_norm, g_kv_a, w_kv_up, g_k_norm,
               w_attn_out, w_dw, b_dw, g_conv_ln, b_conv_ln, w_conv_out, b_conv_out, w_out,
               g_ffn_norm, w_ffn_gate, w_ffn_up, w_ffn_down)
    depth = w_in.shape[0]
    pos_prompt = jnp.arange(x_prompt.shape[1])
    pos_sample = cache_ckv.shape[2] + jnp.arange(x_sample.shape[1])
    y_prompt, y_sample = x_prompt, x_sample
    outs = [[] for _ in range(6)]
    for l in range(depth):
        W = _prep_weights(tuple(w[l] for w in weights))
        y_prompt, ckv, kpe, conv = _layer(y_prompt, pos_prompt, None, None, None, W)
        outs[0].append(ckv); outs[1].append(kpe); outs[2].append(conv)
        y_sample, ckv, kpe, conv = _layer(y_sample, pos_sample, cache_ckv[l], cache_kpe[l], state_conv[l], W)
        outs[3].append(ckv); outs[4].append(kpe); outs[5].append(conv)
    return (y_prompt, y_sample) + tuple(jnp.stack(o) for o in outs)
```

```python
import functools
import math

import jax
import jax.numpy as jnp
from jax import lax
from jax.experimental import pallas as pl
from jax.experimental.pallas import tpu as pltpu

F32 = jnp.float32
BF16 = jnp.bfloat16

CHUNK = 64
N_HEADS = 16
Q_RANK = 512
KV_RANK = 512
QK_NOPE = 128
QK_ROPE = 64
QK_HEAD = QK_NOPE + QK_ROPE
V_HEAD = 128
CONV_CH = 1024
CONV_WIDTH = 31
ROPE_THETA = 10000.0
EPS = 1e-6
NEG_INF = -1e30
SCALE = QK_HEAD ** -0.5
LOG2E = math.log2(math.e)

LANE = 128
QK_PAD = 2 * LANE
N_PAIRS = N_HEADS // 2
PAST_PAD = 32
MIB = 1024 * 1024
_NT = (((1,), (1,)), ((), ()))


def _params(semantics, vmem_mib):
    return pltpu.CompilerParams(dimension_semantics=semantics, vmem_limit_bytes=vmem_mib * MIB)


def _rms_scale(v, n):
    return lax.rsqrt(jnp.sum(v * v, axis=-1, keepdims=True) * (1.0 / n) + EPS)


def _half_masks():
    lane = lax.broadcasted_iota(jnp.int32, (1, LANE), 1)
    lo = (lane < QK_ROPE).astype(F32)
    return lo, 1.0 - lo


def _inproj_kernel(x_ref, g_ref, w_ref, gq_ref, gkv_ref, c2_ref, s2_ref,
                   h_ref, cq_ref, ckv_ref, ckvb_ref, kpe_ref, kpe2_ref):
    x = x_ref[...]
    h = (x * _rms_scale(x, x.shape[-1]) * g_ref[...]).astype(BF16)
    h_ref[...] = h
    z = lax.dot_general(h, w_ref[...], _NT, preferred_element_type=F32)
    cq = z[:, :Q_RANK]
    cq_ref[...] = (cq * _rms_scale(cq, Q_RANK) * gq_ref[...]).astype(BF16)
    ckv = z[:, Q_RANK:Q_RANK + KV_RANK]
    ckv = ckv * _rms_scale(ckv, KV_RANK) * gkv_ref[...]
    ckv_ref[...] = ckv
    ckvb_ref[...] = ckv.astype(BF16)
    base = Q_RANK + KV_RANK
    kpe2 = z[:, base:base + LANE] * c2_ref[...] + z[:, base + LANE:base + 2 * LANE] * s2_ref[...]
    kpe2_ref[...] = kpe2
    kpe_ref[...] = kpe2[:, :QK_ROPE]


def _inproj(x2d, c2, s2, W, tm):
    T, D = x2d.shape
    n_in = W['w_small_t'].shape[0]
    row = lambda i: (i, 0)
    const = lambda i: (0, 0)
    return pl.pallas_call(
        _inproj_kernel,
        grid=(T // tm,),
        in_specs=[
            pl.BlockSpec((tm, D), row),
            pl.BlockSpec((1, D), const),
            pl.BlockSpec((n_in, D), const),
            pl.BlockSpec((1, Q_RANK), const),
            pl.BlockSpec((1, KV_RANK), const),
            pl.BlockSpec((tm, LANE), row),
            pl.BlockSpec((tm, LANE), row),
        ],
        out_specs=[
            pl.BlockSpec((tm, D), row),
            pl.BlockSpec((tm, Q_RANK), row),
            pl.BlockSpec((tm, KV_RANK), row),
            pl.BlockSpec((tm, KV_RANK), row),
            pl.BlockSpec((tm, QK_ROPE), row),
            pl.BlockSpec((tm, LANE), row),
        ],
        out_shape=[
            jax.ShapeDtypeStruct((T, D), BF16),
            jax.ShapeDtypeStruct((T, Q_RANK), BF16),
            jax.ShapeDtypeStruct((T, KV_RANK), F32),
            jax.ShapeDtypeStruct((T, KV_RANK), BF16),
            jax.ShapeDtypeStruct((T, QK_ROPE), F32),
            jax.ShapeDtypeStruct((T, LANE), F32),
        ],
        compiler_params=_params(("arbitrary",), 48),
        name="inproj",
    )(x2d, W['g_mix'], W['w_small_t'], W['g_q_a'], W['g_kv_a'], c2, s2)


def _qup_kernel(cq_ref, w_ref, c2_ref, s2_ref, gn_ref, gr2_ref, q_ref, *, transposed):
    cq = cq_ref[...]
    masks = _half_masks()
    for p in range(N_PAIRS):
        z = jnp.dot(cq, w_ref[:, p * 4 * LANE:(p + 1) * 4 * LANE], preferred_element_type=F32)
        rope2 = z[:, 2 * LANE:3 * LANE] * c2_ref[...] + z[:, 3 * LANE:] * s2_ref[...]
        for e, msk in enumerate(masks):
            nope = z[:, e * LANE:(e + 1) * LANE]
            rope = rope2 * msk
            ss = jnp.sum(nope * nope, axis=-1, keepdims=True) + jnp.sum(rope * rope, axis=-1, keepdims=True)
            r = lax.rsqrt(ss * (1.0 / QK_HEAD) + EPS) * (SCALE * LOG2E)
            if transposed:
                q_ref[2 * p + e, 0, :QK_NOPE, :] = (nope * r * gn_ref[...]).T.astype(BF16)
                q_ref[2 * p + e, 0, QK_NOPE:, :] = (rope * r * gr2_ref[...]).T.astype(BF16)
            else:
                q_ref[2 * p + e, :, :QK_NOPE] = (nope * r * gn_ref[...]).astype(BF16)
                q_ref[2 * p + e, :, QK_NOPE:] = (rope * r * gr2_ref[...]).astype(BF16)


def _q_up(cq, c2, s2, W, tm, transposed):
    T = cq.shape[0]
    row = lambda i: (i, 0)
    const = lambda i: (0, 0)
    if transposed:
        out_spec = pl.BlockSpec((N_HEADS, 1, QK_PAD, tm), lambda i: (0, i, 0, 0))
        out_shape = jax.ShapeDtypeStruct((N_HEADS, T // tm, QK_PAD, tm), BF16)
    else:
        out_spec = pl.BlockSpec((N_HEADS, tm, QK_PAD), lambda i: (0, i, 0))
        out_shape = jax.ShapeDtypeStruct((N_HEADS, T, QK_PAD), BF16)
    return pl.pallas_call(
        functools.partial(_qup_kernel, transposed=transposed),
        grid=(T // tm,),
        in_specs=[
            pl.BlockSpec((tm, Q_RANK), row),
            pl.BlockSpec((Q_RANK, N_PAIRS * 4 * LANE), const),
            pl.BlockSpec((tm, LANE), row),
            pl.BlockSpec((tm, LANE), row),
            pl.BlockSpec((1, LANE), const),
            pl.BlockSpec((1, LANE), const),
        ],
        out_specs=out_spec,
        out_shape=out_shape,
        compiler_params=_params(("arbitrary",), 48),
        name="q_up",
    )(cq, W['w_q'], c2, s2, W['g_q_nope'], W['g_q_rope2'])


V_ROWS = V_HEAD + 16


def _kvup_kernel(ckv_ref, kpe2_ref, w_ref, gn_ref, gr2_ref, k_ref, v_ref):
    ckv = ckv_ref[...]
    kpe2 = kpe2_ref[...]
    lo, hi = _half_masks()
    ss_pe = jnp.sum(kpe2 * kpe2 * lo, axis=-1, keepdims=True)
    tm = ckv.shape[0]
    for p in range(N_PAIRS):
        kv = jnp.dot(ckv, w_ref[p], preferred_element_type=F32)
        for e, msk in enumerate((lo, hi)):
            hd = 2 * p + e
            kn = kv[:, e * LANE:(e + 1) * LANE]
            r = lax.rsqrt((jnp.sum(kn * kn, axis=-1, keepdims=True) + ss_pe) * (1.0 / QK_HEAD) + EPS)
            k_ref[hd, :, :QK_NOPE] = (kn * r * gn_ref[...]).astype(BF16)
            k_ref[hd, :, QK_NOPE:] = (kpe2 * msk * r * gr2_ref[...]).astype(BF16)
            v_ref[hd, 0, :V_HEAD, :] = kv[:, (2 + e) * LANE:(3 + e) * LANE].T.astype(BF16)
            v_ref[hd, 0, V_HEAD:, :] = jnp.ones((V_ROWS - V_HEAD, tm), BF16)


def _kv_up(ckvb, kpe2, W, tm):
    T = ckvb.shape[0]
    row = lambda i: (i, 0)
    const = lambda i: (0, 0)
    return pl.pallas_call(
        _kvup_kernel,
        grid=(T // tm,),
        in_specs=[
            pl.BlockSpec((tm, KV_RANK), row),
            pl.BlockSpec((tm, LANE), row),
            pl.BlockSpec((N_PAIRS, KV_RANK, 4 * LANE), lambda i: (0, 0, 0)),
            pl.BlockSpec((1, LANE), const),
            pl.BlockSpec((1, LANE), const),
        ],
        out_specs=[
            pl.BlockSpec((N_HEADS, tm, QK_PAD), lambda i: (0, i, 0)),
            pl.BlockSpec((N_HEADS, 1, V_ROWS, tm), lambda i: (0, i, 0, 0)),
        ],
        out_shape=[
            jax.ShapeDtypeStruct((N_HEADS, T, QK_PAD), BF16),
            jax.ShapeDtypeStruct((N_HEADS, T // tm, V_ROWS, tm), BF16),
        ],
        compiler_params=_params(("arbitrary",), 48),
        name="kv_up",
    )(ckvb, kpe2, W['w_kv'], W['g_k_nope'], W['g_k_rope2'])


def _online_softmax(s, m_ref, l_ref):
    m_prev = m_ref[...]
    blocks = [s[:, c:c + LANE] for c in range(0, s.shape[1], LANE)]
    m_next = jnp.maximum(m_prev, jnp.max(functools.reduce(jnp.maximum, blocks), axis=-1, keepdims=True))
    blocks = [jnp.exp2(b - m_next) for b in blocks]
    alpha = jnp.exp2(m_prev - m_next)
    l_ref[...] = alpha * l_ref[...] + functools.reduce(jnp.add, blocks)
    m_ref[...] = m_next
    return jnp.concatenate(blocks, axis=-1).astype(BF16), alpha


N_CHAINS = 2
FLASH_UNROLLS = (8, 4, 2, 1)
FLASH_Q_PER_STEP = 8


def _flash_kernel(qt_ref, k_ref, vt_ref, o_ref, s_ref, smax_ref, m_ref, acc_ref, *, tq):
    def scores(qt, j, chain):
        start = pl.multiple_of(j * tq, tq)
        s = jnp.dot(k_ref[0, pl.ds(start, tq), :], qt, preferred_element_type=F32)
        s_ref[chain] = s
        smax_ref[chain] = jnp.max(s, axis=0, keepdims=True)

    def consume(j, chain, diagonal=False):
        s = s_ref[chain]
        if diagonal:
            kchunk = lax.broadcasted_iota(jnp.int32, (tq, tq), 0) // CHUNK
            qchunk = lax.broadcasted_iota(jnp.int32, (tq, tq), 1) // CHUNK
            s = jnp.where(kchunk <= qchunk, s, NEG_INF)
            s_max = jnp.max(s, axis=0, keepdims=True)
        else:
            s_max = smax_ref[chain]
        m_prev = m_ref[chain]
        m_next = jnp.maximum(m_prev, s_max)
        p = jnp.exp2(s - m_next).astype(BF16)
        alpha = jnp.exp2(m_prev - m_next)
        acc_ref[chain] = alpha * acc_ref[chain] + jnp.dot(vt_ref[0, j], p, preferred_element_type=F32)
        m_ref[chain] = m_next

    first = pl.program_id(1) * FLASH_Q_PER_STEP

    def q_tile(u, carry):
        n = first + u + 1
        qt = qt_ref[0, u]
        qt_next = qt_ref[0, jnp.minimum(u + 1, FLASH_Q_PER_STEP - 1)]
        m_ref[...] = jnp.full(m_ref.shape, -jnp.inf, F32)
        acc_ref[...] = jnp.zeros(acc_ref.shape, F32)

        def next_tile_scores(chain):
            scores(qt_next, chain, chain)

        def pair(jj):
            consume(2 * jj, 0)
            scores(qt, 2 * jj + 2, 0)
            consume(2 * jj + 1, 1)
            scores(qt, 2 * jj + 3, 1)

        def pairs(count):
            def body(jj, c):
                for r in range(count):
                    pair(count * jj + r)
                return c
            return body

        n_pairs = jnp.maximum(n // 2 - 1, 0)
        done = 0
        for count in FLASH_UNROLLS:
            iters = (n_pairs - done) // count
            lax.fori_loop(done // count, done // count + iters, pairs(count), 0)
            done = done + iters * count

        @pl.when(n % 2 == 0)
        def _():
            consume(n - 2, 0)
            next_tile_scores(0)
            consume(n - 1, 1, diagonal=True)
            next_tile_scores(1)

        @pl.when(jnp.logical_and(n % 2 == 1, n >= 3))
        def _():
            consume(n - 3, 0)
            scores(qt, n - 1, 0)
            consume(n - 2, 1)
            next_tile_scores(1)
            consume(n - 1, 0, diagonal=True)
            next_tile_scores(0)

        @pl.when(n == 1)
        def _():
            consume(0, 0, diagonal=True)
            next_tile_scores(0)
            next_tile_scores(1)

        m = jnp.maximum(m_ref[0], m_ref[1])
        acc = jnp.exp2(m_ref[0] - m) * acc_ref[0] + jnp.exp2(m_ref[1] - m) * acc_ref[1]
        out_t = acc[:V_HEAD] / acc[V_HEAD:V_HEAD + 1]
        o_ref[pl.ds(pl.multiple_of(u * tq, tq), tq), :] = out_t.T.astype(BF16)
        return carry

    scores(qt_ref[0, 0], 0, 0)

    @pl.when(first >= 1)
    def _():
        scores(qt_ref[0, 0], 1, 1)

    lax.fori_loop(0, FLASH_Q_PER_STEP, q_tile, 0)


def _flash_prompt(qt, k, vt):
    H, n_q, _, tq = qt.shape
    S = k.shape[1]
    per_step = FLASH_Q_PER_STEP
    return pl.pallas_call(
        functools.partial(_flash_kernel, tq=tq),
        grid=(H, n_q // per_step),
        in_specs=[
            pl.BlockSpec((1, per_step, QK_PAD, tq), lambda h, i: (h, i, 0, 0)),
            pl.BlockSpec((1, S, QK_PAD), lambda h, i: (h, 0, 0)),
            pl.BlockSpec((1, n_q, V_ROWS, tq), lambda h, i: (h, 0, 0, 0)),
        ],
        out_specs=pl.BlockSpec((per_step * tq, V_HEAD), lambda h, i: (i, h)),
        out_shape=jax.ShapeDtypeStruct((S, H * V_HEAD), BF16),
        scratch_shapes=[pltpu.VMEM((N_CHAINS, tq, tq), F32),
                        pltpu.VMEM((N_CHAINS, 1, tq), F32),
                        pltpu.VMEM((N_CHAINS, 1, tq), F32),
                        pltpu.VMEM((N_CHAINS, V_ROWS, tq), F32)],
        compiler_params=_params(("arbitrary", "arbitrary"), 48),
        name="flash_prompt",
    )(qt, k, vt)


Q_LAT = KV_RANK + LANE


def _qabsorb_kernel(q_ref, wkt_ref, gn_ref, gr2_ref, o_ref):
    q = q_ref[0]
    qn = (q[:, :QK_NOPE].astype(F32) * gn_ref[...]).astype(BF16)
    o_ref[0, :, :KV_RANK] = jnp.dot(qn, wkt_ref[...], preferred_element_type=F32).astype(BF16)
    o_ref[0, :, KV_RANK:] = (q[:, QK_NOPE:].astype(F32) * gr2_ref[...]).astype(BF16)


def _q_absorb(q, W):
    H, T, _ = q.shape
    return pl.pallas_call(
        _qabsorb_kernel,
        grid=(H,),
        in_specs=[
            pl.BlockSpec((1, T, QK_PAD), lambda h: (h, 0, 0)),
            pl.BlockSpec((QK_NOPE, KV_RANK), lambda h: (h, 0)),
            pl.BlockSpec((1, LANE), lambda h: (0, 0)),
            pl.BlockSpec((1, LANE), lambda h: (0, 0)),
        ],
        out_specs=pl.BlockSpec((1, T, Q_LAT), lambda h: (h, 0, 0)),
        out_shape=jax.ShapeDtypeStruct((H, T, Q_LAT), BF16),
        compiler_params=_params(("arbitrary",), 32),
        name="q_absorb",
    )(q, W['w_kT'], W['g_k_nope'], W['g_k_rope2'])


def _decode_kernel(q_ref, ckvn_ref, kpetn_ref, ckvc_ref, kpetc_ref, wkt_ref, wv_ref, o_ref,
                   m_ref, l_ref, acc_ref, p_ref, alpha_ref, ckvpad_ref, kpetpad_ref, *, n_tiles, t_new):
    j = pl.program_id(1)
    rows = N_HEADS * t_new
    qa = q_ref[...].reshape(rows, Q_LAT)

    def attend(ckv, kpet2, n_valid):
        n = ckv.shape[0]
        knt = lax.dot_general(wkt_ref[...], ckv, _NT, preferred_element_type=F32)
        t = (lax.dot_general(qa[:, :KV_RANK], ckv, _NT, preferred_element_type=F32)
             + jnp.dot(qa[:, KV_RANK:], kpet2.astype(BF16), preferred_element_type=F32))
        kpet = kpet2[:QK_ROPE]
        ss_pe = jnp.sum(kpet * kpet, axis=0, keepdims=True)
        valid = lax.broadcasted_iota(jnp.int32, (1, n), 1) < n_valid
        for hd in range(N_HEADS):
            kn = knt[hd * QK_NOPE:(hd + 1) * QK_NOPE]
            r = lax.rsqrt((jnp.sum(kn * kn, axis=0, keepdims=True) + ss_pe) * (1.0 / QK_HEAD) + EPS)
            s = t[hd * t_new:(hd + 1) * t_new] * r
            if n_valid < n:
                s = jnp.where(valid, s, NEG_INF)
            p, alpha = _online_softmax(s, m_ref.at[hd], l_ref.at[hd])
            p_ref[hd * t_new:(hd + 1) * t_new, :n] = p
            alpha_ref[hd * t_new:(hd + 1) * t_new, :] = alpha
        pv = jnp.dot(p_ref[:, :n], ckv, preferred_element_type=F32)
        alpha = alpha_ref[...]
        acc_ref[...] = jnp.concatenate([alpha] * (KV_RANK // LANE), axis=1) * acc_ref[...] + pv

    @pl.when(j == 0)
    def _():
        m_ref[...] = jnp.full(m_ref.shape, -jnp.inf, F32)
        l_ref[...] = jnp.zeros(l_ref.shape, F32)
        acc_ref[...] = jnp.zeros(acc_ref.shape, F32)
        ckvpad_ref[...] = jnp.zeros(ckvpad_ref.shape, BF16)
        ckvpad_ref[:t_new, :] = ckvn_ref[...]
        kpetpad_ref[...] = jnp.zeros(kpetpad_ref.shape, F32)
        kpetpad_ref[:, :t_new] = kpetn_ref[0]
        attend(ckvpad_ref[...], kpetpad_ref[...], t_new)

    tk = ckvc_ref.shape[1]
    attend(ckvc_ref[0].astype(BF16), kpetc_ref[0], tk)

    @pl.when(j == n_tiles - 1)
    def _():
        for hd in range(N_HEADS):
            l = jnp.sum(l_ref[hd], axis=-1, keepdims=True)
            lat = (acc_ref[hd * t_new:(hd + 1) * t_new, :] / l).astype(BF16)
            o_ref[:, hd * V_HEAD:(hd + 1) * V_HEAD] = jnp.dot(
                lat, wv_ref[hd], preferred_element_type=F32).astype(BF16)


def _decode_attn(qa, ckvb_new, kpet2_new, cache_ckv, cache_kpet2, W, tk):
    B, P, _ = cache_ckv.shape
    T = ckvb_new.shape[0]
    t_new = T // B
    n_tiles = P // tk
    rows = N_HEADS * t_new
    return pl.pallas_call(
        functools.partial(_decode_kernel, n_tiles=n_tiles, t_new=t_new),
        grid=(B, n_tiles),
        in_specs=[
            pl.BlockSpec((N_HEADS, t_new, Q_LAT), lambda b, j: (0, b, 0)),
            pl.BlockSpec((t_new, KV_RANK), lambda b, j: (b, 0)),
            pl.BlockSpec((1, LANE, t_new), lambda b, j: (b, 0, 0)),
            pl.BlockSpec((1, tk, KV_RANK), lambda b, j: (b, j, 0)),
            pl.BlockSpec((1, LANE, tk), lambda b, j: (b, 0, j)),
            pl.BlockSpec((N_HEADS * QK_NOPE, KV_RANK), lambda b, j: (0, 0)),
            pl.BlockSpec((N_HEADS, KV_RANK, V_HEAD), lambda b, j: (0, 0, 0)),
        ],
        out_specs=pl.BlockSpec((t_new, N_HEADS * V_HEAD), lambda b, j: (b, 0)),
        out_shape=jax.ShapeDtypeStruct((T, N_HEADS * V_HEAD), BF16),
        scratch_shapes=[
            pltpu.VMEM((N_HEADS, t_new, LANE), F32),
            pltpu.VMEM((N_HEADS, t_new, LANE), F32),
            pltpu.VMEM((rows, KV_RANK), F32),
            pltpu.VMEM((rows, tk), BF16),
            pltpu.VMEM((rows, LANE), F32),
            pltpu.VMEM((LANE, KV_RANK), BF16),
            pltpu.VMEM((LANE, LANE), F32),
        ],
        compiler_params=_params(("arbitrary", "arbitrary"), 56),
        name="decode_attn",
    )(qa, ckvb_new, kpet2_new, cache_ckv, cache_kpet2, W['w_kT'], W['w_v'])


CONV_ROWS = 64
CONV_COLS = 256
SUBLANE = 8


def _conv_kernel(h_ref, past_ref, wa_ref, wg_ref, ba_ref, bg_ref, wdw_ref, bdw_ref, gln_ref, bln_ref,
                 wpw_ref, bpw_ref, yb_ref, state_ref, buf_ref, shift_ref, y_ref, z_ref, *, tm, n_t, lagged):
    t = pl.program_id(1)
    hist = CONV_WIDTH - 1
    off = PAST_PAD - hist

    @pl.when(t == 0)
    def _():
        buf_ref[0:PAST_PAD, :] = jnp.zeros((PAST_PAD, CONV_CH), F32)
        buf_ref[off:PAST_PAD, :] = past_ref[0]
        if lagged:
            z_ref[...] = jnp.zeros(z_ref.shape, BF16)

    @pl.when(t > 0)
    def _():
        buf_ref[0:PAST_PAD, :] = buf_ref[tm:tm + PAST_PAD, :]

    h = h_ref[0]
    span = shift_ref.shape[1]
    rows = min(CONV_ROWS, tm)
    n_blocks = CONV_CH // CONV_COLS
    d_out = yb_ref.shape[-1]
    out_cols = d_out // n_blocks

    def project(ob):
        oc = slice(ob * out_cols, (ob + 1) * out_cols)
        yb = jnp.dot(z_ref[...], wpw_ref[:, oc], preferred_element_type=F32) + bpw_ref[:, oc]
        yb_ref[0, :, oc] = yb.astype(BF16)

    for cb, c0 in enumerate(range(0, CONV_CH, CONV_COLS)):
        cols = slice(c0, c0 + CONV_COLS)
        a = jnp.dot(h, wa_ref[:, cols], preferred_element_type=F32) + ba_ref[:, cols]
        g = jnp.dot(h, wg_ref[:, cols], preferred_element_type=F32) + bg_ref[:, cols]
        buf_ref[PAST_PAD:PAST_PAD + tm, cols] = a * jax.nn.sigmoid(g)
        if lagged:
            project(cb)
        for r in range(1, SUBLANE):
            shift_ref[r - 1, :, cols] = buf_ref[r:r + span, cols]
        for r0 in range(0, tm, rows):
            acc = jnp.zeros((rows, CONV_COLS), F32)
            for q in range(off, PAST_PAD + 1):
                r, base = q % SUBLANE, r0 + q - q % SUBLANE
                src = buf_ref if r == 0 else shift_ref.at[r - 1]
                acc = acc + wdw_ref[q - off:q - off + 1, cols] * src[base:base + rows, cols]
            y_ref[r0:r0 + rows, cols] = acc

    y = y_ref[...] + bdw_ref[...]
    yc = y - jnp.mean(y, axis=-1, keepdims=True)
    y = yc * lax.rsqrt(jnp.mean(yc * yc, axis=-1, keepdims=True) + EPS) * gln_ref[...] + bln_ref[...]
    z_ref[...] = (y * jax.nn.sigmoid(y)).astype(BF16)
    if not lagged:
        for ob in range(n_blocks):
            project(ob)

    @pl.when(t == n_t - 1)
    def _():
        state_ref[0] = buf_ref[tm + off:tm + PAST_PAD, :]


def _conv_branch(h3d, past, W, tm):
    B, S, D = h3d.shape
    n_t = S // tm
    hist = CONV_WIDTH - 1
    const = lambda b, t: (0, 0)
    lag = 1 if n_t >= 8 else 0
    return pl.pallas_call(
        functools.partial(_conv_kernel, tm=tm, n_t=n_t, lagged=bool(lag)),
        grid=(B, n_t + lag),
        in_specs=[
            pl.BlockSpec((1, tm, D), lambda b, t: (b, jnp.minimum(t, n_t - 1), 0)),
            pl.BlockSpec((1, hist, CONV_CH), lambda b, t: (b, 0, 0)),
            pl.BlockSpec((D, CONV_CH), const),
            pl.BlockSpec((D, CONV_CH), const),
            pl.BlockSpec((1, CONV_CH), const),
            pl.BlockSpec((1, CONV_CH), const),
            pl.BlockSpec((CONV_WIDTH, CONV_CH), const),
            pl.BlockSpec((1, CONV_CH), const),
            pl.BlockSpec((1, CONV_CH), const),
            pl.BlockSpec((1, CONV_CH), const),
            pl.BlockSpec((CONV_CH, D), const),
            pl.BlockSpec((1, D), const),
        ],
        out_specs=[
            pl.BlockSpec((1, tm, D), lambda b, t: (b, jnp.maximum(t - lag, 0), 0)),
            pl.BlockSpec((1, hist, CONV_CH), lambda b, t: (b, 0, 0)),
        ],
        out_shape=[
            jax.ShapeDtypeStruct((B, S, D), BF16),
            jax.ShapeDtypeStruct((B, hist, CONV_CH), F32),
        ],
        scratch_shapes=[pltpu.VMEM((PAST_PAD + tm, CONV_CH), F32),
                        pltpu.VMEM((SUBLANE - 1, PAST_PAD + tm - SUBLANE, CONV_CH), F32),
                        pltpu.VMEM((tm, CONV_CH), F32),
                        pltpu.VMEM((tm, CONV_CH), BF16)],
        compiler_params=_params(("arbitrary", "arbitrary"), 48),
        name="conv_branch",
    )(h3d, past, W['w_glu_a'], W['w_glu_g'], W['b_glu_a'], W['b_glu_g'], W['w_dw'], W['b_dw'],
      W['g_ln'], W['b_ln'], W['w_pw'], W['b_pw'])


def _mix_kernel(h_ref, attn_ref, yb_ref, wga_ref, wgb_ref, wao_ref, bga_ref, bgb_ref, m_ref):
    h = h_ref[...]
    ga = jax.nn.sigmoid(lax.dot_general(h, wga_ref[...], _NT, preferred_element_type=F32) + bga_ref[...])
    gb = jax.nn.sigmoid(lax.dot_general(h, wgb_ref[...], _NT, preferred_element_type=F32) + bgb_ref[...])
    ya = jnp.dot(attn_ref[...], wao_ref[...], preferred_element_type=F32)
    m_ref[...] = (ga * ya + gb * yb_ref[...].astype(F32)).astype(BF16)


def _mix(h, attn, yb, W, tm, tn):
    T, D = h.shape
    row = lambda i, j: (i, 0)
    col = lambda i, j: (0, j)
    blk = lambda i, j: (i, j)
    return pl.pallas_call(
        _mix_kernel,
        grid=(T // tm, D // tn),
        in_specs=[
            pl.BlockSpec((tm, D), row),
            pl.BlockSpec((tm, D), row),
            pl.BlockSpec((tm, tn), blk),
            pl.BlockSpec((tn, D), lambda i, j: (j, 0)),
            pl.BlockSpec((tn, D), lambda i, j: (j, 0)),
            pl.BlockSpec((D, tn), col),
            pl.BlockSpec((1, tn), col),
            pl.BlockSpec((1, tn), col),
        ],
        out_specs=pl.BlockSpec((tm, tn), blk),
        out_shape=jax.ShapeDtypeStruct((T, D), BF16),
        compiler_params=_params(("arbitrary", "arbitrary"), 48),
        name="gated_mix",
    )(h, attn, yb, W['w_gate_a_t'], W['w_gate_b_t'], W['w_attn_out'], W['b_gate_a'], W['b_gate_b'])


def _outproj_kernel(x_ref, m_ref, w_ref, o_ref):
    o_ref[...] = x_ref[...] + jnp.dot(m_ref[...], w_ref[...], preferred_element_type=F32)


def _out_proj(x2d, m, W, tm):
    T, D = x2d.shape
    return pl.pallas_call(
        _outproj_kernel,
        grid=(T // tm,),
        in_specs=[
            pl.BlockSpec((tm, D), lambda i: (i, 0)),
            pl.BlockSpec((tm, D), lambda i: (i, 0)),
            pl.BlockSpec((D, D), lambda i: (0, 0)),
        ],
        out_specs=pl.BlockSpec((tm, D), lambda i: (i, 0)),
        out_shape=jax.ShapeDtypeStruct((T, D), F32),
        compiler_params=_params(("arbitrary",), 48),
        name="out_proj",
    )(x2d, m, W['w_out'])


def _ffn_kernel(x_ref, g_ref, wg_ref, wu_ref, wd_ref, o_ref, h_ref):
    @pl.when(pl.program_id(1) == 0)
    def _():
        x = x_ref[...]
        h_ref[...] = (x * _rms_scale(x, x.shape[-1]) * g_ref[...]).astype(BF16)
        o_ref[...] = x

    h = h_ref[...]
    gate = jnp.dot(h, wg_ref[...], preferred_element_type=F32)
    up = jnp.dot(h, wu_ref[...], preferred_element_type=F32)
    act = (gate * jax.nn.sigmoid(gate) * up).astype(BF16)
    o_ref[...] += jnp.dot(act, wd_ref[...], preferred_element_type=F32)


def _ffn(x2d, W, tm, tf):
    T, D = x2d.shape
    d_ff = W['w_ffn_gate'].shape[1]
    return pl.pallas_call(
        _ffn_kernel,
        grid=(T // tm, d_ff // tf),
        in_specs=[
            pl.BlockSpec((tm, D), lambda i, j: (i, 0)),
            pl.BlockSpec((1, D), lambda i, j: (0, 0)),
            pl.BlockSpec((D, tf), lambda i, j: (0, j)),
            pl.BlockSpec((D, tf), lambda i, j: (0, j)),
            pl.BlockSpec((tf, D), lambda i, j: (j, 0)),
        ],
        out_specs=pl.BlockSpec((tm, D), lambda i, j: (i, 0)),
        out_shape=jax.ShapeDtypeStruct((T, D), F32),
        scratch_shapes=[pltpu.VMEM((tm, D), BF16)],
        compiler_params=_params(("arbitrary", "arbitrary"), 60),
        name="ffn",
    )(x2d, W['g_ffn'], W['w_ffn_gate'], W['w_ffn_up'], W['w_ffn_down'])


def _rot_half_cols(w):
    half = QK_ROPE // 2
    return jnp.concatenate([-w[..., half:], w[..., :half]], axis=-1)


def _prep_weights(lw):
    (g_mix_norm, w_in, b_glu, b_gate, g_q_a, w_q_up, g_q_norm, g_kv_a, w_kv_up, g_k_norm,
     w_attn_out, w_dw, b_dw, g_conv_ln, b_conv_ln, w_conv_out, b_conv_out, w_out,
     g_ffn_norm, w_ffn_gate, w_ffn_up, w_ffn_down) = lw
    D = w_in.shape[0]
    o_kv = Q_RANK
    o_pe = o_kv + KV_RANK
    o_glu = o_pe + QK_ROPE
    o_gate = o_glu + 2 * CONV_CH
    w_t = w_in.T
    w_pe_t = w_t[o_pe:o_glu]
    half = QK_ROPE // 2
    w_pe_rot_t = jnp.concatenate([-w_pe_t[half:], w_pe_t[:half]], axis=0)
    row = lambda v: v.reshape(1, -1).astype(F32)
    W = {
        'g_mix': row(g_mix_norm),
        'w_small_t': jnp.concatenate([w_t[:o_pe], w_pe_t, w_pe_t, w_pe_rot_t, w_pe_rot_t], axis=0).astype(BF16),
        'g_q_a': row(g_q_a),
        'g_kv_a': row(g_kv_a),
        'w_glu_a': w_t[o_glu:o_glu + CONV_CH].T.astype(BF16),
        'w_glu_g': w_t[o_glu + CONV_CH:o_gate].T.astype(BF16),
        'b_glu_a': row(b_glu[:CONV_CH]),
        'b_glu_g': row(b_glu[CONV_CH:]),
        'w_gate_a_t': w_t[o_gate:o_gate + D].astype(BF16),
        'w_gate_b_t': w_t[o_gate + D:].astype(BF16),
        'b_gate_a': row(b_gate[:D]),
        'b_gate_b': row(b_gate[D:]),
        'w_attn_out': w_attn_out.astype(BF16),
        'w_dw': w_dw.astype(F32),
        'b_dw': row(b_dw),
        'g_ln': row(g_conv_ln),
        'b_ln': row(b_conv_ln),
        'w_pw': w_conv_out.astype(BF16),
        'b_pw': row(b_conv_out),
        'w_out': w_out.astype(BF16),
        'g_ffn': row(g_ffn_norm),
        'w_ffn_gate': w_ffn_gate.astype(BF16),
        'w_ffn_up': w_ffn_up.astype(BF16),
        'w_ffn_down': w_ffn_down.astype(BF16),
    }
    wq = w_q_up.reshape(Q_RANK, N_PAIRS, 2, QK_HEAD)
    wq_nope = wq[..., :QK_NOPE].reshape(Q_RANK, N_PAIRS, 2 * QK_NOPE)
    wq_rope = wq[..., QK_NOPE:]
    W['w_q'] = jnp.concatenate(
        [wq_nope, wq_rope.reshape(Q_RANK, N_PAIRS, 2 * QK_ROPE),
         _rot_half_cols(wq_rope).reshape(Q_RANK, N_PAIRS, 2 * QK_ROPE)], axis=-1
    ).reshape(Q_RANK, N_PAIRS * 4 * LANE).astype(BF16)
    wkv = w_kv_up.reshape(KV_RANK, N_PAIRS, 2, QK_NOPE + V_HEAD)
    W['w_kv'] = jnp.concatenate(
        [wkv[..., :QK_NOPE].reshape(KV_RANK, N_PAIRS, 2 * QK_NOPE),
         wkv[..., QK_NOPE:].reshape(KV_RANK, N_PAIRS, 2 * V_HEAD)], axis=-1
    ).transpose(1, 0, 2).astype(BF16)
    W['w_kT'] = wkv[..., :QK_NOPE].reshape(KV_RANK, N_HEADS * QK_NOPE).T.astype(BF16)
    W['w_v'] = wkv[..., QK_NOPE:].reshape(KV_RANK, N_HEADS, V_HEAD).transpose(1, 0, 2).astype(BF16)
    dup = lambda v: jnp.concatenate([v, v]).reshape(1, LANE).astype(F32)
    W['g_q_nope'] = row(g_q_norm[:QK_NOPE])
    W['g_q_rope2'] = dup(g_q_norm[QK_NOPE:])
    W['g_k_nope'] = row(g_k_norm[:QK_NOPE])
    W['g_k_rope2'] = dup(g_k_norm[QK_NOPE:])
    return W


def _rope_tables(pos):
    inv_freq = 1.0 / (ROPE_THETA ** (jnp.arange(0, QK_ROPE, 2, dtype=F32) / QK_ROPE))
    ang = pos.astype(F32)[:, None] * inv_freq[None, :]
    return jnp.tile(jnp.cos(ang), (1, 4)), jnp.tile(jnp.sin(ang), (1, 4))


ROW_TILE = 512
WIDE_ROW_TILE = 1024
COL_TILE = 512
FFN_TILE = 512
CONV_ROW_TILE = 256
DECODE_KEY_TILE = 1024


def _layer(x, pos, past_ckv, past_kpe, past_conv, W):
    B, S, D = x.shape
    T = B * S
    x2d = x.reshape(T, D)
    c2, s2 = _rope_tables(pos)
    if B > 1:
        c2, s2 = jnp.tile(c2, (B, 1)), jnp.tile(s2, (B, 1))
    tm = min(T, ROW_TILE)
    tm_wide = min(T, WIDE_ROW_TILE)
    h, cq, ckv, ckvb, kpe, kpe2 = _inproj(x2d, c2, s2, W, tm)
    if past_ckv is None:
        qt = _q_up(cq, c2, s2, W, tm, transposed=True)
        k, vt = _kv_up(ckvb, kpe2, W, tm)
        attn = _flash_prompt(qt, k, vt)
        past_conv = jnp.zeros((B, CONV_WIDTH - 1, CONV_CH), x.dtype)
    else:
        q = _q_up(cq, c2, s2, W, tm, transposed=False)
        cache_kpet = jnp.swapaxes(past_kpe, 1, 2)
        cache_kpet2 = jnp.concatenate([cache_kpet, cache_kpet], axis=1)
        kpet2_new = jnp.swapaxes(kpe2.reshape(B, S, LANE), 1, 2)
        attn = _decode_attn(_q_absorb(q, W), ckvb, kpet2_new, past_ckv, cache_kpet2, W, DECODE_KEY_TILE)
    yb, conv_state = _conv_branch(h.reshape(B, S, D), past_conv, W, min(S, CONV_ROW_TILE))
    m = _mix(h, attn, yb.reshape(T, D), W, tm_wide, COL_TILE)
    x1 = _out_proj(x2d, m, W, tm)
    y = _ffn(x1, W, tm_wide, FFN_TILE)
    return (y.reshape(B, S, D), ckv.reshape(B, S, KV_RANK), kpe.reshape(B, S, QK_ROPE), conv_state)


def kernel(x_prompt, x_sample, cache_ckv, cache_kpe, state_conv, g_mix_norm, w_in, b_glu, b_gate, g_q_a,
           w_q_up, g_q_norm, g_kv_a, w_kv_up, g_k_norm, w_attn_out, w_dw, b_dw, g_conv_ln, b_conv_ln,
           w_conv_out, b_conv_out, w_out, g_ffn_norm, w_ffn_gate, w_ffn_up, w_ffn_down):
    weights = (g_mix_norm, w_in, b_glu, b_gate, g_q_a, w_q_up, g_q_norm, g_kv_a, w_kv_up, g_k_norm,
               w_attn_out, w_dw, b_dw, g_conv_ln, b_conv_ln, w_conv_out, b_conv_out, w_out,
               g_ffn_norm, w_ffn_gate, w_ffn_up, w_ffn_down)
    depth = w_in.shape[0]
    pos_prompt = jnp.arange(x_prompt.shape[1])
    pos_sample = cache_ckv.shape[2] + jnp.arange(x_sample.shape[1])
    y_prompt, y_sample = x_prompt, x_sample
    outs = [[] for _ in range(6)]
    for l in range(depth):
        W = _prep_weights(tuple(w[l] for w in weights))
        y_prompt, ckv, kpe, conv = _layer(y_prompt, pos_prompt, None, None, None, W)
        outs[0].append(ckv); outs[1].append(kpe); outs[2].append(conv)
        y_sample, ckv, kpe, conv = _layer(y_sample, pos_sample, cache_ckv[l], cache_kpe[l], state_conv[l], W)
        outs[3].append(ckv); outs[4].append(kpe); outs[5].append(conv)
    return (y_prompt, y_sample) + tuple(jnp.stack(o) for o in outs)
```

```python
import functools
import math

import jax
import jax.numpy as jnp
from jax import lax
from jax.experimental import pallas as pl
from jax.experimental.pallas import tpu as pltpu

F32 = jnp.float32
BF16 = jnp.bfloat16

CHUNK = 64
N_HEADS = 16
Q_RANK = 512
KV_RANK = 512
QK_NOPE = 128
QK_ROPE = 64
QK_HEAD = QK_NOPE + QK_ROPE
V_HEAD = 128
CONV_CH = 1024
CONV_WIDTH = 31
ROPE_THETA = 10000.0
EPS = 1e-6
NEG_INF = -1e30
SCALE = QK_HEAD ** -0.5
LOG2E = math.log2(math.e)

LANE = 128
QK_PAD = 2 * LANE
N_PAIRS = N_HEADS // 2
PAST_PAD = 32
MIB = 1024 * 1024
_NT = (((1,), (1,)), ((), ()))


def _params(semantics, vmem_mib):
    return pltpu.CompilerParams(dimension_semantics=semantics, vmem_limit_bytes=vmem_mib * MIB)


def _rms_scale(v, n):
    return lax.rsqrt(jnp.sum(v * v, axis=-1, keepdims=True) * (1.0 / n) + EPS)


def _half_masks():
    lane = lax.broadcasted_iota(jnp.int32, (1, LANE), 1)
    lo = (lane < QK_ROPE).astype(F32)
    return lo, 1.0 - lo


def _inproj_kernel(x_ref, g_ref, w_ref, gq_ref, gkv_ref, c2_ref, s2_ref,
                   h_ref, cq_ref, ckv_ref, ckvb_ref, kpe_ref, kpe2_ref):
    x = x_ref[...]
    h = (x * _rms_scale(x, x.shape[-1]) * g_ref[...]).astype(BF16)
    h_ref[...] = h
    z = lax.dot_general(h, w_ref[...], _NT, preferred_element_type=F32)
    cq = z[:, :Q_RANK]
    cq_ref[...] = (cq * _rms_scale(cq, Q_RANK) * gq_ref[...]).astype(BF16)
    ckv = z[:, Q_RANK:Q_RANK + KV_RANK]
    ckv = ckv * _rms_scale(ckv, KV_RANK) * gkv_ref[...]
    ckv_ref[...] = ckv
    ckvb_ref[...] = ckv.astype(BF16)
    base = Q_RANK + KV_RANK
    kpe2 = z[:, base:base + LANE] * c2_ref[...] + z[:, base + LANE:base + 2 * LANE] * s2_ref[...]
    kpe2_ref[...] = kpe2
    kpe_ref[...] = kpe2[:, :QK_ROPE]


def _inproj(x2d, c2, s2, W, tm):
    T, D = x2d.shape
    n_in = W['w_small_t'].shape[0]
    row = lambda i: (i, 0)
    const = lambda i: (0, 0)
    return pl.pallas_call(
        _inproj_kernel,
        grid=(T // tm,),
        in_specs=[
            pl.BlockSpec((tm, D), row),
            pl.BlockSpec((1, D), const),
            pl.BlockSpec((n_in, D), const),
            pl.BlockSpec((1, Q_RANK), const),
            pl.BlockSpec((1, KV_RANK), const),
            pl.BlockSpec((tm, LANE), row),
            pl.BlockSpec((tm, LANE), row),
        ],
        out_specs=[
            pl.BlockSpec((tm, D), row),
            pl.BlockSpec((tm, Q_RANK), row),
            pl.BlockSpec((tm, KV_RANK), row),
            pl.BlockSpec((tm, KV_RANK), row),
            pl.BlockSpec((tm, QK_ROPE), row),
            pl.BlockSpec((tm, LANE), row),
        ],
        out_shape=[
            jax.ShapeDtypeStruct((T, D), BF16),
            jax.ShapeDtypeStruct((T, Q_RANK), BF16),
            jax.ShapeDtypeStruct((T, KV_RANK), F32),
            jax.ShapeDtypeStruct((T, KV_RANK), BF16),
            jax.ShapeDtypeStruct((T, QK_ROPE), F32),
            jax.ShapeDtypeStruct((T, LANE), F32),
        ],
        compiler_params=_params(("arbitrary",), 48),
        name="inproj",
    )(x2d, W['g_mix'], W['w_small_t'], W['g_q_a'], W['g_kv_a'], c2, s2)


def _qup_kernel(cq_ref, w_ref, c2_ref, s2_ref, gn_ref, gr2_ref, q_ref, *, transposed):
    cq = cq_ref[...]
    masks = _half_masks()
    for p in range(N_PAIRS):
        z = jnp.dot(cq, w_ref[:, p * 4 * LANE:(p + 1) * 4 * LANE], preferred_element_type=F32)
        rope2 = z[:, 2 * LANE:3 * LANE] * c2_ref[...] + z[:, 3 * LANE:] * s2_ref[...]
        for e, msk in enumerate(masks):
            nope = z[:, e * LANE:(e + 1) * LANE]
            rope = rope2 * msk
            ss = jnp.sum(nope * nope, axis=-1, keepdims=True) + jnp.sum(rope * rope, axis=-1, keepdims=True)
            r = lax.rsqrt(ss * (1.0 / QK_HEAD) + EPS) * (SCALE * LOG2E)
            if transposed:
                q_ref[2 * p + e, 0, :QK_NOPE, :] = (nope * r * gn_ref[...]).T.astype(BF16)
                q_ref[2 * p + e, 0, QK_NOPE:, :] = (rope * r * gr2_ref[...]).T.astype(BF16)
            else:
                q_ref[2 * p + e, :, :QK_NOPE] = (nope * r * gn_ref[...]).astype(BF16)
                q_ref[2 * p + e, :, QK_NOPE:] = (rope * r * gr2_ref[...]).astype(BF16)


def _q_up(cq, c2, s2, W, tm, transposed):
    T = cq.shape[0]
    row = lambda i: (i, 0)
    const = lambda i: (0, 0)
    if transposed:
        out_spec = pl.BlockSpec((N_HEADS, 1, QK_PAD, tm), lambda i: (0, i, 0, 0))
        out_shape = jax.ShapeDtypeStruct((N_HEADS, T // tm, QK_PAD, tm), BF16)
    else:
        out_spec = pl.BlockSpec((N_HEADS, tm, QK_PAD), lambda i: (0, i, 0))
        out_shape = jax.ShapeDtypeStruct((N_HEADS, T, QK_PAD), BF16)
    return pl.pallas_call(
        functools.partial(_qup_kernel, transposed=transposed),
        grid=(T // tm,),
        in_specs=[
            pl.BlockSpec((tm, Q_RANK), row),
            pl.BlockSpec((Q_RANK, N_PAIRS * 4 * LANE), const),
            pl.BlockSpec((tm, LANE), row),
            pl.BlockSpec((tm, LANE), row),
            pl.BlockSpec((1, LANE), const),
            pl.BlockSpec((1, LANE), const),
        ],
        out_specs=out_spec,
        out_shape=out_shape,
        compiler_params=_params(("arbitrary",), 48),
        name="q_up",
    )(cq, W['w_q'], c2, s2, W['g_q_nope'], W['g_q_rope2'])


V_ROWS = V_HEAD + 16


def _kvup_kernel(ckv_ref, kpe2_ref, w_ref, gn_ref, gr2_ref, k_ref, v_ref):
    ckv = ckv_ref[...]
    kpe2 = kpe2_ref[...]
    lo, hi = _half_masks()
    ss_pe = jnp.sum(kpe2 * kpe2 * lo, axis=-1, keepdims=True)
    tm = ckv.shape[0]
    for p in range(N_PAIRS):
        kv = jnp.dot(ckv, w_ref[p], preferred_element_type=F32)
        for e, msk in enumerate((lo, hi)):
            hd = 2 * p + e
            kn = kv[:, e * LANE:(e + 1) * LANE]
            r = lax.rsqrt((jnp.sum(kn * kn, axis=-1, keepdims=True) + ss_pe) * (1.0 / QK_HEAD) + EPS)
            k_ref[hd, :, :QK_NOPE] = (kn * r * gn_ref[...]).astype(BF16)
            k_ref[hd, :, QK_NOPE:] = (kpe2 * msk * r * gr2_ref[...]).astype(BF16)
            v_ref[hd, 0, :V_HEAD, :] = kv[:, (2 + e) * LANE:(3 + e) * LANE].T.astype(BF16)
            v_ref[hd, 0, V_HEAD:, :] = jnp.ones((V_ROWS - V_HEAD, tm), BF16)


def _kv_up(ckvb, kpe2, W, tm):
    T = ckvb.shape[0]
    row = lambda i: (i, 0)
    const = lambda i: (0, 0)
    return pl.pallas_call(
        _kvup_kernel,
        grid=(T // tm,),
        in_specs=[
            pl.BlockSpec((tm, KV_RANK), row),
            pl.BlockSpec((tm, LANE), row),
            pl.BlockSpec((N_PAIRS, KV_RANK, 4 * LANE), lambda i: (0, 0, 0)),
            pl.BlockSpec((1, LANE), const),
            pl.BlockSpec((1, LANE), const),
        ],
        out_specs=[
            pl.BlockSpec((N_HEADS, tm, QK_PAD), lambda i: (0, i, 0)),
            pl.BlockSpec((N_HEADS, 1, V_ROWS, tm), lambda i: (0, i, 0, 0)),
        ],
        out_shape=[
            jax.ShapeDtypeStruct((N_HEADS, T, QK_PAD), BF16),
            jax.ShapeDtypeStruct((N_HEADS, T // tm, V_ROWS, tm), BF16),
        ],
        compiler_params=_params(("arbitrary",), 48),
        name="kv_up",
    )(ckvb, kpe2, W['w_kv'], W['g_k_nope'], W['g_k_rope2'])


def _online_softmax(s, m_ref, l_ref):
    m_prev = m_ref[...]
    blocks = [s[:, c:c + LANE] for c in range(0, s.shape[1], LANE)]
    m_next = jnp.maximum(m_prev, jnp.max(functools.reduce(jnp.maximum, blocks), axis=-1, keepdims=True))
    blocks = [jnp.exp2(b - m_next) for b in blocks]
    alpha = jnp.exp2(m_prev - m_next)
    l_ref[...] = alpha * l_ref[...] + functools.reduce(jnp.add, blocks)
    m_ref[...] = m_next
    return jnp.concatenate(blocks, axis=-1).astype(BF16), alpha


N_CHAINS = 2
FLASH_UNROLLS = (8, 4, 2, 1)
FLASH_Q_PER_STEP = 8


def _flash_kernel(qt_ref, k_ref, vt_ref, o_ref, s_ref, smax_ref, m_ref, acc_ref, *, tq):
    def scores(qt, j, chain):
        start = pl.multiple_of(j * tq, tq)
        s = jnp.dot(k_ref[0, pl.ds(start, tq), :], qt, preferred_element_type=F32)
        s_ref[chain] = s
        smax_ref[chain] = jnp.max(s, axis=0, keepdims=True)

    def consume(j, chain, diagonal=False):
        s = s_ref[chain]
        if diagonal:
            kchunk = lax.broadcasted_iota(jnp.int32, (tq, tq), 0) // CHUNK
            qchunk = lax.broadcasted_iota(jnp.int32, (tq, tq), 1) // CHUNK
            s = jnp.where(kchunk <= qchunk, s, NEG_INF)
            s_max = jnp.max(s, axis=0, keepdims=True)
        else:
            s_max = smax_ref[chain]
        m_prev = m_ref[chain]
        m_next = jnp.maximum(m_prev, s_max)
        p = jnp.exp2(s - m_next).astype(BF16)
        alpha = jnp.exp2(m_prev - m_next)
        acc_ref[chain] = alpha * acc_ref[chain] + jnp.dot(vt_ref[0, j], p, preferred_element_type=F32)
        m_ref[chain] = m_next

    first = pl.program_id(1) * FLASH_Q_PER_STEP

    def q_tile(u, carry):
        n = first + u + 1
        qt = qt_ref[0, u]
        qt_next = qt_ref[0, jnp.minimum(u + 1, FLASH_Q_PER_STEP - 1)]
        m_ref[...] = jnp.full(m_ref.shape, -jnp.inf, F32)
        acc_ref[...] = jnp.zeros(acc_ref.shape, F32)

        def next_tile_scores(chain):
            scores(qt_next, chain, chain)

        def pair(jj):
            consume(2 * jj, 0)
            scores(qt, 2 * jj + 2, 0)
            consume(2 * jj + 1, 1)
            scores(qt, 2 * jj + 3, 1)

        def pairs(count):
            def body(jj, c):
                for r in range(count):
                    pair(count * jj + r)
                return c
            return body

        n_pairs = jnp.maximum(n // 2 - 1, 0)
        done = 0
        for count in FLASH_UNROLLS:
            iters = (n_pairs - done) // count
            lax.fori_loop(done // count, done // count + iters, pairs(count), 0)
            done = done + iters * count

        @pl.when(n % 2 == 0)
        def _():
            consume(n - 2, 0)
            next_tile_scores(0)
            consume(n - 1, 1, diagonal=True)
            next_tile_scores(1)

        @pl.when(jnp.logical_and(n % 2 == 1, n >= 3))
        def _():
            consume(n - 3, 0)
            scores(qt, n - 1, 0)
            consume(n - 2, 1)
            next_tile_scores(1)
            consume(n - 1, 0, diagonal=True)
            next_tile_scores(0)

        @pl.when(n == 1)
        def _():
            consume(0, 0, diagonal=True)
            next_tile_scores(0)
            next_tile_scores(1)

        m = jnp.maximum(m_ref[0], m_ref[1])
        acc = jnp.exp2(m_ref[0] - m) * acc_ref[0] + jnp.exp2(m_ref[1] - m) * acc_ref[1]
        out_t = acc[:V_HEAD] / acc[V_HEAD:V_HEAD + 1]
        o_ref[pl.ds(pl.multiple_of(u * tq, tq), tq), :] = out_t.T.astype(BF16)
        return carry

    scores(qt_ref[0, 0], 0, 0)

    @pl.when(first >= 1)
    def _():
        scores(qt_ref[0, 0], 1, 1)

    lax.fori_loop(0, FLASH_Q_PER_STEP, q_tile, 0)


def _flash_prompt(qt, k, vt):
    H, n_q, _, tq = qt.shape
    S = k.shape[1]
    per_step = FLASH_Q_PER_STEP
    return pl.pallas_call(
        functools.partial(_flash_kernel, tq=tq),
        grid=(H, n_q // per_step),
        in_specs=[
            pl.BlockSpec((1, per_step, QK_PAD, tq), lambda h, i: (h, i, 0, 0)),
            pl.BlockSpec((1, S, QK_PAD), lambda h, i: (h, 0, 0)),
            pl.BlockSpec((1, n_q, V_ROWS, tq), lambda h, i: (h, 0, 0, 0)),
        ],
        out_specs=pl.BlockSpec((per_step * tq, V_HEAD), lambda h, i: (i, h)),
        out_shape=jax.ShapeDtypeStruct((S, H * V_HEAD), BF16),
        scratch_shapes=[pltpu.VMEM((N_CHAINS, tq, tq), F32),
                        pltpu.VMEM((N_CHAINS, 1, tq), F32),
                        pltpu.VMEM((N_CHAINS, 1, tq), F32),
                        pltpu.VMEM((N_CHAINS, V_ROWS, tq), F32)],
        compiler_params=_params(("arbitrary", "arbitrary"), 48),
        name="flash_prompt",
    )(qt, k, vt)


Q_LAT = KV_RANK + LANE
DECODE_GROUPS = 4


def _qabsorb_kernel(q_ref, wkt_ref, gn_ref, gr2_ref, o_ref):
    q = q_ref[0]
    qn = (q[:, :QK_NOPE].astype(F32) * gn_ref[...]).astype(BF16)
    o_ref[0, :, :KV_RANK] = jnp.dot(qn, wkt_ref[...], preferred_element_type=F32).astype(BF16)
    o_ref[0, :, KV_RANK:] = (q[:, QK_NOPE:].astype(F32) * gr2_ref[...]).astype(BF16)


def _q_absorb(q, W):
    H, T, _ = q.shape
    return pl.pallas_call(
        _qabsorb_kernel,
        grid=(H,),
        in_specs=[
            pl.BlockSpec((1, T, QK_PAD), lambda h: (h, 0, 0)),
            pl.BlockSpec((QK_NOPE, KV_RANK), lambda h: (h, 0)),
            pl.BlockSpec((1, LANE), lambda h: (0, 0)),
            pl.BlockSpec((1, LANE), lambda h: (0, 0)),
        ],
        out_specs=pl.BlockSpec((1, T, Q_LAT), lambda h: (h, 0, 0)),
        out_shape=jax.ShapeDtypeStruct((H, T, Q_LAT), BF16),
        compiler_params=_params(("arbitrary",), 32),
        name="q_absorb",
    )(q, W['w_kT'], W['g_k_nope'], W['g_k_rope2'])


def _decode_kernel(q_ref, ckvn_ref, kpetn_ref, ckvc_ref, kpetc_ref, wkt_ref, wv_ref, o_ref,
                   m_ref, l_ref, acc_ref, p_ref, alpha_ref, ckvpad_ref, kpetpad_ref, *, n_tiles, t_new):
    j = pl.program_id(1)
    rows = N_HEADS * t_new
    qa = q_ref[...].reshape(rows, Q_LAT)

    def attend(ckv, kpet2, n_valid):
        n = ckv.shape[0]
        t = (lax.dot_general(qa[:, :KV_RANK], ckv, _NT, preferred_element_type=F32)
             + jnp.dot(qa[:, KV_RANK:], kpet2.astype(BF16), preferred_element_type=F32))
        kpet = kpet2[:QK_ROPE]
        ss_pe = jnp.sum(kpet * kpet, axis=0, keepdims=True)
        valid = lax.broadcasted_iota(jnp.int32, (1, n), 1) < n_valid
        group = N_HEADS // DECODE_GROUPS
        for hd in range(N_HEADS):
            if hd % group == 0:
                knt = lax.dot_general(wkt_ref[hd * QK_NOPE:(hd + group) * QK_NOPE, :], ckv, _NT,
                                      preferred_element_type=F32)
            kn = knt[(hd % group) * QK_NOPE:(hd % group + 1) * QK_NOPE]
            r = lax.rsqrt((jnp.sum(kn * kn, axis=0, keepdims=True) + ss_pe) * (1.0 / QK_HEAD) + EPS)
            s = t[hd * t_new:(hd + 1) * t_new] * r
            if n_valid < n:
                s = jnp.where(valid, s, NEG_INF)
            p, alpha = _online_softmax(s, m_ref.at[hd], l_ref.at[hd])
            p_ref[hd * t_new:(hd + 1) * t_new, :n] = p
            alpha_ref[hd * t_new:(hd + 1) * t_new, :] = alpha
        pv = jnp.dot(p_ref[:, :n], ckv, preferred_element_type=F32)
        alpha = alpha_ref[...]
        acc_ref[...] = jnp.concatenate([alpha] * (KV_RANK // LANE), axis=1) * acc_ref[...] + pv

    @pl.when(j == 0)
    def _():
        m_ref[...] = jnp.full(m_ref.shape, -jnp.inf, F32)
        l_ref[...] = jnp.zeros(l_ref.shape, F32)
        acc_ref[...] = jnp.zeros(acc_ref.shape, F32)
        ckvpad_ref[...] = jnp.zeros(ckvpad_ref.shape, BF16)
        ckvpad_ref[:t_new, :] = ckvn_ref[...]
        kpetpad_ref[...] = jnp.zeros(kpetpad_ref.shape, F32)
        kpetpad_ref[:, :t_new] = kpetn_ref[0]
        attend(ckvpad_ref[...], kpetpad_ref[...], t_new)

    tk = ckvc_ref.shape[1]
    attend(ckvc_ref[0].astype(BF16), kpetc_ref[0], tk)

    @pl.when(j == n_tiles - 1)
    def _():
        for hd in range(N_HEADS):
            l = jnp.sum(l_ref[hd], axis=-1, keepdims=True)
            lat = (acc_ref[hd * t_new:(hd + 1) * t_new, :] / l).astype(BF16)
            o_ref[:, hd * V_HEAD:(hd + 1) * V_HEAD] = jnp.dot(
                lat, wv_ref[hd], preferred_element_type=F32).astype(BF16)


def _decode_attn(qa, ckvb_new, kpet2_new, cache_ckv, cache_kpet2, W, tk):
    B, P, _ = cache_ckv.shape
    T = ckvb_new.shape[0]
    t_new = T // B
    n_tiles = P // tk
    rows = N_HEADS * t_new
    return pl.pallas_call(
        functools.partial(_decode_kernel, n_tiles=n_tiles, t_new=t_new),
        grid=(B, n_tiles),
        in_specs=[
            pl.BlockSpec((N_HEADS, t_new, Q_LAT), lambda b, j: (0, b, 0)),
            pl.BlockSpec((t_new, KV_RANK), lambda b, j: (b, 0)),
            pl.BlockSpec((1, LANE, t_new), lambda b, j: (b, 0, 0)),
            pl.BlockSpec((1, tk, KV_RANK), lambda b, j: (b, j, 0)),
            pl.BlockSpec((1, LANE, tk), lambda b, j: (b, 0, j)),
            pl.BlockSpec((N_HEADS * QK_NOPE, KV_RANK), lambda b, j: (0, 0)),
            pl.BlockSpec((N_HEADS, KV_RANK, V_HEAD), lambda b, j: (0, 0, 0)),
        ],
        out_specs=pl.BlockSpec((t_new, N_HEADS * V_HEAD), lambda b, j: (b, 0)),
        out_shape=jax.ShapeDtypeStruct((T, N_HEADS * V_HEAD), BF16),
        scratch_shapes=[
            pltpu.VMEM((N_HEADS, t_new, LANE), F32),
            pltpu.VMEM((N_HEADS, t_new, LANE), F32),
            pltpu.VMEM((rows, KV_RANK), F32),
            pltpu.VMEM((rows, tk), BF16),
            pltpu.VMEM((rows, LANE), F32),
            pltpu.VMEM((LANE, KV_RANK), BF16),
            pltpu.VMEM((LANE, LANE), F32),
        ],
        compiler_params=_params(("arbitrary", "arbitrary"), 56),
        name="decode_attn",
    )(qa, ckvb_new, kpet2_new, cache_ckv, cache_kpet2, W['w_kT'], W['w_v'])


CONV_ROWS = 64
CONV_COLS = 256
SUBLANE = 8


def _conv_kernel(h_ref, past_ref, wa_ref, wg_ref, ba_ref, bg_ref, wdw_ref, bdw_ref, gln_ref, bln_ref,
                 wpw_ref, bpw_ref, yb_ref, state_ref, buf_ref, shift_ref, y_ref, z_ref, *, tm, n_t, lagged):
    t = pl.program_id(1)
    hist = CONV_WIDTH - 1
    off = PAST_PAD - hist

    @pl.when(t == 0)
    def _():
        buf_ref[0:PAST_PAD, :] = jnp.zeros((PAST_PAD, CONV_CH), F32)
        buf_ref[off:PAST_PAD, :] = past_ref[0]
        if lagged:
            z_ref[...] = jnp.zeros(z_ref.shape, BF16)

    @pl.when(t > 0)
    def _():
        buf_ref[0:PAST_PAD, :] = buf_ref[tm:tm + PAST_PAD, :]

    h = h_ref[0]
    span = shift_ref.shape[1]
    rows = min(CONV_ROWS, tm)
    n_blocks = CONV_CH // CONV_COLS
    d_out = yb_ref.shape[-1]
    out_cols = d_out // n_blocks

    def project(ob):
        oc = slice(ob * out_cols, (ob + 1) * out_cols)
        yb = jnp.dot(z_ref[...], wpw_ref[:, oc], preferred_element_type=F32) + bpw_ref[:, oc]
        yb_ref[0, :, oc] = yb.astype(BF16)

    for cb, c0 in enumerate(range(0, CONV_CH, CONV_COLS)):
        cols = slice(c0, c0 + CONV_COLS)
        a = jnp.dot(h, wa_ref[:, cols], preferred_element_type=F32) + ba_ref[:, cols]
        g = jnp.dot(h, wg_ref[:, cols], preferred_element_type=F32) + bg_ref[:, cols]
        buf_ref[PAST_PAD:PAST_PAD + tm, cols] = a * jax.nn.sigmoid(g)
        if lagged:
            project(cb)
        for r in range(1, SUBLANE):
            shift_ref[r - 1, :, cols] = buf_ref[r:r + span, cols]
        for r0 in range(0, tm, rows):
            acc = jnp.zeros((rows, CONV_COLS), F32)
            for q in range(off, PAST_PAD + 1):
                r, base = q % SUBLANE, r0 + q - q % SUBLANE
                src = buf_ref if r == 0 else shift_ref.at[r - 1]
                acc = acc + wdw_ref[q - off:q - off + 1, cols] * src[base:base + rows, cols]
            y_ref[r0:r0 + rows, cols] = acc

    y = y_ref[...] + bdw_ref[...]
    yc = y - jnp.mean(y, axis=-1, keepdims=True)
    y = yc * lax.rsqrt(jnp.mean(yc * yc, axis=-1, keepdims=True) + EPS) * gln_ref[...] + bln_ref[...]
    z_ref[...] = (y * jax.nn.sigmoid(y)).astype(BF16)
    if not lagged:
        for ob in range(n_blocks):
            project(ob)

    @pl.when(t == n_t - 1)
    def _():
        state_ref[0] = buf_ref[tm + off:tm + PAST_PAD, :]


def _conv_branch(h3d, past, W, tm):
    B, S, D = h3d.shape
    n_t = S // tm
    hist = CONV_WIDTH - 1
    const = lambda b, t: (0, 0)
    lag = 1 if n_t >= 8 else 0
    return pl.pallas_call(
        functools.partial(_conv_kernel, tm=tm, n_t=n_t, lagged=bool(lag)),
        grid=(B, n_t + lag),
        in_specs=[
            pl.BlockSpec((1, tm, D), lambda b, t: (b, jnp.minimum(t, n_t - 1), 0)),
            pl.BlockSpec((1, hist, CONV_CH), lambda b, t: (b, 0, 0)),
            pl.BlockSpec((D, CONV_CH), const),
            pl.BlockSpec((D, CONV_CH), const),
            pl.BlockSpec((1, CONV_CH), const),
            pl.BlockSpec((1, CONV_CH), const),
            pl.BlockSpec((CONV_WIDTH, CONV_CH), const),
            pl.BlockSpec((1, CONV_CH), const),
            pl.BlockSpec((1, CONV_CH), const),
            pl.BlockSpec((1, CONV_CH), const),
            pl.BlockSpec((CONV_CH, D), const),
            pl.BlockSpec((1, D), const),
        ],
        out_specs=[
            pl.BlockSpec((1, tm, D), lambda b, t: (b, jnp.maximum(t - lag, 0), 0)),
            pl.BlockSpec((1, hist, CONV_CH), lambda b, t: (b, 0, 0)),
        ],
        out_shape=[
            jax.ShapeDtypeStruct((B, S, D), BF16),
            jax.ShapeDtypeStruct((B, hist, CONV_CH), F32),
        ],
        scratch_shapes=[pltpu.VMEM((PAST_PAD + tm, CONV_CH), F32),
                        pltpu.VMEM((SUBLANE - 1, PAST_PAD + tm - SUBLANE, CONV_CH), F32),
                        pltpu.VMEM((tm, CONV_CH), F32),
                        pltpu.VMEM((tm, CONV_CH), BF16)],
        compiler_params=_params(("arbitrary", "arbitrary"), 48),
        name="conv_branch",
    )(h3d, past, W['w_glu_a'], W['w_glu_g'], W['b_glu_a'], W['b_glu_g'], W['w_dw'], W['b_dw'],
      W['g_ln'], W['b_ln'], W['w_pw'], W['b_pw'])


def _mix_kernel(h_ref, attn_ref, yb_ref, wga_ref, wgb_ref, wao_ref, bga_ref, bgb_ref, m_ref):
    h = h_ref[...]
    ga = jax.nn.sigmoid(lax.dot_general(h, wga_ref[...], _NT, preferred_element_type=F32) + bga_ref[...])
    gb = jax.nn.sigmoid(lax.dot_general(h, wgb_ref[...], _NT, preferred_element_type=F32) + bgb_ref[...])
    ya = jnp.dot(attn_ref[...], wao_ref[...], preferred_element_type=F32)
    m_ref[...] = (ga * ya + gb * yb_ref[...].astype(F32)).astype(BF16)


def _mix(h, attn, yb, W, tm, tn):
    T, D = h.shape
    row = lambda i, j: (i, 0)
    col = lambda i, j: (0, j)
    blk = lambda i, j: (i, j)
    return pl.pallas_call(
        _mix_kernel,
        grid=(T // tm, D // tn),
        in_specs=[
            pl.BlockSpec((tm, D), row),
            pl.BlockSpec((tm, D), row),
            pl.BlockSpec((tm, tn), blk),
            pl.BlockSpec((tn, D), lambda i, j: (j, 0)),
            pl.BlockSpec((tn, D), lambda i, j: (j, 0)),
            pl.BlockSpec((D, tn), col),
            pl.BlockSpec((1, tn), col),
            pl.BlockSpec((1, tn), col),
        ],
        out_specs=pl.BlockSpec((tm, tn), blk),
        out_shape=jax.ShapeDtypeStruct((T, D), BF16),
        compiler_params=_params(("arbitrary", "arbitrary"), 48),
        name="gated_mix",
    )(h, attn, yb, W['w_gate_a_t'], W['w_gate_b_t'], W['w_attn_out'], W['b_gate_a'], W['b_gate_b'])


def _outproj_kernel(x_ref, m_ref, w_ref, o_ref):
    o_ref[...] = x_ref[...] + jnp.dot(m_ref[...], w_ref[...], preferred_element_type=F32)


def _out_proj(x2d, m, W, tm):
    T, D = x2d.shape
    return pl.pallas_call(
        _outproj_kernel,
        grid=(T // tm,),
        in_specs=[
            pl.BlockSpec((tm, D), lambda i: (i, 0)),
            pl.BlockSpec((tm, D), lambda i: (i, 0)),
            pl.BlockSpec((D, D), lambda i: (0, 0)),
        ],
        out_specs=pl.BlockSpec((tm, D), lambda i: (i, 0)),
        out_shape=jax.ShapeDtypeStruct((T, D), F32),
        compiler_params=_params(("arbitrary",), 48),
        name="out_proj",
    )(x2d, m, W['w_out'])


def _ffn_kernel(x_ref, g_ref, wg_ref, wu_ref, wd_ref, o_ref, h_ref):
    @pl.when(pl.program_id(1) == 0)
    def _():
        x = x_ref[...]
        h_ref[...] = (x * _rms_scale(x, x.shape[-1]) * g_ref[...]).astype(BF16)
        o_ref[...] = x

    h = h_ref[...]
    gate = jnp.dot(h, wg_ref[...], preferred_element_type=F32)
    up = jnp.dot(h, wu_ref[...], preferred_element_type=F32)
    act = (gate * jax.nn.sigmoid(gate) * up).astype(BF16)
    o_ref[...] += jnp.dot(act, wd_ref[...], preferred_element_type=F32)


def _ffn(x2d, W, tm, tf):
    T, D = x2d.shape
    d_ff = W['w_ffn_gate'].shape[1]
    return pl.pallas_call(
        _ffn_kernel,
        grid=(T // tm, d_ff // tf),
        in_specs=[
            pl.BlockSpec((tm, D), lambda i, j: (i, 0)),
            pl.BlockSpec((1, D), lambda i, j: (0, 0)),
            pl.BlockSpec((D, tf), lambda i, j: (0, j)),
            pl.BlockSpec((D, tf), lambda i, j: (0, j)),
            pl.BlockSpec((tf, D), lambda i, j: (j, 0)),
        ],
        out_specs=pl.BlockSpec((tm, D), lambda i, j: (i, 0)),
        out_shape=jax.ShapeDtypeStruct((T, D), F32),
        scratch_shapes=[pltpu.VMEM((tm, D), BF16)],
        compiler_params=_params(("arbitrary", "arbitrary"), 60),
        name="ffn",
    )(x2d, W['g_ffn'], W['w_ffn_gate'], W['w_ffn_up'], W['w_ffn_down'])


def _rot_half_cols(w):
    half = QK_ROPE // 2
    return jnp.concatenate([-w[..., half:], w[..., :half]], axis=-1)


def _prep_weights(lw):
    (g_mix_norm, w_in, b_glu, b_gate, g_q_a, w_q_up, g_q_norm, g_kv_a, w_kv_up, g_k_norm,
     w_attn_out, w_dw, b_dw, g_conv_ln, b_conv_ln, w_conv_out, b_conv_out, w_out,
     g_ffn_norm, w_ffn_gate, w_ffn_up, w_ffn_down) = lw
    D = w_in.shape[0]
    o_kv = Q_RANK
    o_pe = o_kv + KV_RANK
    o_glu = o_pe + QK_ROPE
    o_gate = o_glu + 2 * CONV_CH
    w_t = w_in.T
    w_pe_t = w_t[o_pe:o_glu]
    half = QK_ROPE // 2
    w_pe_rot_t = jnp.concatenate([-w_pe_t[half:], w_pe_t[:half]], axis=0)
    row = lambda v: v.reshape(1, -1).astype(F32)
    W = {
        'g_mix': row(g_mix_norm),
        'w_small_t': jnp.concatenate([w_t[:o_pe], w_pe_t, w_pe_t, w_pe_rot_t, w_pe_rot_t], axis=0).astype(BF16),
        'g_q_a': row(g_q_a),
        'g_kv_a': row(g_kv_a),
        'w_glu_a': w_t[o_glu:o_glu + CONV_CH].T.astype(BF16),
        'w_glu_g': w_t[o_glu + CONV_CH:o_gate].T.astype(BF16),
        'b_glu_a': row(b_glu[:CONV_CH]),
        'b_glu_g': row(b_glu[CONV_CH:]),
        'w_gate_a_t': w_t[o_gate:o_gate + D].astype(BF16),
        'w_gate_b_t': w_t[o_gate + D:].astype(BF16),
        'b_gate_a': row(b_gate[:D]),
        'b_gate_b': row(b_gate[D:]),
        'w_attn_out': w_attn_out.astype(BF16),
        'w_dw': w_dw.astype(F32),
        'b_dw': row(b_dw),
        'g_ln': row(g_conv_ln),
        'b_ln': row(b_conv_ln),
        'w_pw': w_conv_out.astype(BF16),
        'b_pw': row(b_conv_out),
        'w_out': w_out.astype(BF16),
        'g_ffn': row(g_ffn_norm),
        'w_ffn_gate': w_ffn_gate.astype(BF16),
        'w_ffn_up': w_ffn_up.astype(BF16),
        'w_ffn_down': w_ffn_down.astype(BF16),
    }
    wq = w_q_up.reshape(Q_RANK, N_PAIRS, 2, QK_HEAD)
    wq_nope = wq[..., :QK_NOPE].reshape(Q_RANK, N_PAIRS, 2 * QK_NOPE)
    wq_rope = wq[..., QK_NOPE:]
    W['w_q'] = jnp.concatenate(
        [wq_nope, wq_rope.reshape(Q_RANK, N_PAIRS, 2 * QK_ROPE),
         _rot_half_cols(wq_rope).reshape(Q_RANK, N_PAIRS, 2 * QK_ROPE)], axis=-1
    ).reshape(Q_RANK, N_PAIRS * 4 * LANE).astype(BF16)
    wkv = w_kv_up.reshape(KV_RANK, N_PAIRS, 2, QK_NOPE + V_HEAD)
    W['w_kv'] = jnp.concatenate(
        [wkv[..., :QK_NOPE].reshape(KV_RANK, N_PAIRS, 2 * QK_NOPE),
         wkv[..., QK_NOPE:].reshape(KV_RANK, N_PAIRS, 2 * V_HEAD)], axis=-1
    ).transpose(1, 0, 2).astype(BF16)
    W['w_kT'] = wkv[..., :QK_NOPE].reshape(KV_RANK, N_HEADS * QK_NOPE).T.astype(BF16)
    W['w_v'] = wkv[..., QK_NOPE:].reshape(KV_RANK, N_HEADS, V_HEAD).transpose(1, 0, 2).astype(BF16)
    dup = lambda v: jnp.concatenate([v, v]).reshape(1, LANE).astype(F32)
    W['g_q_nope'] = row(g_q_norm[:QK_NOPE])
    W['g_q_rope2'] = dup(g_q_norm[QK_NOPE:])
    W['g_k_nope'] = row(g_k_norm[:QK_NOPE])
    W['g_k_rope2'] = dup(g_k_norm[QK_NOPE:])
    return W


def _rope_tables(pos):
    inv_freq = 1.0 / (ROPE_THETA ** (jnp.arange(0, QK_ROPE, 2, dtype=F32) / QK_ROPE))
    ang = pos.astype(F32)[:, None] * inv_freq[None, :]
    return jnp.tile(jnp.cos(ang), (1, 4)), jnp.tile(jnp.sin(ang), (1, 4))


ROW_TILE = 512
WIDE_ROW_TILE = 1024
COL_TILE = 512
FFN_TILE = 512
CONV_ROW_TILE = 256
DECODE_KEY_TILE = 1024


def _layer(x, pos, past_ckv, past_kpe, past_conv, W):
    B, S, D = x.shape
    T = B * S
    x2d = x.reshape(T, D)
    c2, s2 = _rope_tables(pos)
    if B > 1:
        c2, s2 = jnp.tile(c2, (B, 1)), jnp.tile(s2, (B, 1))
    tm = min(T, ROW_TILE)
    tm_wide = min(T, WIDE_ROW_TILE)
    h, cq, ckv, ckvb, kpe, kpe2 = _inproj(x2d, c2, s2, W, tm)
    if past_ckv is None:
        qt = _q_up(cq, c2, s2, W, tm, transposed=True)
        k, vt = _kv_up(ckvb, kpe2, W, tm)
        attn = _flash_prompt(qt, k, vt)
        past_conv = jnp.zeros((B, CONV_WIDTH - 1, CONV_CH), x.dtype)
    else:
        q = _q_up(cq, c2, s2, W, tm, transposed=False)
        cache_kpet = jnp.swapaxes(past_kpe, 1, 2)
        cache_kpet2 = jnp.concatenate([cache_kpet, cache_kpet], axis=1)
        kpet2_new = jnp.swapaxes(kpe2.reshape(B, S, LANE), 1, 2)
        attn = _decode_attn(_q_absorb(q, W), ckvb, kpet2_new, past_ckv, cache_kpet2, W, DECODE_KEY_TILE)
    yb, conv_state = _conv_branch(h.reshape(B, S, D), past_conv, W, min(S, CONV_ROW_TILE))
    m = _mix(h, attn, yb.reshape(T, D), W, tm_wide, COL_TILE)
    x1 = _out_proj(x2d, m, W, tm)
    y = _ffn(x1, W, tm_wide, FFN_TILE)
    return (y.reshape(B, S, D), ckv.reshape(B, S, KV_RANK), kpe.reshape(B, S, QK_ROPE), conv_state)


def kernel(x_prompt, x_sample, cache_ckv, cache_kpe, state_conv, g_mix_norm, w_in, b_glu, b_gate, g_q_a,
           w_q_up, g_q_norm, g_kv_a, w_kv_up, g_k_norm, w_attn_out, w_dw, b_dw, g_conv_ln, b_conv_ln,
           w_conv_out, b_conv_out, w_out, g_ffn_norm, w_ffn_gate, w_ffn_up, w_ffn_down):
    weights = (g_mix_norm, w_in, b_glu, b_gate, g_q_a, w_q_up, g_q_norm, g_kv_a, w_kv_up, g_k_norm,
               w_attn_out, w_dw, b_dw, g_conv_ln, b_conv_ln, w_conv_out, b_conv_out, w_out,
               g_ffn_norm, w_ffn_gate, w_ffn_up, w_ffn_down)
    depth = w_in.shape[0]
    pos_prompt = jnp.arange(x_prompt.shape[1])
    pos_sample = cache_ckv.shape[2] + jnp.arange(x_sample.shape[1])
    y_prompt, y_sample = x_prompt, x_sample
    outs = [[] for _ in range(6)]
    for l in range(depth):
        W = _prep_weights(tuple(w[l] for w in weights))
        y_prompt, ckv, kpe, conv = _layer(y_prompt, pos_prompt, None, None, None, W)
        outs[0].append(ckv); outs[1].append(kpe); outs[2].append(conv)
        y_sample, ckv, kpe, conv = _layer(y_sample, pos_sample, cache_ckv[l], cache_kpe[l], state_conv[l], W)
        outs[3].append(ckv); outs[4].append(kpe); outs[5].append(conv)
    return (y_prompt, y_sample) + tuple(jnp.stack(o) for o in outs)
```

```python
import functools
import math

import jax
import jax.numpy as jnp
from jax import lax
from jax.experimental import pallas as pl
from jax.experimental.pallas import tpu as pltpu

F32 = jnp.float32
BF16 = jnp.bfloat16

CHUNK = 64
N_HEADS = 16
Q_RANK = 512
KV_RANK = 512
QK_NOPE = 128
QK_ROPE = 64
QK_HEAD = QK_NOPE + QK_ROPE
V_HEAD = 128
CONV_CH = 1024
CONV_WIDTH = 31
ROPE_THETA = 10000.0
EPS = 1e-6
NEG_INF = -1e30
SCALE = QK_HEAD ** -0.5
LOG2E = math.log2(math.e)

LANE = 128
QK_PAD = 2 * LANE
N_PAIRS = N_HEADS // 2
PAST_PAD = 32
MIB = 1024 * 1024
_NT = (((1,), (1,)), ((), ()))


def _params(semantics, vmem_mib):
    return pltpu.CompilerParams(dimension_semantics=semantics, vmem_limit_bytes=vmem_mib * MIB)


def _rms_scale(v, n):
    return lax.rsqrt(jnp.sum(v * v, axis=-1, keepdims=True) * (1.0 / n) + EPS)


def _half_masks():
    lane = lax.broadcasted_iota(jnp.int32, (1, LANE), 1)
    lo = (lane < QK_ROPE).astype(F32)
    return lo, 1.0 - lo


def _inproj_kernel(x_ref, g_ref, w_ref, gq_ref, gkv_ref, c2_ref, s2_ref,
                   h_ref, cq_ref, ckv_ref, ckvb_ref, kpe_ref, kpe2_ref):
    x = x_ref[...]
    h = (x * _rms_scale(x, x.shape[-1]) * g_ref[...]).astype(BF16)
    h_ref[...] = h
    z = lax.dot_general(h, w_ref[...], _NT, preferred_element_type=F32)
    cq = z[:, :Q_RANK]
    cq_ref[...] = (cq * _rms_scale(cq, Q_RANK) * gq_ref[...]).astype(BF16)
    ckv = z[:, Q_RANK:Q_RANK + KV_RANK]
    ckv = ckv * _rms_scale(ckv, KV_RANK) * gkv_ref[...]
    ckv_ref[...] = ckv
    ckvb_ref[...] = ckv.astype(BF16)
    base = Q_RANK + KV_RANK
    kpe2 = z[:, base:base + LANE] * c2_ref[...] + z[:, base + LANE:base + 2 * LANE] * s2_ref[...]
    kpe2_ref[...] = kpe2
    kpe_ref[...] = kpe2[:, :QK_ROPE]


def _inproj(x2d, c2, s2, W, tm):
    T, D = x2d.shape
    n_in = W['w_small_t'].shape[0]
    row = lambda i: (i, 0)
    const = lambda i: (0, 0)
    return pl.pallas_call(
        _inproj_kernel,
        grid=(T // tm,),
        in_specs=[
            pl.BlockSpec((tm, D), row),
            pl.BlockSpec((1, D), const),
            pl.BlockSpec((n_in, D), const),
            pl.BlockSpec((1, Q_RANK), const),
            pl.BlockSpec((1, KV_RANK), const),
            pl.BlockSpec((tm, LANE), row),
            pl.BlockSpec((tm, LANE), row),
        ],
        out_specs=[
            pl.BlockSpec((tm, D), row),
            pl.BlockSpec((tm, Q_RANK), row),
            pl.BlockSpec((tm, KV_RANK), row),
            pl.BlockSpec((tm, KV_RANK), row),
            pl.BlockSpec((tm, QK_ROPE), row),
            pl.BlockSpec((tm, LANE), row),
        ],
        out_shape=[
            jax.ShapeDtypeStruct((T, D), BF16),
            jax.ShapeDtypeStruct((T, Q_RANK), BF16),
            jax.ShapeDtypeStruct((T, KV_RANK), F32),
            jax.ShapeDtypeStruct((T, KV_RANK), BF16),
            jax.ShapeDtypeStruct((T, QK_ROPE), F32),
            jax.ShapeDtypeStruct((T, LANE), F32),
        ],
        compiler_params=_params(("arbitrary",), 48),
        name="inproj",
    )(x2d, W['g_mix'], W['w_small_t'], W['g_q_a'], W['g_kv_a'], c2, s2)


def _qup_kernel(cq_ref, w_ref, c2_ref, s2_ref, gn_ref, gr2_ref, q_ref, *, transposed):
    cq = cq_ref[...]
    masks = _half_masks()
    for p in range(N_PAIRS):
        z = jnp.dot(cq, w_ref[:, p * 4 * LANE:(p + 1) * 4 * LANE], preferred_element_type=F32)
        rope2 = z[:, 2 * LANE:3 * LANE] * c2_ref[...] + z[:, 3 * LANE:] * s2_ref[...]
        for e, msk in enumerate(masks):
            nope = z[:, e * LANE:(e + 1) * LANE]
            rope = rope2 * msk
            ss = jnp.sum(nope * nope, axis=-1, keepdims=True) + jnp.sum(rope * rope, axis=-1, keepdims=True)
            r = lax.rsqrt(ss * (1.0 / QK_HEAD) + EPS) * (SCALE * LOG2E)
            if transposed:
                q_ref[2 * p + e, 0, :QK_NOPE, :] = (nope * r * gn_ref[...]).T.astype(BF16)
                q_ref[2 * p + e, 0, QK_NOPE:, :] = (rope * r * gr2_ref[...]).T.astype(BF16)
            else:
                q_ref[2 * p + e, :, :QK_NOPE] = (nope * r * gn_ref[...]).astype(BF16)
                q_ref[2 * p + e, :, QK_NOPE:] = (rope * r * gr2_ref[...]).astype(BF16)


def _q_up(cq, c2, s2, W, tm, transposed):
    T = cq.shape[0]
    row = lambda i: (i, 0)
    const = lambda i: (0, 0)
    if transposed:
        out_spec = pl.BlockSpec((N_HEADS, 1, QK_PAD, tm), lambda i: (0, i, 0, 0))
        out_shape = jax.ShapeDtypeStruct((N_HEADS, T // tm, QK_PAD, tm), BF16)
    else:
        out_spec = pl.BlockSpec((N_HEADS, tm, QK_PAD), lambda i: (0, i, 0))
        out_shape = jax.ShapeDtypeStruct((N_HEADS, T, QK_PAD), BF16)
    return pl.pallas_call(
        functools.partial(_qup_kernel, transposed=transposed),
        grid=(T // tm,),
        in_specs=[
            pl.BlockSpec((tm, Q_RANK), row),
            pl.BlockSpec((Q_RANK, N_PAIRS * 4 * LANE), const),
            pl.BlockSpec((tm, LANE), row),
            pl.BlockSpec((tm, LANE), row),
            pl.BlockSpec((1, LANE), const),
            pl.BlockSpec((1, LANE), const),
        ],
        out_specs=out_spec,
        out_shape=out_shape,
        compiler_params=_params(("arbitrary",), 48),
        name="q_up",
    )(cq, W['w_q'], c2, s2, W['g_q_nope'], W['g_q_rope2'])


V_ROWS = V_HEAD + 16


def _kvup_kernel(ckv_ref, kpe2_ref, w_ref, gn_ref, gr2_ref, k_ref, v_ref):
    ckv = ckv_ref[...]
    kpe2 = kpe2_ref[...]
    lo, hi = _half_masks()
    ss_pe = jnp.sum(kpe2 * kpe2 * lo, axis=-1, keepdims=True)
    tm = ckv.shape[0]
    for p in range(N_PAIRS):
        kv = jnp.dot(ckv, w_ref[p], preferred_element_type=F32)
        for e, msk in enumerate((lo, hi)):
            hd = 2 * p + e
            kn = kv[:, e * LANE:(e + 1) * LANE]
            r = lax.rsqrt((jnp.sum(kn * kn, axis=-1, keepdims=True) + ss_pe) * (1.0 / QK_HEAD) + EPS)
            k_ref[hd, :, :QK_NOPE] = (kn * r * gn_ref[...]).astype(BF16)
            k_ref[hd, :, QK_NOPE:] = (kpe2 * msk * r * gr2_ref[...]).astype(BF16)
            v_ref[hd, 0, :V_HEAD, :] = kv[:, (2 + e) * LANE:(3 + e) * LANE].T.astype(BF16)
            v_ref[hd, 0, V_HEAD:, :] = jnp.ones((V_ROWS - V_HEAD, tm), BF16)


def _kv_up(ckvb, kpe2, W, tm):
    T = ckvb.shape[0]
    row = lambda i: (i, 0)
    const = lambda i: (0, 0)
    return pl.pallas_call(
        _kvup_kernel,
        grid=(T // tm,),
        in_specs=[
            pl.BlockSpec((tm, KV_RANK), row),
            pl.BlockSpec((tm, LANE), row),
            pl.BlockSpec((N_PAIRS, KV_RANK, 4 * LANE), lambda i: (0, 0, 0)),
            pl.BlockSpec((1, LANE), const),
            pl.BlockSpec((1, LANE), const),
        ],
        out_specs=[
            pl.BlockSpec((N_HEADS, tm, QK_PAD), lambda i: (0, i, 0)),
            pl.BlockSpec((N_HEADS, 1, V_ROWS, tm), lambda i: (0, i, 0, 0)),
        ],
        out_shape=[
            jax.ShapeDtypeStruct((N_HEADS, T, QK_PAD), BF16),
            jax.ShapeDtypeStruct((N_HEADS, T // tm, V_ROWS, tm), BF16),
        ],
        compiler_params=_params(("arbitrary",), 48),
        name="kv_up",
    )(ckvb, kpe2, W['w_kv'], W['g_k_nope'], W['g_k_rope2'])


def _online_softmax(s, m_ref, l_ref):
    m_prev = m_ref[...]
    blocks = [s[:, c:c + LANE] for c in range(0, s.shape[1], LANE)]
    m_next = jnp.maximum(m_prev, jnp.max(functools.reduce(jnp.maximum, blocks), axis=-1, keepdims=True))
    blocks = [jnp.exp2(b - m_next) for b in blocks]
    alpha = jnp.exp2(m_prev - m_next)
    l_ref[...] = alpha * l_ref[...] + functools.reduce(jnp.add, blocks)
    m_ref[...] = m_next
    return jnp.concatenate(blocks, axis=-1).astype(BF16), alpha


N_CHAINS = 2
FLASH_UNROLLS = (8, 4, 2, 1)
FLASH_Q_PER_STEP = 8


def _flash_kernel(qt_ref, k_ref, vt_ref, o_ref, s_ref, smax_ref, m_ref, acc_ref, *, tq):
    def scores(qt, j, chain):
        start = pl.multiple_of(j * tq, tq)
        s = jnp.dot(k_ref[0, pl.ds(start, tq), :], qt, preferred_element_type=F32)
        s_ref[chain] = s
        smax_ref[chain] = jnp.max(s, axis=0, keepdims=True)

    def consume(j, chain, diagonal=False, between=None):
        s = s_ref[chain]
        if diagonal:
            kchunk = lax.broadcasted_iota(jnp.int32, (tq, tq), 0) // CHUNK
            qchunk = lax.broadcasted_iota(jnp.int32, (tq, tq), 1) // CHUNK
            s = jnp.where(kchunk <= qchunk, s, NEG_INF)
            s_max = jnp.max(s, axis=0, keepdims=True)
        else:
            s_max = smax_ref[chain]
        m_prev = m_ref[chain]
        m_next = jnp.maximum(m_prev, s_max)
        p = jnp.exp2(s - m_next).astype(BF16)
        alpha = jnp.exp2(m_prev - m_next)
        if between is not None:
            between()
        acc_ref[chain] = alpha * acc_ref[chain] + jnp.dot(vt_ref[0, j], p, preferred_element_type=F32)
        m_ref[chain] = m_next

    first = pl.program_id(1) * FLASH_Q_PER_STEP

    def q_tile(u, carry):
        n = first + u + 1
        qt = qt_ref[0, u]
        qt_next = qt_ref[0, jnp.minimum(u + 1, FLASH_Q_PER_STEP - 1)]
        m_ref[...] = jnp.full(m_ref.shape, -jnp.inf, F32)
        acc_ref[...] = jnp.zeros(acc_ref.shape, F32)

        def next_tile_scores(chain):
            scores(qt_next, chain, chain)

        def pair(jj):
            consume(2 * jj, 0, between=lambda: scores(qt, 2 * jj + 2, 0))
            consume(2 * jj + 1, 1, between=lambda: scores(qt, 2 * jj + 3, 1))

        def pairs(count):
            def body(jj, c):
                for r in range(count):
                    pair(count * jj + r)
                return c
            return body

        n_pairs = jnp.maximum(n // 2 - 1, 0)
        done = 0
        for count in FLASH_UNROLLS:
            iters = (n_pairs - done) // count
            lax.fori_loop(done // count, done // count + iters, pairs(count), 0)
            done = done + iters * count

        @pl.when(n % 2 == 0)
        def _():
            consume(n - 2, 0, between=lambda: next_tile_scores(0))
            consume(n - 1, 1, diagonal=True, between=lambda: next_tile_scores(1))

        @pl.when(jnp.logical_and(n % 2 == 1, n >= 3))
        def _():
            consume(n - 3, 0, between=lambda: scores(qt, n - 1, 0))
            consume(n - 2, 1, between=lambda: next_tile_scores(1))
            consume(n - 1, 0, diagonal=True, between=lambda: next_tile_scores(0))

        @pl.when(n == 1)
        def _():
            consume(0, 0, diagonal=True, between=lambda: next_tile_scores(0))
            next_tile_scores(1)

        m = jnp.maximum(m_ref[0], m_ref[1])
        acc = jnp.exp2(m_ref[0] - m) * acc_ref[0] + jnp.exp2(m_ref[1] - m) * acc_ref[1]
        out_t = acc[:V_HEAD] / acc[V_HEAD:V_HEAD + 1]
        o_ref[pl.ds(pl.multiple_of(u * tq, tq), tq), :] = out_t.T.astype(BF16)
        return carry

    scores(qt_ref[0, 0], 0, 0)

    @pl.when(first >= 1)
    def _():
        scores(qt_ref[0, 0], 1, 1)

    lax.fori_loop(0, FLASH_Q_PER_STEP, q_tile, 0)


def _flash_prompt(qt, k, vt):
    H, n_q, _, tq = qt.shape
    S = k.shape[1]
    per_step = FLASH_Q_PER_STEP
    return pl.pallas_call(
        functools.partial(_flash_kernel, tq=tq),
        grid=(H, n_q // per_step),
        in_specs=[
            pl.BlockSpec((1, per_step, QK_PAD, tq), lambda h, i: (h, i, 0, 0)),
            pl.BlockSpec((1, S, QK_PAD), lambda h, i: (h, 0, 0)),
            pl.BlockSpec((1, n_q, V_ROWS, tq), lambda h, i: (h, 0, 0, 0)),
        ],
        out_specs=pl.BlockSpec((per_step * tq, V_HEAD), lambda h, i: (i, h)),
        out_shape=jax.ShapeDtypeStruct((S, H * V_HEAD), BF16),
        scratch_shapes=[pltpu.VMEM((N_CHAINS, tq, tq), F32),
                        pltpu.VMEM((N_CHAINS, 1, tq), F32),
                        pltpu.VMEM((N_CHAINS, 1, tq), F32),
                        pltpu.VMEM((N_CHAINS, V_ROWS, tq), F32)],
        compiler_params=_params(("arbitrary", "arbitrary"), 48),
        name="flash_prompt",
    )(qt, k, vt)


Q_LAT = KV_RANK + LANE
DECODE_GROUPS = 4


def _qabsorb_kernel(q_ref, wkt_ref, gn_ref, gr2_ref, o_ref):
    q = q_ref[0]
    qn = (q[:, :QK_NOPE].astype(F32) * gn_ref[...]).astype(BF16)
    o_ref[0, :, :KV_RANK] = jnp.dot(qn, wkt_ref[...], preferred_element_type=F32).astype(BF16)
    o_ref[0, :, KV_RANK:] = (q[:, QK_NOPE:].astype(F32) * gr2_ref[...]).astype(BF16)


def _q_absorb(q, W):
    H, T, _ = q.shape
    return pl.pallas_call(
        _qabsorb_kernel,
        grid=(H,),
        in_specs=[
            pl.BlockSpec((1, T, QK_PAD), lambda h: (h, 0, 0)),
            pl.BlockSpec((QK_NOPE, KV_RANK), lambda h: (h, 0)),
            pl.BlockSpec((1, LANE), lambda h: (0, 0)),
            pl.BlockSpec((1, LANE), lambda h: (0, 0)),
        ],
        out_specs=pl.BlockSpec((1, T, Q_LAT), lambda h: (h, 0, 0)),
        out_shape=jax.ShapeDtypeStruct((H, T, Q_LAT), BF16),
        compiler_params=_params(("arbitrary",), 32),
        name="q_absorb",
    )(q, W['w_kT'], W['g_k_nope'], W['g_k_rope2'])


def _decode_kernel(q_ref, ckvn_ref, kpetn_ref, ckvc_ref, kpetc_ref, wkt_ref, wv_ref, o_ref,
                   m_ref, l_ref, acc_ref, p_ref, alpha_ref, ckvpad_ref, kpetpad_ref, *, n_tiles, t_new):
    j = pl.program_id(1)
    rows = N_HEADS * t_new
    qa = q_ref[...].reshape(rows, Q_LAT)

    def attend(ckv, kpet2, n_valid):
        n = ckv.shape[0]
        t = (lax.dot_general(qa[:, :KV_RANK], ckv, _NT, preferred_element_type=F32)
             + jnp.dot(qa[:, KV_RANK:], kpet2.astype(BF16), preferred_element_type=F32))
        kpet = kpet2[:QK_ROPE]
        ss_pe = jnp.sum(kpet * kpet, axis=0, keepdims=True)
        valid = lax.broadcasted_iota(jnp.int32, (1, n), 1) < n_valid
        group = N_HEADS // DECODE_GROUPS
        for hd in range(N_HEADS):
            if hd % group == 0:
                knt = lax.dot_general(wkt_ref[hd * QK_NOPE:(hd + group) * QK_NOPE, :], ckv, _NT,
                                      preferred_element_type=F32)
            kn = knt[(hd % group) * QK_NOPE:(hd % group + 1) * QK_NOPE]
            r = lax.rsqrt((jnp.sum(kn * kn, axis=0, keepdims=True) + ss_pe) * (1.0 / QK_HEAD) + EPS)
            s = t[hd * t_new:(hd + 1) * t_new] * r
            if n_valid < n:
                s = jnp.where(valid, s, NEG_INF)
            p, alpha = _online_softmax(s, m_ref.at[hd], l_ref.at[hd])
            p_ref[hd * t_new:(hd + 1) * t_new, :n] = p
            alpha_ref[hd * t_new:(hd + 1) * t_new, :] = alpha
        pv = jnp.dot(p_ref[:, :n], ckv, preferred_element_type=F32)
        alpha = alpha_ref[...]
        acc_ref[...] = jnp.concatenate([alpha] * (KV_RANK // LANE), axis=1) * acc_ref[...] + pv

    @pl.when(j == 0)
    def _():
        m_ref[...] = jnp.full(m_ref.shape, -jnp.inf, F32)
        l_ref[...] = jnp.zeros(l_ref.shape, F32)
        acc_ref[...] = jnp.zeros(acc_ref.shape, F32)
        ckvpad_ref[...] = jnp.zeros(ckvpad_ref.shape, BF16)
        ckvpad_ref[:t_new, :] = ckvn_ref[...]
        kpetpad_ref[...] = jnp.zeros(kpetpad_ref.shape, F32)
        kpetpad_ref[:, :t_new] = kpetn_ref[0]
        attend(ckvpad_ref[...], kpetpad_ref[...], t_new)

    tk = ckvc_ref.shape[1]
    attend(ckvc_ref[0].astype(BF16), kpetc_ref[0], tk)

    @pl.when(j == n_tiles - 1)
    def _():
        for hd in range(N_HEADS):
            l = jnp.sum(l_ref[hd], axis=-1, keepdims=True)
            lat = (acc_ref[hd * t_new:(hd + 1) * t_new, :] / l).astype(BF16)
            o_ref[:, hd * V_HEAD:(hd + 1) * V_HEAD] = jnp.dot(
                lat, wv_ref[hd], preferred_element_type=F32).astype(BF16)


def _decode_attn(qa, ckvb_new, kpet2_new, cache_ckv, cache_kpet2, W, tk):
    B, P, _ = cache_ckv.shape
    T = ckvb_new.shape[0]
    t_new = T // B
    n_tiles = P // tk
    rows = N_HEADS * t_new
    return pl.pallas_call(
        functools.partial(_decode_kernel, n_tiles=n_tiles, t_new=t_new),
        grid=(B, n_tiles),
        in_specs=[
            pl.BlockSpec((N_HEADS, t_new, Q_LAT), lambda b, j: (0, b, 0)),
            pl.BlockSpec((t_new, KV_RANK), lambda b, j: (b, 0)),
            pl.BlockSpec((1, LANE, t_new), lambda b, j: (b, 0, 0)),
            pl.BlockSpec((1, tk, KV_RANK), lambda b, j: (b, j, 0)),
            pl.BlockSpec((1, LANE, tk), lambda b, j: (b, 0, j)),
            pl.BlockSpec((N_HEADS * QK_NOPE, KV_RANK), lambda b, j: (0, 0)),
            pl.BlockSpec((N_HEADS, KV_RANK, V_HEAD), lambda b, j: (0, 0, 0)),
        ],
        out_specs=pl.BlockSpec((t_new, N_HEADS * V_HEAD), lambda b, j: (b, 0)),
        out_shape=jax.ShapeDtypeStruct((T, N_HEADS * V_HEAD), BF16),
        scratch_shapes=[
            pltpu.VMEM((N_HEADS, t_new, LANE), F32),
            pltpu.VMEM((N_HEADS, t_new, LANE), F32),
            pltpu.VMEM((rows, KV_RANK), F32),
            pltpu.VMEM((rows, tk), BF16),
            pltpu.VMEM((rows, LANE), F32),
            pltpu.VMEM((LANE, KV_RANK), BF16),
            pltpu.VMEM((LANE, LANE), F32),
        ],
        compiler_params=_params(("arbitrary", "arbitrary"), 56),
        name="decode_attn",
    )(qa, ckvb_new, kpet2_new, cache_ckv, cache_kpet2, W['w_kT'], W['w_v'])


CONV_ROWS = 64
CONV_COLS = 256
SUBLANE = 8


def _conv_kernel(h_ref, past_ref, wa_ref, wg_ref, ba_ref, bg_ref, wdw_ref, bdw_ref, gln_ref, bln_ref,
                 wpw_ref, bpw_ref, yb_ref, state_ref, buf_ref, shift_ref, y_ref, z_ref, *, tm, n_t, lagged):
    t = pl.program_id(1)
    hist = CONV_WIDTH - 1
    off = PAST_PAD - hist

    @pl.when(t == 0)
    def _():
        buf_ref[0:PAST_PAD, :] = jnp.zeros((PAST_PAD, CONV_CH), F32)
        buf_ref[off:PAST_PAD, :] = past_ref[0]
        if lagged:
            z_ref[...] = jnp.zeros(z_ref.shape, BF16)

    @pl.when(t > 0)
    def _():
        buf_ref[0:PAST_PAD, :] = buf_ref[tm:tm + PAST_PAD, :]

    h = h_ref[0]
    span = shift_ref.shape[1]
    rows = min(CONV_ROWS, tm)
    n_blocks = CONV_CH // CONV_COLS
    d_out = yb_ref.shape[-1]
    out_cols = d_out // n_blocks

    def project(ob):
        oc = slice(ob * out_cols, (ob + 1) * out_cols)
        yb = jnp.dot(z_ref[...], wpw_ref[:, oc], preferred_element_type=F32) + bpw_ref[:, oc]
        yb_ref[0, :, oc] = yb.astype(BF16)

    for cb, c0 in enumerate(range(0, CONV_CH, CONV_COLS)):
        cols = slice(c0, c0 + CONV_COLS)
        a = jnp.dot(h, wa_ref[:, cols], preferred_element_type=F32) + ba_ref[:, cols]
        g = jnp.dot(h, wg_ref[:, cols], preferred_element_type=F32) + bg_ref[:, cols]
        buf_ref[PAST_PAD:PAST_PAD + tm, cols] = a * jax.nn.sigmoid(g)
        if lagged:
            project(cb)
        for r in range(1, SUBLANE):
            shift_ref[r - 1, :, cols] = buf_ref[r:r + span, cols]
        for r0 in range(0, tm, rows):
            acc = jnp.zeros((rows, CONV_COLS), F32)
            for q in range(off, PAST_PAD + 1):
                r, base = q % SUBLANE, r0 + q - q % SUBLANE
                src = buf_ref if r == 0 else shift_ref.at[r - 1]
                acc = acc + wdw_ref[q - off:q - off + 1, cols] * src[base:base + rows, cols]
            y_ref[r0:r0 + rows, cols] = acc

    y = y_ref[...] + bdw_ref[...]
    yc = y - jnp.mean(y, axis=-1, keepdims=True)
    y = yc * lax.rsqrt(jnp.mean(yc * yc, axis=-1, keepdims=True) + EPS) * gln_ref[...] + bln_ref[...]
    z_ref[...] = (y * jax.nn.sigmoid(y)).astype(BF16)
    if not lagged:
        for ob in range(n_blocks):
            project(ob)

    @pl.when(t == n_t - 1)
    def _():
        state_ref[0] = buf_ref[tm + off:tm + PAST_PAD, :]


def _conv_branch(h3d, past, W, tm):
    B, S, D = h3d.shape
    n_t = S // tm
    hist = CONV_WIDTH - 1
    const = lambda b, t: (0, 0)
    lag = 1 if n_t >= 8 else 0
    return pl.pallas_call(
        functools.partial(_conv_kernel, tm=tm, n_t=n_t, lagged=bool(lag)),
        grid=(B, n_t + lag),
        in_specs=[
            pl.BlockSpec((1, tm, D), lambda b, t: (b, jnp.minimum(t, n_t - 1), 0)),
            pl.BlockSpec((1, hist, CONV_CH), lambda b, t: (b, 0, 0)),
            pl.BlockSpec((D, CONV_CH), const),
            pl.BlockSpec((D, CONV_CH), const),
            pl.BlockSpec((1, CONV_CH), const),
            pl.BlockSpec((1, CONV_CH), const),
            pl.BlockSpec((CONV_WIDTH, CONV_CH), const),
            pl.BlockSpec((1, CONV_CH), const),
            pl.BlockSpec((1, CONV_CH), const),
            pl.BlockSpec((1, CONV_CH), const),
            pl.BlockSpec((CONV_CH, D), const),
            pl.BlockSpec((1, D), const),
        ],
        out_specs=[
            pl.BlockSpec((1, tm, D), lambda b, t: (b, jnp.maximum(t - lag, 0), 0)),
            pl.BlockSpec((1, hist, CONV_CH), lambda b, t: (b, 0, 0)),
        ],
        out_shape=[
            jax.ShapeDtypeStruct((B, S, D), BF16),
            jax.ShapeDtypeStruct((B, hist, CONV_CH), F32),
        ],
        scratch_shapes=[pltpu.VMEM((PAST_PAD + tm, CONV_CH), F32),
                        pltpu.VMEM((SUBLANE - 1, PAST_PAD + tm - SUBLANE, CONV_CH), F32),
                        pltpu.VMEM((tm, CONV_CH), F32),
                        pltpu.VMEM((tm, CONV_CH), BF16)],
        compiler_params=_params(("arbitrary", "arbitrary"), 48),
        name="conv_branch",
    )(h3d, past, W['w_glu_a'], W['w_glu_g'], W['b_glu_a'], W['b_glu_g'], W['w_dw'], W['b_dw'],
      W['g_ln'], W['b_ln'], W['w_pw'], W['b_pw'])


def _mix_kernel(h_ref, attn_ref, yb_ref, wga_ref, wgb_ref, wao_ref, bga_ref, bgb_ref, m_ref):
    h = h_ref[...]
    ga = jax.nn.sigmoid(lax.dot_general(h, wga_ref[...], _NT, preferred_element_type=F32) + bga_ref[...])
    gb = jax.nn.sigmoid(lax.dot_general(h, wgb_ref[...], _NT, preferred_element_type=F32) + bgb_ref[...])
    ya = jnp.dot(attn_ref[...], wao_ref[...], preferred_element_type=F32)
    m_ref[...] = (ga * ya + gb * yb_ref[...].astype(F32)).astype(BF16)


def _mix(h, attn, yb, W, tm, tn):
    T, D = h.shape
    row = lambda i, j: (i, 0)
    col = lambda i, j: (0, j)
    blk = lambda i, j: (i, j)
    return pl.pallas_call(
        _mix_kernel,
        grid=(T // tm, D // tn),
        in_specs=[
            pl.BlockSpec((tm, D), row),
            pl.BlockSpec((tm, D), row),
            pl.BlockSpec((tm, tn), blk),
            pl.BlockSpec((tn, D), lambda i, j: (j, 0)),
            pl.BlockSpec((tn, D), lambda i, j: (j, 0)),
            pl.BlockSpec((D, tn), col),
            pl.BlockSpec((1, tn), col),
            pl.BlockSpec((1, tn), col),
        ],
        out_specs=pl.BlockSpec((tm, tn), blk),
        out_shape=jax.ShapeDtypeStruct((T, D), BF16),
        compiler_params=_params(("arbitrary", "arbitrary"), 48),
        name="gated_mix",
    )(h, attn, yb, W['w_gate_a_t'], W['w_gate_b_t'], W['w_attn_out'], W['b_gate_a'], W['b_gate_b'])


def _outproj_kernel(x_ref, m_ref, w_ref, o_ref):
    o_ref[...] = x_ref[...] + jnp.dot(m_ref[...], w_ref[...], preferred_element_type=F32)


def _out_proj(x2d, m, W, tm):
    T, D = x2d.shape
    return pl.pallas_call(
        _outproj_kernel,
        grid=(T // tm,),
        in_specs=[
            pl.BlockSpec((tm, D), lambda i: (i, 0)),
            pl.BlockSpec((tm, D), lambda i: (i, 0)),
            pl.BlockSpec((D, D), lambda i: (0, 0)),
        ],
        out_specs=pl.BlockSpec((tm, D), lambda i: (i, 0)),
        out_shape=jax.ShapeDtypeStruct((T, D), F32),
        compiler_params=_params(("arbitrary",), 48),
        name="out_proj",
    )(x2d, m, W['w_out'])


def _ffn_kernel(x_ref, g_ref, wg_ref, wu_ref, wd_ref, o_ref, h_ref):
    @pl.when(pl.program_id(1) == 0)
    def _():
        x = x_ref[...]
        h_ref[...] = (x * _rms_scale(x, x.shape[-1]) * g_ref[...]).astype(BF16)
        o_ref[...] = x

    h = h_ref[...]
    gate = jnp.dot(h, wg_ref[...], preferred_element_type=F32)
    up = jnp.dot(h, wu_ref[...], preferred_element_type=F32)
    act = (gate * jax.nn.sigmoid(gate) * up).astype(BF16)
    o_ref[...] += jnp.dot(act, wd_ref[...], preferred_element_type=F32)


def _ffn(x2d, W, tm, tf):
    T, D = x2d.shape
    d_ff = W['w_ffn_gate'].shape[1]
    return pl.pallas_call(
        _ffn_kernel,
        grid=(T // tm, d_ff // tf),
        in_specs=[
            pl.BlockSpec((tm, D), lambda i, j: (i, 0)),
            pl.BlockSpec((1, D), lambda i, j: (0, 0)),
            pl.BlockSpec((D, tf), lambda i, j: (0, j)),
            pl.BlockSpec((D, tf), lambda i, j: (0, j)),
            pl.BlockSpec((tf, D), lambda i, j: (j, 0)),
        ],
        out_specs=pl.BlockSpec((tm, D), lambda i, j: (i, 0)),
        out_shape=jax.ShapeDtypeStruct((T, D), F32),
        scratch_shapes=[pltpu.VMEM((tm, D), BF16)],
        compiler_params=_params(("arbitrary", "arbitrary"), 60),
        name="ffn",
    )(x2d, W['g_ffn'], W['w_ffn_gate'], W['w_ffn_up'], W['w_ffn_down'])


def _rot_half_cols(w):
    half = QK_ROPE // 2
    return jnp.concatenate([-w[..., half:], w[..., :half]], axis=-1)


def _prep_weights(lw):
    (g_mix_norm, w_in, b_glu, b_gate, g_q_a, w_q_up, g_q_norm, g_kv_a, w_kv_up, g_k_norm,
     w_attn_out, w_dw, b_dw, g_conv_ln, b_conv_ln, w_conv_out, b_conv_out, w_out,
     g_ffn_norm, w_ffn_gate, w_ffn_up, w_ffn_down) = lw
    D = w_in.shape[0]
    o_kv = Q_RANK
    o_pe = o_kv + KV_RANK
    o_glu = o_pe + QK_ROPE
    o_gate = o_glu + 2 * CONV_CH
    w_t = w_in.T
    w_pe_t = w_t[o_pe:o_glu]
    half = QK_ROPE // 2
    w_pe_rot_t = jnp.concatenate([-w_pe_t[half:], w_pe_t[:half]], axis=0)
    row = lambda v: v.reshape(1, -1).astype(F32)
    W = {
        'g_mix': row(g_mix_norm),
        'w_small_t': jnp.concatenate([w_t[:o_pe], w_pe_t, w_pe_t, w_pe_rot_t, w_pe_rot_t], axis=0).astype(BF16),
        'g_q_a': row(g_q_a),
        'g_kv_a': row(g_kv_a),
        'w_glu_a': w_t[o_glu:o_glu + CONV_CH].T.astype(BF16),
        'w_glu_g': w_t[o_glu + CONV_CH:o_gate].T.astype(BF16),
        'b_glu_a': row(b_glu[:CONV_CH]),
        'b_glu_g': row(b_glu[CONV_CH:]),
        'w_gate_a_t': w_t[o_gate:o_gate + D].astype(BF16),
        'w_gate_b_t': w_t[o_gate + D:].astype(BF16),
        'b_gate_a': row(b_gate[:D]),
        'b_gate_b': row(b_gate[D:]),
        'w_attn_out': w_attn_out.astype(BF16),
        'w_dw': w_dw.astype(F32),
        'b_dw': row(b_dw),
        'g_ln': row(g_conv_ln),
        'b_ln': row(b_conv_ln),
        'w_pw': w_conv_out.astype(BF16),
        'b_pw': row(b_conv_out),
        'w_out': w_out.astype(BF16),
        'g_ffn': row(g_ffn_norm),
        'w_ffn_gate': w_ffn_gate.astype(BF16),
        'w_ffn_up': w_ffn_up.astype(BF16),
        'w_ffn_down': w_ffn_down.astype(BF16),
    }
    wq = w_q_up.reshape(Q_RANK, N_PAIRS, 2, QK_HEAD)
    wq_nope = wq[..., :QK_NOPE].reshape(Q_RANK, N_PAIRS, 2 * QK_NOPE)
    wq_rope = wq[..., QK_NOPE:]
    W['w_q'] = jnp.concatenate(
        [wq_nope, wq_rope.reshape(Q_RANK, N_PAIRS, 2 * QK_ROPE),
         _rot_half_cols(wq_rope).reshape(Q_RANK, N_PAIRS, 2 * QK_ROPE)], axis=-1
    ).reshape(Q_RANK, N_PAIRS * 4 * LANE).astype(BF16)
    wkv = w_kv_up.reshape(KV_RANK, N_PAIRS, 2, QK_NOPE + V_HEAD)
    W['w_kv'] = jnp.concatenate(
        [wkv[..., :QK_NOPE].reshape(KV_RANK, N_PAIRS, 2 * QK_NOPE),
         wkv[..., QK_NOPE:].reshape(KV_RANK, N_PAIRS, 2 * V_HEAD)], axis=-1
    ).transpose(1, 0, 2).astype(BF16)
    W['w_kT'] = wkv[..., :QK_NOPE].reshape(KV_RANK, N_HEADS * QK_NOPE).T.astype(BF16)
    W['w_v'] = wkv[..., QK_NOPE:].reshape(KV_RANK, N_HEADS, V_HEAD).transpose(1, 0, 2).astype(BF16)
    dup = lambda v: jnp.concatenate([v, v]).reshape(1, LANE).astype(F32)
    W['g_q_nope'] = row(g_q_norm[:QK_NOPE])
    W['g_q_rope2'] = dup(g_q_norm[QK_NOPE:])
    W['g_k_nope'] = row(g_k_norm[:QK_NOPE])
    W['g_k_rope2'] = dup(g_k_norm[QK_NOPE:])
    return W


def _rope_tables(pos):
    inv_freq = 1.0 / (ROPE_THETA ** (jnp.arange(0, QK_ROPE, 2, dtype=F32) / QK_ROPE))
    ang = pos.astype(F32)[:, None] * inv_freq[None, :]
    return jnp.tile(jnp.cos(ang), (1, 4)), jnp.tile(jnp.sin(ang), (1, 4))


ROW_TILE = 512
WIDE_ROW_TILE = 1024
COL_TILE = 512
FFN_TILE = 512
CONV_ROW_TILE = 256
DECODE_KEY_TILE = 1024


def _layer(x, pos, past_ckv, past_kpe, past_conv, W):
    B, S, D = x.shape
    T = B * S
    x2d = x.reshape(T, D)
    c2, s2 = _rope_tables(pos)
    if B > 1:
        c2, s2 = jnp.tile(c2, (B, 1)), jnp.tile(s2, (B, 1))
    tm = min(T, ROW_TILE)
    tm_wide = min(T, WIDE_ROW_TILE)
    h, cq, ckv, ckvb, kpe, kpe2 = _inproj(x2d, c2, s2, W, tm)
    if past_ckv is None:
        qt = _q_up(cq, c2, s2, W, tm, transposed=True)
        k, vt = _kv_up(ckvb, kpe2, W, tm)
        attn = _flash_prompt(qt, k, vt)
        past_conv = jnp.zeros((B, CONV_WIDTH - 1, CONV_CH), x.dtype)
    else:
        q = _q_up(cq, c2, s2, W, tm, transposed=False)
        cache_kpet = jnp.swapaxes(past_kpe, 1, 2)
        cache_kpet2 = jnp.concatenate([cache_kpet, cache_kpet], axis=1)
        kpet2_new = jnp.swapaxes(kpe2.reshape(B, S, LANE), 1, 2)
        attn = _decode_attn(_q_absorb(q, W), ckvb, kpet2_new, past_ckv, cache_kpet2, W, DECODE_KEY_TILE)
    yb, conv_state = _conv_branch(h.reshape(B, S, D), past_conv, W, min(S, CONV_ROW_TILE))
    m = _mix(h, attn, yb.reshape(T, D), W, tm_wide, COL_TILE)
    x1 = _out_proj(x2d, m, W, tm)
    y = _ffn(x1, W, tm_wide, FFN_TILE)
    return (y.reshape(B, S, D), ckv.reshape(B, S, KV_RANK), kpe.reshape(B, S, QK_ROPE), conv_state)


def kernel(x_prompt, x_sample, cache_ckv, cache_kpe, state_conv, g_mix_norm, w_in, b_glu, b_gate, g_q_a,
           w_q_up, g_q_norm, g_kv_a, w_kv_up, g_k_norm, w_attn_out, w_dw, b_dw, g_conv_ln, b_conv_ln,
           w_conv_out, b_conv_out, w_out, g_ffn_norm, w_ffn_gate, w_ffn_up, w_ffn_down):
    weights = (g_mix_norm, w_in, b_glu, b_gate, g_q_a, w_q_up, g_q_norm, g_kv_a, w_kv_up, g_k_norm,
               w_attn_out, w_dw, b_dw, g_conv_ln, b_conv_ln, w_conv_out, b_conv_out, w_out,
               g_ffn_norm, w_ffn_gate, w_ffn_up, w_ffn_down)
    depth = w_in.shape[0]
    pos_prompt = jnp.arange(x_prompt.shape[1])
    pos_sample = cache_ckv.shape[2] + jnp.arange(x_sample.shape[1])
    y_prompt, y_sample = x_prompt, x_sample
    outs = [[] for _ in range(6)]
    for l in range(depth):
        W = _prep_weights(tuple(w[l] for w in weights))
        y_prompt, ckv, kpe, conv = _layer(y_prompt, pos_prompt, None, None, None, W)
        outs[0].append(ckv); outs[1].append(kpe); outs[2].append(conv)
        y_sample, ckv, kpe, conv = _layer(y_sample, pos_sample, cache_ckv[l], cache_kpe[l], state_conv[l], W)
        outs[3].append(ckv); outs[4].append(kpe); outs[5].append(conv)
    return (y_prompt, y_sample) + tuple(jnp.stack(o) for o in outs)
```
